```python
import math
import jax, jax.numpy as jnp
from jax import lax
import numpy as np

D_MODEL = 1024
BATCH = 8
SEQ = 4096
DEPTH = 1

CHUNK = 64
RMS_EPS = 1e-6
S5_WIDTH = D_MODEL // 2
S5_GROUP = 16
S5_GROUPS = S5_WIDTH // S5_GROUP
S5_STATE = 64
S5_DT_MIN = 1e-3
S5_DT_MAX = 1e-1
LRU_WIDTH = D_MODEL
LRU_HEADS = 16
LRU_HEAD_DIM = LRU_WIDTH // LRU_HEADS
CONV_WIDTH = 4
LRU_C = 8.0
LRU_A_MIN = 0.9
LRU_A_MAX = 0.999
IN_WIDTH = S5_WIDTH + 2 * LRU_WIDTH + 2 * D_MODEL
N_GROUPS = 4
EXPERTS_PER_GROUP = 8
N_EXPERTS = N_GROUPS * EXPERTS_PER_GROUP
TOP_K = 2
D_EXPERT = D_MODEL // 2
MOE_BLOCK = 128

kernel_name = "hybrid_s5_rglru_hmoe_block"


def rms_norm(x, g):
    xf = x.astype(jnp.float32)
    y = xf * lax.rsqrt(jnp.mean(xf * xf, axis=-1, keepdims=True) + RMS_EPS)
    return (y * g.astype(jnp.float32)).astype(x.dtype)


def _linear_combine(left, right):
    a_l, b_l = left
    a_r, b_r = right
    return a_r * a_l, a_r * b_l + b_r


def s5_mixer(u, lam_re, lam_im, log_dt, b_re, b_im, c_re, c_im, d_skip):
    bsz, seq, _ = u.shape
    uf = u.astype(jnp.float32).reshape(bsz, seq, S5_GROUPS, S5_GROUP)
    lam = lax.complex(lam_re.astype(jnp.float32), lam_im.astype(jnp.float32))
    dt = jnp.exp(log_dt.astype(jnp.float32))[:, None]
    lam_bar = jnp.exp(lam * dt)
    bmat = lax.complex(b_re.astype(jnp.float32), b_im.astype(jnp.float32))
    b_bar = ((lam_bar - 1.0) / lam)[..., None] * bmat
    bu = jnp.einsum('gpc,blgc->blgp', b_bar, uf.astype(jnp.complex64))
    a = jnp.broadcast_to(lam_bar, (1, seq, S5_GROUPS, S5_STATE))
    _, s = lax.associative_scan(_linear_combine, (a, bu), axis=1)
    cmat = lax.complex(c_re.astype(jnp.float32), c_im.astype(jnp.float32))
    y = jnp.real(jnp.einsum('gcp,blgp->blgc', cmat, s))
    y = y + d_skip.astype(jnp.float32).reshape(S5_GROUPS, S5_GROUP) * uf
    return y.reshape(bsz, seq, S5_WIDTH)


def rglru_mixer(x, conv_w, conv_b, w_r, b_r, w_i, b_i, lam):
    bsz, seq, width = x.shape
    xp = jnp.pad(x, ((0, 0), (CONV_WIDTH - 1, 0), (0, 0)))
    xc = conv_b + sum(xp[:, k:k + seq] * conv_w[k] for k in range(CONV_WIDTH))
    xh = xc.reshape(bsz, seq, LRU_HEADS, LRU_HEAD_DIM)
    r = jax.nn.sigmoid(jnp.einsum('blhi,hij->blhj', xh, w_r).reshape(bsz, seq, width) + b_r)
    i = jax.nn.sigmoid(jnp.einsum('blhi,hij->blhj', xh, w_i).reshape(bsz, seq, width) + b_i)
    log_a = -LRU_C * r.astype(jnp.float32) * jax.nn.softplus(-lam.astype(jnp.float32))
    a = jnp.exp(log_a)
    mult = jnp.sqrt(-jnp.expm1(2.0 * log_a))
    bx = mult * (i * xc).astype(jnp.float32)
    _, h = lax.associative_scan(_linear_combine, (a, bx), axis=1)
    return h.astype(x.dtype)


def hierarchical_moe(h, w_rg, b_rg, w_re, b_re, w_gate, w_up, w_down):
    bsz, seq, d = h.shape
    n_tok = bsz * seq
    hf = h.reshape(n_tok, d)
    g_prob = jax.nn.softmax((hf @ w_rg).astype(jnp.float32) + b_rg.astype(jnp.float32), axis=-1)
    g_idx = jnp.argmax(g_prob, axis=-1).astype(jnp.int32)
    p_g = jnp.max(g_prob, axis=-1, keepdims=True)
    e_logits = ((hf @ w_re).astype(jnp.float32) + b_re.astype(jnp.float32)).reshape(n_tok, N_GROUPS, EXPERTS_PER_GROUP)
    e_logits = jnp.take_along_axis(e_logits, g_idx[:, None, None], axis=1)[:, 0]
    e_prob = jax.nn.softmax(e_logits, axis=-1)
    top_p, top_i = lax.top_k(e_prob, TOP_K)
    gate = p_g * (top_p / jnp.sum(top_p, axis=-1, keepdims=True))
    expert_id = g_idx[:, None] * EXPERTS_PER_GROUP + top_i.astype(jnp.int32)

    n_assign = n_tok * TOP_K
    flat_e = expert_id.reshape(n_assign)
    flat_tok = jnp.repeat(jnp.arange(n_tok, dtype=jnp.int32), TOP_K)
    flat_w = gate.reshape(n_assign)
    order = jnp.argsort(flat_e)
    sorted_e = flat_e[order]
    counts = jnp.bincount(flat_e, length=N_EXPERTS)
    starts = jnp.cumsum(counts) - counts
    padded = ((counts + MOE_BLOCK - 1) // MOE_BLOCK) * MOE_BLOCK
    pad_ends = jnp.cumsum(padded)
    pad_starts = pad_ends - padded
    dest = pad_starts[sorted_e] + (jnp.arange(n_assign, dtype=jnp.int32) - starts[sorted_e])
    n_blocks = -(-n_assign // MOE_BLOCK) + N_EXPERTS
    slot_tok = jnp.full((n_blocks * MOE_BLOCK,), n_tok, jnp.int32).at[dest].set(flat_tok[order])
    slot_w = jnp.zeros((n_blocks * MOE_BLOCK,), jnp.float32).at[dest].set(flat_w[order])
    block_e = jnp.minimum(jnp.searchsorted(pad_ends, jnp.arange(n_blocks, dtype=jnp.int32) * MOE_BLOCK, side='right'), N_EXPERTS - 1)
    h_pad = jnp.concatenate([hf, jnp.zeros((1, d), hf.dtype)], axis=0)

    def expert_block(args):
        tok, wt, e = args
        xb = h_pad[tok]
        y = (jax.nn.silu(xb @ w_gate[e]) * (xb @ w_up[e])) @ w_down[e]
        return y * wt[:, None].astype(y.dtype)

    y = lax.map(expert_block, (slot_tok.reshape(n_blocks, MOE_BLOCK), slot_w.reshape(n_blocks, MOE_BLOCK), block_e))
    out = jnp.zeros((n_tok + 1, d), h.dtype).at[slot_tok].add(y.reshape(-1, d).astype(h.dtype))[:n_tok]
    return out.reshape(bsz, seq, d)


def setup_inputs(seed: int = 0) -> dict:
    key = jax.random.key(seed)
    ks = jax.random.split(key, 32)
    f32 = jnp.float32
    nrm = lambda k, shape, scale: jax.random.normal(k, shape, f32) * scale
    x = jax.random.normal(ks[0], (BATCH, SEQ, D_MODEL), f32)
    norm_mix = 1.0 + nrm(ks[1], (DEPTH, D_MODEL), 0.02)
    w_in = nrm(ks[2], (DEPTH, D_MODEL, IN_WIDTH), D_MODEL ** -0.5)
    s5_lam_re = -0.5 + nrm(ks[3], (DEPTH, S5_GROUPS, S5_STATE), 0.01)
    s5_lam_im = math.pi * jnp.arange(S5_STATE, dtype=f32) + nrm(ks[4], (DEPTH, S5_GROUPS, S5_STATE), 0.01)
    s5_log_dt = jax.random.uniform(ks[5], (DEPTH, S5_GROUPS), f32, math.log(S5_DT_MIN), math.log(S5_DT_MAX))
    s5_b_re = nrm(ks[6], (DEPTH, S5_GROUPS, S5_STATE, S5_GROUP), (2.0 * S5_GROUP) ** -0.5)
    s5_b_im = nrm(ks[7], (DEPTH, S5_GROUPS, S5_STATE, S5_GROUP), (2.0 * S5_GROUP) ** -0.5)
    s5_c_re = nrm(ks[8], (DEPTH, S5_GROUPS, S5_GROUP, S5_STATE), (2.0 * S5_STATE) ** -0.5)
    s5_c_im = nrm(ks[9], (DEPTH, S5_GROUPS, S5_GROUP, S5_STATE), (2.0 * S5_STATE) ** -0.5)
    s5_d = nrm(ks[10], (DEPTH, S5_WIDTH), 1.0)
    w_s5_out = nrm(ks[11], (DEPTH, S5_WIDTH, 2 * D_MODEL), S5_WIDTH ** -0.5)
    conv_w = nrm(ks[12], (DEPTH, CONV_WIDTH, LRU_WIDTH), CONV_WIDTH ** -0.5)
    conv_b = nrm(ks[13], (DEPTH, LRU_WIDTH), 0.01)
    lru_w_r = nrm(ks[14], (DEPTH, LRU_HEADS, LRU_HEAD_DIM, LRU_HEAD_DIM), LRU_HEAD_DIM ** -0.5)
    lru_b_r = nrm(ks[15], (DEPTH, LRU_WIDTH), 0.01)
    lru_w_i = nrm(ks[16], (DEPTH, LRU_HEADS, LRU_HEAD_DIM, LRU_HEAD_DIM), LRU_HEAD_DIM ** -0.5)
    lru_b_i = nrm(ks[17], (DEPTH, LRU_WIDTH), 0.01)
    a_c = jax.random.uniform(ks[18], (DEPTH, LRU_WIDTH), f32, LRU_A_MIN, LRU_A_MAX)
    a0 = a_c ** (1.0 / LRU_C)
    lru_lambda = jnp.log(a0) - jnp.log1p(-a0)
    w_lru_out = nrm(ks[19], (DEPTH, LRU_WIDTH, D_MODEL), LRU_WIDTH ** -0.5)
    w_o = nrm(ks[20], (DEPTH, D_MODEL, D_MODEL), D_MODEL ** -0.5)
    norm_ffn = 1.0 + nrm(ks[21], (DEPTH, D_MODEL), 0.02)
    router_group_w = nrm(ks[22], (DEPTH, D_MODEL, N_GROUPS), D_MODEL ** -0.5)
    router_group_b = nrm(ks[23], (DEPTH, N_GROUPS), 0.01)
    router_expert_w = nrm(ks[24], (DEPTH, D_MODEL, N_EXPERTS), D_MODEL ** -0.5)
    router_expert_b = nrm(ks[25], (DEPTH, N_EXPERTS), 0.01)
    expert_w_gate = nrm(ks[26], (DEPTH, N_EXPERTS, D_MODEL, D_EXPERT), D_MODEL ** -0.5)
    expert_w_up = nrm(ks[27], (DEPTH, N_EXPERTS, D_MODEL, D_EXPERT), D_MODEL ** -0.5)
    expert_w_down = nrm(ks[28], (DEPTH, N_EXPERTS, D_EXPERT, D_MODEL), D_EXPERT ** -0.5)
    norm_final = 1.0 + nrm(ks[29], (D_MODEL,), 0.02)
    return {"x": x, "norm_mix": norm_mix, "w_in": w_in,
            "s5_lam_re": s5_lam_re, "s5_lam_im": s5_lam_im, "s5_log_dt": s5_log_dt,
            "s5_b_re": s5_b_re, "s5_b_im": s5_b_im, "s5_c_re": s5_c_re, "s5_c_im": s5_c_im,
            "s5_d": s5_d, "w_s5_out": w_s5_out,
            "conv_w": conv_w, "conv_b": conv_b, "lru_w_r": lru_w_r, "lru_b_r": lru_b_r,
            "lru_w_i": lru_w_i, "lru_b_i": lru_b_i, "lru_lambda": lru_lambda, "w_lru_out": w_lru_out,
            "w_o": w_o, "norm_ffn": norm_ffn,
            "router_group_w": router_group_w, "router_group_b": router_group_b,
            "router_expert_w": router_expert_w, "router_expert_b": router_expert_b,
            "expert_w_gate": expert_w_gate, "expert_w_up": expert_w_up, "expert_w_down": expert_w_down,
            "norm_final": norm_final}


def reference(x, norm_mix, w_in, s5_lam_re, s5_lam_im, s5_log_dt, s5_b_re, s5_b_im, s5_c_re, s5_c_im,
              s5_d, w_s5_out, conv_w, conv_b, lru_w_r, lru_b_r, lru_w_i, lru_b_i, lru_lambda, w_lru_out,
              w_o, norm_ffn, router_group_w, router_group_b, router_expert_w, router_expert_b,
              expert_w_gate, expert_w_up, expert_w_down, norm_final):
    d = D_MODEL
    for l in range(DEPTH):
        h = rms_norm(x, norm_mix[l])
        proj = h @ w_in[l]
        o1 = S5_WIDTH
        o2 = o1 + LRU_WIDTH
        o3 = o2 + LRU_WIDTH
        o4 = o3 + d
        u_s5 = proj[..., :o1]
        x_lru = proj[..., o1:o2]
        g_lru = proj[..., o2:o3]
        gate_a = jax.nn.sigmoid(proj[..., o3:o4])
        gate_b = jax.nn.sigmoid(proj[..., o4:])
        y_s5 = s5_mixer(u_s5, s5_lam_re[l], s5_lam_im[l], s5_log_dt[l], s5_b_re[l], s5_b_im[l],
                        s5_c_re[l], s5_c_im[l], s5_d[l]).astype(x.dtype)
        z = jax.nn.gelu(y_s5) @ w_s5_out[l]
        y_a = z[..., :d] * jax.nn.sigmoid(z[..., d:])
        y_lru = rglru_mixer(x_lru, conv_w[l], conv_b[l], lru_w_r[l], lru_b_r[l], lru_w_i[l], lru_b_i[l],
                            lru_lambda[l]) * jax.nn.gelu(g_lru)
        y_b = y_lru @ w_lru_out[l]
        x = x + (gate_a * y_a + gate_b * y_b) @ w_o[l]
        x = x + hierarchical_moe(rms_norm(x, norm_ffn[l]), router_group_w[l], router_group_b[l],
                                 router_expert_w[l], router_expert_b[l], expert_w_gate[l],
                                 expert_w_up[l], expert_w_down[l])
    return rms_norm(x, norm_final)
```

```python
import functools
import math

import jax
import jax.numpy as jnp
from jax import lax
from jax.experimental import pallas as pl
from jax.experimental.pallas import tpu as pltpu

F32 = jnp.float32
BF16 = jnp.bfloat16

RMS_EPS = 1e-6
S5_GROUP = 16
S5_STATE = 64
LRU_HEADS = 16
CONV_WIDTH = 4
LRU_C = 8.0
N_GROUPS = 4
EXPERTS_PER_GROUP = 8
N_EXPERTS = N_GROUPS * EXPERTS_PER_GROUP

LANES = 128
SUBLANES = 8
MXU_DIM = 256
ROW_TILE = 512
TIME_CHUNK = 64
SLOT_BLOCK = 256
MOVE_TILE = 256
ROUTER_ROWS = 40
VMEM_LIMIT = 56 * 1024 * 1024


def _gelu(x):
    c = math.sqrt(2.0 / math.pi)
    return 0.5 * x * (1.0 + jnp.tanh(c * (x + 0.044715 * (x * x * x))))


def _sigmoid(x):
    return 1.0 / (1.0 + jnp.exp(-x))


def _params(sem):
    return pltpu.CompilerParams(dimension_semantics=sem, vmem_limit_bytes=VMEM_LIMIT)


def _inproj_kernel(x_ref, g_ref, w_ref, u_ref, xl_ref, gg_ref, ga_ref, gb_ref, *, s5w, lruw, d):
    x = x_ref[...]
    h = x * lax.rsqrt(jnp.mean(x * x, axis=-1, keepdims=True) + RMS_EPS) * g_ref[...]
    hb = h.astype(BF16)

    def proj(lo, hi):
        return jnp.dot(hb, w_ref[:, lo:hi], preferred_element_type=F32)

    o1 = s5w
    o2 = o1 + lruw
    o3 = o2 + lruw
    o4 = o3 + d
    u_ref[...] = proj(0, o1)
    xl_ref[...] = proj(o1, o2)
    gg_ref[...] = _gelu(proj(o2, o3)).astype(BF16)
    ga_ref[...] = _sigmoid(proj(o3, o4)).astype(BF16)
    gb_ref[...] = _sigmoid(proj(o4, o4 + d)).astype(BF16)


def _inproj(x2, g, w_in_b, s5w, lruw):
    t, d = x2.shape
    n = w_in_b.shape[1]
    tm = ROW_TILE
    row = lambda c: pl.BlockSpec((tm, c), lambda i: (i, 0))
    full = lambda a: pl.BlockSpec(a.shape, lambda i: (0,) * a.ndim)
    return pl.pallas_call(
        functools.partial(_inproj_kernel, s5w=s5w, lruw=lruw, d=d),
        grid=(t // tm,),
        in_specs=[row(d), full(g), full(w_in_b)],
        out_specs=[row(s5w), row(lruw), row(lruw), row(d), row(d)],
        out_shape=[jax.ShapeDtypeStruct((t, s5w), F32), jax.ShapeDtypeStruct((t, lruw), F32),
                   jax.ShapeDtypeStruct((t, lruw), BF16), jax.ShapeDtypeStruct((t, d), BF16),
                   jax.ShapeDtypeStruct((t, d), BF16)],
        compiler_params=_params(("arbitrary",)),
        name="inproj",
    )(x2, g, w_in_b)


def _s5_kernel(u_ref, bm_ref, lr_ref, li_ref, cm_ref, dk_ref, wo_ref, ga_ref, o_ref,
               state_ref, sbuf_ref, *, tq, nb):
    rows = nb * tq
    s5w = u_ref.shape[-1]
    d = o_ref.shape[-1]
    npair = bm_ref.shape[0]
    kper = LANES // (2 * S5_GROUP)

    @pl.when(pl.program_id(0) == 0)
    def _():
        state_ref[...] = jnp.zeros_like(state_ref)

    u2 = u_ref[...].reshape(rows, s5w)
    ub = u2.astype(BF16)
    for k in range(npair):
        kb = k // kper
        bu = jnp.dot(ub[:, LANES * kb:LANES * (kb + 1)], bm_ref[k], preferred_element_type=F32)
        sbuf_ref[2 * k] = bu[:, :LANES]
        sbuf_ref[2 * k + 1] = bu[:, LANES:]

    pairs_per_loop = 4
    for kk in range(npair // pairs_per_loop):
        k0 = kk * pairs_per_loop
        lr = [jnp.broadcast_to(lr_ref[k0 + j:k0 + j + 1, :], (nb, LANES)) for j in range(pairs_per_loop)]
        li = [jnp.broadcast_to(li_ref[k0 + j:k0 + j + 1, :], (nb, LANES)) for j in range(pairs_per_loop)]
        init = tuple(state_ref[2 * k0 + j] for j in range(2 * pairs_per_loop))

        def body(t, carry, k0=k0, lr=lr, li=li):
            new = []
            for j in range(pairs_per_loop):
                sre, sim = carry[2 * j], carry[2 * j + 1]
                bre = sbuf_ref[2 * (k0 + j), pl.ds(t, nb, stride=tq), :]
                bim = sbuf_ref[2 * (k0 + j) + 1, pl.ds(t, nb, stride=tq), :]
                nre = lr[j] * sre - li[j] * sim + bre
                nim = lr[j] * sim + li[j] * sre + bim
                sbuf_ref[2 * (k0 + j), pl.ds(t, nb, stride=tq), :] = nre
                sbuf_ref[2 * (k0 + j) + 1, pl.ds(t, nb, stride=tq), :] = nim
                new += [nre, nim]
            return tuple(new)

        fin = lax.fori_loop(0, tq, body, init)
        for j in range(2 * pairs_per_loop):
            state_ref[2 * k0 + j] = fin[j]

    ys = []
    for kb in range(npair // kper):
        acc = None
        for j in range(kper):
            k = kb * kper + j
            s = jnp.concatenate([sbuf_ref[2 * k], sbuf_ref[2 * k + 1]], axis=1).astype(BF16)
            p = jnp.dot(s, cm_ref[k], preferred_element_type=F32)
            acc = p if acc is None else acc + p
        ys.append(acc)
    y = jnp.concatenate(ys, axis=1) + dk_ref[...] * u2
    z = jnp.dot(_gelu(y).astype(BF16), wo_ref[...], preferred_element_type=F32)
    ya = z[:, :d] * _sigmoid(z[:, d:])
    ga = ga_ref[...].reshape(rows, d).astype(F32)
    o_ref[...] = (ya * ga).astype(BF16).reshape(nb, tq, d)


def _s5(u3, bm, lr, li, cm, dk, w_out_b, ga3):
    nb, seq, s5w = u3.shape
    d = ga3.shape[-1]
    tq = TIME_CHUNK
    nstate = bm.shape[0] * MXU_DIM
    blk = lambda c: pl.BlockSpec((nb, tq, c), lambda i: (0, i, 0))
    full = lambda a: pl.BlockSpec(a.shape, lambda i: (0,) * a.ndim)
    return pl.pallas_call(
        functools.partial(_s5_kernel, tq=tq, nb=nb),
        grid=(seq // tq,),
        in_specs=[blk(s5w), full(bm), full(lr), full(li), full(cm), full(dk), full(w_out_b), blk(d)],
        out_specs=blk(d),
        out_shape=jax.ShapeDtypeStruct((nb, seq, d), BF16),
        scratch_shapes=[pltpu.VMEM((nstate // LANES, nb, LANES), F32),
                        pltpu.VMEM((nstate // LANES, nb * tq, LANES), F32)],
        compiler_params=_params(("arbitrary",)),
        name="s5",
    )(u3, bm, lr, li, cm, dk, w_out_b, ga3)


def _lru_kernel(xl_ref, gg_ref, gb_ref, cw_ref, cb_ref, wr_ref, wi_ref, br_ref, bi_ref, sp_ref, wo_ref,
                o_ref, h_ref, xprev_ref, x2_ref, xc_ref, a_ref, b_ref, *, tq, nb):
    rows = nb * tq
    w = xl_ref.shape[-1]
    d = o_ref.shape[-1]

    @pl.when(pl.program_id(0) == 0)
    def _():
        h_ref[...] = jnp.zeros_like(h_ref)
        xprev_ref[...] = jnp.zeros_like(xprev_ref)

    nct = w // LANES
    xin = xl_ref[...].reshape(rows, w)
    for c in range(nct):
        x2_ref[c] = xin[:, LANES * c:LANES * (c + 1)]

    def conv_body(t, carry):
        out = []
        for c in range(nct):
            x1, x2, x3 = carry[3 * c:3 * c + 3]
            cs = slice(LANES * c, LANES * (c + 1))
            xt = x2_ref[c, pl.ds(t, nb, stride=tq), :]
            xc = (cb_ref[:, cs] + x3 * cw_ref[0:1, cs] + x2 * cw_ref[1:2, cs] + x1 * cw_ref[2:3, cs]
                  + xt * cw_ref[3:4, cs])
            xc_ref[c, pl.ds(t, nb, stride=tq), :] = xc
            out += [xt, x1, x2]
        return tuple(out)

    init = tuple(xprev_ref[i, c] for c in range(nct) for i in range(CONV_WIDTH - 1))
    fin = lax.fori_loop(0, tq, conv_body, init)
    for c in range(nct):
        for i in range(CONV_WIDTH - 1):
            xprev_ref[i, c] = fin[3 * c + i]

    ntile = wr_ref.shape[0]
    per = MXU_DIM // LANES
    for j in range(ntile):
        cs = slice(MXU_DIM * j, MXU_DIM * (j + 1))
        xc = jnp.concatenate([xc_ref[per * j + i] for i in range(per)], axis=1)
        xcb = xc.astype(BF16)
        r = _sigmoid(jnp.dot(xcb, wr_ref[j], preferred_element_type=F32) + br_ref[:, cs])
        ig = _sigmoid(jnp.dot(xcb, wi_ref[j], preferred_element_type=F32) + bi_ref[:, cs])
        log_a = (-LRU_C) * r * sp_ref[:, cs]
        th = jnp.tanh(log_a)
        mult = jnp.sqrt((-2.0 * th) / (1.0 - th))
        a = jnp.exp(log_a)
        b = mult * (ig * xc)
        for i in range(per):
            a_ref[per * j + i] = a[:, LANES * i:LANES * (i + 1)]
            b_ref[per * j + i] = b[:, LANES * i:LANES * (i + 1)]

    def scan_body(t, hs):
        out = []
        for c in range(nct):
            h = a_ref[c, pl.ds(t, nb, stride=tq), :] * hs[c] + b_ref[c, pl.ds(t, nb, stride=tq), :]
            b_ref[c, pl.ds(t, nb, stride=tq), :] = h
            out.append(h)
        return tuple(out)

    hfin = lax.fori_loop(0, tq, scan_body, tuple(h_ref[c] for c in range(nct)))
    for c in range(nct):
        h_ref[c] = hfin[c]

    y = jnp.concatenate([b_ref[c] for c in range(nct)], axis=1) * gg_ref[...].reshape(rows, w).astype(F32)
    yb = jnp.dot(y.astype(BF16), wo_ref[...], preferred_element_type=F32)
    gb = gb_ref[...].reshape(rows, d).astype(F32)
    o_ref[...] = (yb * gb).astype(BF16).reshape(nb, tq, d)


def _lru(xl3, gg3, gb3, cw, cb, wr, wi, br, bi, sp, w_out_b):
    nb, seq, w = xl3.shape
    d = gb3.shape[-1]
    tq = TIME_CHUNK
    blk = lambda c: pl.BlockSpec((nb, tq, c), lambda i: (0, i, 0))
    full = lambda a: pl.BlockSpec(a.shape, lambda i: (0,) * a.ndim)
    return pl.pallas_call(
        functools.partial(_lru_kernel, tq=tq, nb=nb),
        grid=(seq // tq,),
        in_specs=[blk(w), blk(w), blk(d), full(cw), full(cb), full(wr), full(wi), full(br), full(bi),
                  full(sp), full(w_out_b)],
        out_specs=blk(d),
        out_shape=jax.ShapeDtypeStruct((nb, seq, d), BF16),
        scratch_shapes=[pltpu.VMEM((w // LANES, nb, LANES), F32),
                        pltpu.VMEM((CONV_WIDTH - 1, w // LANES, nb, LANES), F32)]
        + [pltpu.VMEM((w // LANES, nb * tq, LANES), F32)] * 4,
        compiler_params=_params(("arbitrary",)),
        name="lru",
    )(xl3, gg3, gb3, cw, cb, wr, wi, br, bi, sp, w_out_b)


def _mixroute_kernel(x_ref, ya_ref, yb_ref, wo_ref, g_ref, wh_ref, wl_ref, rb_ref,
                     x1_ref, h2_ref, eid_ref, gate_ref):
    tm, d = x_ref.shape
    m = (ya_ref[...].astype(F32) + yb_ref[...].astype(F32)).astype(BF16)
    x1 = x_ref[...] + jnp.dot(m, wo_ref[...], preferred_element_type=F32)
    x1_ref[...] = x1
    h = x1 * lax.rsqrt(jnp.mean(x1 * x1, axis=-1, keepdims=True) + RMS_EPS) * g_ref[...]
    for j in range(d // LANES):
        h2_ref[pl.ds(j, tm, stride=d // LANES), :] = h[:, LANES * j:LANES * (j + 1)]

    hh = h.astype(BF16)
    hl = (h - hh.astype(F32)).astype(BF16)
    dn = (((1,), (1,)), ((), ()))
    lt = (lax.dot_general(wh_ref[...], hh, dn, preferred_element_type=F32)
          + lax.dot_general(wh_ref[...], hl, dn, preferred_element_type=F32)
          + lax.dot_general(wl_ref[...], hh, dn, preferred_element_type=F32))
    lt = lt + rb_ref[:, 0:1]

    gl = lt[0:N_GROUPS, :]
    gmax = jnp.max(gl, axis=0, keepdims=True)
    gsum = jnp.sum(jnp.exp(gl - gmax), axis=0, keepdims=True)
    p_g = 1.0 / gsum
    iota_g = lax.broadcasted_iota(jnp.int32, gl.shape, 0)
    g_idx = jnp.min(jnp.where(gl == gmax, iota_g, N_GROUPS), axis=0, keepdims=True)

    el = lt[SUBLANES:SUBLANES + EXPERTS_PER_GROUP, :]
    for g in range(1, N_GROUPS):
        lo = SUBLANES + EXPERTS_PER_GROUP * g
        el = jnp.where(g_idx == g, lt[lo:lo + EXPERTS_PER_GROUP, :], el)
    emax = jnp.max(el, axis=0, keepdims=True)
    ee = jnp.exp(el - emax)
    esum = jnp.sum(ee, axis=0, keepdims=True)
    iota_e = lax.broadcasted_iota(jnp.int32, el.shape, 0)
    i1 = jnp.min(jnp.where(el == emax, iota_e, EXPERTS_PER_GROUP), axis=0, keepdims=True)
    el2 = jnp.where(iota_e == i1, -jnp.inf, el)
    emax2 = jnp.max(el2, axis=0, keepdims=True)
    i2 = jnp.min(jnp.where(el2 == emax2, iota_e, EXPERTS_PER_GROUP), axis=0, keepdims=True)
    p1 = 1.0 / esum
    p2 = jnp.exp(emax2 - emax) / esum
    psum = p1 + p2
    g1 = p_g * (p1 / psum)
    g2 = p_g * (p2 / psum)
    e1 = g_idx * EXPERTS_PER_GROUP + i1
    e2 = g_idx * EXPERTS_PER_GROUP + i2
    eid_ref[...] = jnp.concatenate([e1, e2, jnp.zeros((SUBLANES - 2, tm), jnp.int32)], axis=0)
    gate_ref[...] = jnp.concatenate([g1, g2, jnp.zeros((SUBLANES - 2, tm), F32)], axis=0)


def _mixroute(x2, ya2, yb2, w_o_b, g, wr_hi, wr_lo, rbias):
    t, d = x2.shape
    tm = ROW_TILE
    row = lambda c: pl.BlockSpec((tm, c), lambda i: (i, 0))
    col = pl.BlockSpec((SUBLANES, tm), lambda i: (0, i))
    full = lambda a: pl.BlockSpec(a.shape, lambda i: (0,) * a.ndim)
    sub = d // LANES
    return pl.pallas_call(
        _mixroute_kernel,
        grid=(t // tm,),
        in_specs=[row(d), row(d), row(d), full(w_o_b), full(g), full(wr_hi), full(wr_lo), full(rbias)],
        out_specs=[row(d), pl.BlockSpec((tm * sub, LANES), lambda i: (i, 0)), col, col],
        out_shape=[jax.ShapeDtypeStruct((t, d), F32), jax.ShapeDtypeStruct((t * sub, LANES), F32),
                   jax.ShapeDtypeStruct((SUBLANES, t), jnp.int32), jax.ShapeDtypeStruct((SUBLANES, t), F32)],
        compiler_params=_params(("arbitrary",)),
        name="mixroute",
    )(x2, ya2, yb2, w_o_b, g, wr_hi, wr_lo, rbias)


def _plan_kernel(eid_ref, tri_ref, rank_ref, cnt_ref, carry_ref):
    tm = eid_ref.shape[1]

    @pl.when(pl.program_id(0) == 0)
    def _():
        carry_ref[...] = jnp.zeros_like(carry_ref)

    e0 = eid_ref[0:1, :]
    e1 = eid_ref[1:2, :]
    iota = lax.broadcasted_iota(jnp.int32, (N_EXPERTS, tm), 0)
    oh0 = iota == e0
    oh1 = iota == e1
    c = jnp.where(oh0 | oh1, 1.0, 0.0)
    pref = jnp.dot(c.astype(BF16), tri_ref[...], preferred_element_type=F32) + carry_ref[:, 0:1]
    r0 = jnp.sum(jnp.where(oh0, pref, 0.0), axis=0, keepdims=True)
    r1 = jnp.sum(jnp.where(oh1, pref, 0.0), axis=0, keepdims=True)
    rank_ref[...] = jnp.concatenate(
        [r0.astype(jnp.int32), r1.astype(jnp.int32), jnp.zeros((SUBLANES - 2, tm), jnp.int32)], axis=0)
    carry_ref[...] = carry_ref[...] + jnp.sum(c, axis=1, keepdims=True)
    cnt_ref[...] = carry_ref[...]


def _plan(eid, tri):
    t = eid.shape[1]
    tm = tri.shape[0]
    col = pl.BlockSpec((SUBLANES, tm), lambda i: (0, i))
    return pl.pallas_call(
        _plan_kernel,
        grid=(t // tm,),
        in_specs=[col, pl.BlockSpec(tri.shape, lambda i: (0, 0))],
        out_specs=[col, pl.BlockSpec((N_EXPERTS, LANES), lambda i: (0, 0))],
        out_shape=[jax.ShapeDtypeStruct((SUBLANES, t), jnp.int32),
                   jax.ShapeDtypeStruct((N_EXPERTS, LANES), F32)],
        scratch_shapes=[pltpu.VMEM((N_EXPERTS, LANES), F32)],
        compiler_params=_params(("arbitrary",)),
        name="plan",
    )(eid, tri)


def _row(ref, r):
    return ref.at[pl.ds(pl.multiple_of(r * SUBLANES, SUBLANES), SUBLANES)]


def _dispatch_kernel(d0_ref, d1_ref, cnt_ref, pst_ref, pen_ref, h2_ref, xs_ref, zero_ref, sem, zsem, *, tm):
    i = pl.program_id(0)
    base = i * tm

    def issue(r, c):
        src = _row(h2_ref, r)
        pltpu.make_async_copy(src, _row(xs_ref, d0_ref[base + r]), sem).start()
        pltpu.make_async_copy(src, _row(xs_ref, d1_ref[base + r]), sem).start()
        return c

    lax.fori_loop(0, tm, issue, 0)

    @pl.when(i == pl.num_programs(0) - 1)
    def _():
        zero_ref[...] = jnp.zeros_like(zero_ref)
        zrow = _row(zero_ref, 0)
        for e in range(N_EXPERTS):
            lo = pst_ref[e] + cnt_ref[e]
            hi = pen_ref[e]

            def zissue(s, c):
                pltpu.make_async_copy(zrow, _row(xs_ref, s), zsem).start()
                return c

            def zwait(s, c):
                pltpu.make_async_copy(zrow, _row(xs_ref, s), zsem).wait()
                return c

            lax.fori_loop(lo, hi, zissue, 0)
            lax.fori_loop(lo, hi, zwait, 0)

        brows = zero_ref.shape[0]
        nblk = xs_ref.shape[0] // brows

        def block(b):
            return xs_ref.at[pl.ds(pl.multiple_of(b * brows, brows), brows)]

        def bissue(b, c):
            pltpu.make_async_copy(zero_ref, block(b), zsem).start()
            return c

        def bwait(b, c):
            pltpu.make_async_copy(zero_ref, block(b), zsem).wait()
            return c

        first_unused = pen_ref[N_EXPERTS - 1] // (brows // SUBLANES)
        lax.fori_loop(first_unused, nblk, bissue, 0)
        lax.fori_loop(first_unused, nblk, bwait, 0)

    def drain(r, c):
        src = _row(h2_ref, r)
        pltpu.make_async_copy(src, _row(xs_ref, 0), sem).wait()
        pltpu.make_async_copy(src, _row(xs_ref, 0), sem).wait()
        return c

    lax.fori_loop(0, tm, drain, 0)


def _dispatch(d0, d1, cnt, pst, pen, h2, n_slots):
    sub = SUBLANES
    t = h2.shape[0] // sub
    tm = MOVE_TILE
    grid_spec = pltpu.PrefetchScalarGridSpec(
        num_scalar_prefetch=5,
        grid=(t // tm,),
        in_specs=[pl.BlockSpec((tm * sub, LANES), lambda i, *_: (i, 0))],
        out_specs=pl.BlockSpec(memory_space=pl.ANY),
        scratch_shapes=[pltpu.VMEM((SLOT_BLOCK * sub, LANES), F32), pltpu.SemaphoreType.DMA(()),
                        pltpu.SemaphoreType.DMA(())],
    )
    return pl.pallas_call(
        functools.partial(_dispatch_kernel, tm=tm),
        grid_spec=grid_spec,
        out_shape=jax.ShapeDtypeStruct((n_slots * sub, LANES), F32),
        compiler_params=_params(("arbitrary",)),
        name="dispatch",
    )(d0, d1, cnt, pst, pen, h2)


def _ffn_kernel(be_ref, nu_ref, xs_ref, wg_ref, wu_ref, wd_ref, ys_ref, wgb_ref, wub_ref, wdb_ref, *, blk):
    i = pl.program_id(0)
    sub = SUBLANES

    @pl.when(i < nu_ref[0])
    def _():
        prev = be_ref[jnp.maximum(i - 1, 0)]

        @pl.when((i == 0) | (be_ref[i] != prev))
        def _():
            wgb_ref[...] = wg_ref[...].astype(BF16)
            wub_ref[...] = wu_ref[...].astype(BF16)
            wdb_ref[...] = wd_ref[...].astype(BF16)

        x = jnp.concatenate([xs_ref[pl.ds(j, blk, stride=sub), :] for j in range(sub)], axis=1).astype(BF16)
        g = jnp.dot(x, wgb_ref[...], preferred_element_type=F32)
        u = jnp.dot(x, wub_ref[...], preferred_element_type=F32)
        a = (g * _sigmoid(g)) * u
        y = jnp.dot(a.astype(BF16), wdb_ref[...], preferred_element_type=F32)
        for j in range(sub):
            ys_ref[pl.ds(j, blk, stride=sub), :] = y[:, LANES * j:LANES * (j + 1)]

    @pl.when(i >= nu_ref[0])
    def _():
        ys_ref[...] = jnp.zeros_like(ys_ref)


def _ffn(block_e, n_used, xs, wg, wu, wd):
    sub = SUBLANES
    blk = SLOT_BLOCK
    n_blocks = xs.shape[0] // (sub * blk)
    ne, d, de = wg.shape

    def slot_map(i, be, nu):
        return (i, 0)

    def w_map(i, be, nu):
        return (be[jnp.minimum(i, nu[0] - 1)], 0, 0)

    grid_spec = pltpu.PrefetchScalarGridSpec(
        num_scalar_prefetch=2,
        grid=(n_blocks,),
        in_specs=[pl.BlockSpec((blk * sub, LANES), slot_map),
                  pl.BlockSpec((None, d, de), w_map), pl.BlockSpec((None, d, de), w_map),
                  pl.BlockSpec((None, de, d), w_map)],
        out_specs=pl.BlockSpec((blk * sub, LANES), slot_map),
        scratch_shapes=[pltpu.VMEM((d, de), BF16), pltpu.VMEM((d, de), BF16), pltpu.VMEM((de, d), BF16)],
    )
    return pl.pallas_call(
        functools.partial(_ffn_kernel, blk=blk),
        grid_spec=grid_spec,
        out_shape=jax.ShapeDtypeStruct(xs.shape, F32),
        compiler_params=_params(("arbitrary",)),
        name="ffn",
    )(block_e, n_used, xs, wg, wu, wd)


def _combine_kernel(d0_ref, d1_ref, x1_ref, gate_ref, ys_ref, g_ref, o_ref, y0_ref, y1_ref, sem, *, tm):
    i = pl.program_id(0)
    base = i * tm
    sub = SUBLANES

    def issue(r, c):
        pltpu.make_async_copy(_row(ys_ref, d0_ref[base + r]), _row(y0_ref, r), sem).start()
        pltpu.make_async_copy(_row(ys_ref, d1_ref[base + r]), _row(y1_ref, r), sem).start()
        return c

    lax.fori_loop(0, tm, issue, 0)

    def drain(r, c):
        pltpu.make_async_copy(_row(ys_ref, 0), _row(y0_ref, r), sem).wait()
        pltpu.make_async_copy(_row(ys_ref, 0), _row(y1_ref, r), sem).wait()
        return c

    lax.fori_loop(0, tm, drain, 0)

    y0 = jnp.concatenate([y0_ref[pl.ds(j, tm, stride=sub), :] for j in range(sub)], axis=1)
    y1 = jnp.concatenate([y1_ref[pl.ds(j, tm, stride=sub), :] for j in range(sub)], axis=1)
    gt = gate_ref[...]
    gcol = jnp.concatenate([gt] * (LANES // sub), axis=0).T
    x = x1_ref[...] + (gcol[:, 0:1] * y0 + gcol[:, 1:2] * y1)
    o_ref[...] = x * lax.rsqrt(jnp.mean(x * x, axis=-1, keepdims=True) + RMS_EPS) * g_ref[...]


def _combine(d0, d1, x1, gate, ys, g):
    t, d = x1.shape
    tm = MOVE_TILE
    sub = SUBLANES
    grid_spec = pltpu.PrefetchScalarGridSpec(
        num_scalar_prefetch=2,
        grid=(t // tm,),
        in_specs=[pl.BlockSpec((tm, d), lambda i, *_: (i, 0)),
                  pl.BlockSpec((sub, tm), lambda i, *_: (0, i)),
                  pl.BlockSpec(memory_space=pl.ANY),
                  pl.BlockSpec(g.shape, lambda i, *_: (0, 0))],
        out_specs=pl.BlockSpec((tm, d), lambda i, *_: (i, 0)),
        scratch_shapes=[pltpu.VMEM((tm * sub, LANES), F32), pltpu.VMEM((tm * sub, LANES), F32),
                        pltpu.SemaphoreType.DMA(())],
    )
    return pl.pallas_call(
        functools.partial(_combine_kernel, tm=tm),
        grid_spec=grid_spec,
        out_shape=jax.ShapeDtypeStruct((t, d), F32),
        compiler_params=_params(("arbitrary",)),
        name="combine",
    )(d0, d1, x1, gate, ys, g)


def _s5_tables(lam_re, lam_im, log_dt, b_re, b_im, c_re, c_im):
    ng, p = lam_re.shape
    npair = ng // 2
    kper = LANES // (2 * S5_GROUP)
    lam = lax.complex(lam_re, lam_im)
    dt = jnp.exp(log_dt)[:, None]
    lam_bar = jnp.exp(lam * dt)
    b_bar = ((lam_bar - 1.0) / lam)[..., None] * lax.complex(b_re, b_im)
    lr = jnp.real(lam_bar).reshape(npair, 2 * p)
    li = jnp.imag(lam_bar).reshape(npair, 2 * p)
    eye2 = jnp.eye(2, dtype=F32)
    sel = jax.nn.one_hot(jnp.arange(npair) % kper, kper, dtype=F32)

    bv = jnp.stack([jnp.real(b_bar), jnp.imag(b_bar)])
    bv = bv.reshape(2, npair, 2, p, S5_GROUP).transpose(1, 2, 4, 0, 3)
    bblk = bv[:, :, :, :, None, :] * eye2[None, :, None, None, :, None]
    bblk = bblk.reshape(npair, 2 * S5_GROUP, 4 * p)
    bm = (sel[:, :, None, None] * bblk[:, None]).reshape(npair, LANES, 4 * p)

    cv = jnp.stack([c_re, -c_im])
    cv = cv.reshape(2, npair, 2, S5_GROUP, p).transpose(1, 0, 2, 4, 3)
    cblk = cv[:, :, :, :, None, :] * eye2[None, None, :, None, :, None]
    cblk = cblk.reshape(npair, 4 * p, 2 * S5_GROUP)
    cm = (cblk[:, :, None, :] * sel[:, None, :, None]).reshape(npair, 4 * p, LANES)
    return bm.astype(BF16), lr, li, cm.astype(BF16)


def _blockdiag_tiles(w):
    nh, hi, ho = w.shape
    per = MXU_DIM // hi
    eye = jnp.eye(per, dtype=w.dtype)
    t = w.reshape(nh // per, per, hi, ho)[:, :, :, None, :] * eye[None, :, None, :, None]
    return t.reshape(nh // per, per * hi, per * ho)


def kernel(x, norm_mix, w_in, s5_lam_re, s5_lam_im, s5_log_dt, s5_b_re, s5_b_im, s5_c_re, s5_c_im, s5_d,
           w_s5_out, conv_w, conv_b, lru_w_r, lru_b_r, lru_w_i, lru_b_i, lru_lambda, w_lru_out, w_o, norm_ffn,
           router_group_w, router_group_b, router_expert_w, router_expert_b, expert_w_gate, expert_w_up,
           expert_w_down, norm_final):
    nb, seq, d = x.shape
    t = nb * seq
    depth = w_in.shape[0]
    s5w = s5_d.shape[-1]
    lruw = conv_b.shape[-1]
    x2 = x.reshape(t, d)

    for l in range(depth):
        u, xl, gg, ga, gb = _inproj(x2, norm_mix[l][None, :], w_in[l].astype(BF16), s5w, lruw)

        bm, lr, li, cm = _s5_tables(s5_lam_re[l], s5_lam_im[l], s5_log_dt[l], s5_b_re[l], s5_b_im[l],
                                    s5_c_re[l], s5_c_im[l])
        ya = _s5(u.reshape(nb, seq, s5w), bm, lr, li, cm, s5_d[l][None, :], w_s5_out[l].astype(BF16),
                 ga.reshape(nb, seq, d))

        sp = jax.nn.softplus(-lru_lambda[l])[None, :]
        yb = _lru(xl.reshape(nb, seq, lruw), gg.reshape(nb, seq, lruw), gb.reshape(nb, seq, d),
                  conv_w[l], conv_b[l][None, :], _blockdiag_tiles(lru_w_r[l]).astype(BF16),
                  _blockdiag_tiles(lru_w_i[l]).astype(BF16), lru_b_r[l][None, :], lru_b_i[l][None, :], sp,
                  w_lru_out[l].astype(BF16))

        wrt = jnp.zeros((ROUTER_ROWS, d), F32)
        wrt = wrt.at[0:N_GROUPS].set(router_group_w[l].T).at[SUBLANES:SUBLANES + N_EXPERTS].set(router_expert_w[l].T)
        wr_hi = wrt.astype(BF16)
        wr_lo = (wrt - wr_hi.astype(F32)).astype(BF16)
        rb = jnp.zeros((ROUTER_ROWS,), F32)
        rb = rb.at[0:N_GROUPS].set(router_group_b[l]).at[SUBLANES:SUBLANES + N_EXPERTS].set(router_expert_b[l])
        rbias = jnp.broadcast_to(rb[:, None], (ROUTER_ROWS, LANES))

        x1, h2, eid, gate = _mixroute(x2, ya.reshape(t, d), yb.reshape(t, d), w_o[l].astype(BF16),
                                      norm_ffn[l][None, :], wr_hi, wr_lo, rbias)

        tri = jnp.triu(jnp.ones((ROW_TILE, ROW_TILE), BF16), k=1)
        rank, cnt = _plan(eid, tri)

        counts = cnt[:, 0].astype(jnp.int32)
        padded = ((counts + SLOT_BLOCK - 1) // SLOT_BLOCK) * SLOT_BLOCK
        pad_ends = jnp.cumsum(padded)
        pad_starts = pad_ends - padded
        n_blocks = -(-(2 * t) // SLOT_BLOCK) + N_EXPERTS
        n_slots = n_blocks * SLOT_BLOCK
        dest = pad_starts[eid[0:2]] + rank[0:2]
        block_e = jnp.minimum(
            jnp.searchsorted(pad_ends, jnp.arange(n_blocks, dtype=jnp.int32) * SLOT_BLOCK, side='right'),
            N_EXPERTS - 1).astype(jnp.int32)
        n_used = (pad_ends[-1:] // SLOT_BLOCK).astype(jnp.int32)

        xs = _dispatch(dest[0], dest[1], counts, pad_starts.astype(jnp.int32), pad_ends.astype(jnp.int32),
                       h2, n_slots)
        ys = _ffn(block_e, n_used, xs, expert_w_gate[l], expert_w_up[l], expert_w_down[l])
        last = l == depth - 1
        gfin = norm_final[None, :] if last else jnp.ones((1, d), F32)
        x2 = _combine(dest[0], dest[1], x1, gate, ys, gfin)
        assert last, "the final RMSNorm is fused into the last layer's combine"
    return x2.reshape(nb, seq, d)
```

```python
import functools
import math

import jax
import jax.numpy as jnp
from jax import lax
from jax.experimental import pallas as pl
from jax.experimental.pallas import tpu as pltpu

F32 = jnp.float32
BF16 = jnp.bfloat16

RMS_EPS = 1e-6
S5_GROUP = 16
CONV_WIDTH = 4
LRU_C = 8.0
N_GROUPS = 4
EXPERTS_PER_GROUP = 8
N_EXPERTS = N_GROUPS * EXPERTS_PER_GROUP

LANES = 128
SUBLANES = 8
MXU_DIM = 256
TIME_CHUNK = 64
SCAN_UNROLL = 4
SLOT_BLOCK = 256
SLOT_SHIFT = 8
MOVE_TILE = 256
ROUTER_ROWS = 40
VMEM_LIMIT = 56 * 1024 * 1024


def _gelu(x):
    c = math.sqrt(2.0 / math.pi)
    return 0.5 * x * (1.0 + jnp.tanh(c * (x + 0.044715 * (x * x * x))))


def _sigmoid(x):
    return 1.0 / (1.0 + jnp.exp(-x))


def _rms(x, g):
    return x * lax.rsqrt(jnp.mean(x * x, axis=-1, keepdims=True) + RMS_EPS) * g


def _params():
    return pltpu.CompilerParams(dimension_semantics=("arbitrary",), vmem_limit_bytes=VMEM_LIMIT)


def _full(a):
    return pl.BlockSpec(a.shape, lambda i, *_: (0,) * a.ndim)


def _tile(t):
    return pl.ds(pl.multiple_of(t * SUBLANES, SUBLANES), SUBLANES)


def _inproj_kernel(x_ref, g_ref, p_ref, w_ref, u_ref, xl_ref, gg_ref, ga_ref, gb_ref, *, s5w, lruw):
    nb, tq, d = x_ref.shape
    h = _rms(x_ref[...].reshape(nb * tq, d), g_ref[...])
    hb = jnp.dot(p_ref[...], h.astype(BF16), preferred_element_type=F32).astype(BF16)

    def proj(lo, hi):
        return jnp.dot(hb, w_ref[:, lo:hi], preferred_element_type=F32)

    o1 = s5w
    o2 = o1 + lruw
    o3 = o2 + lruw
    o4 = o3 + d
    u_ref[...] = proj(0, o1)
    xl_ref[...] = proj(o1, o2)
    gg_ref[...] = _gelu(proj(o2, o3)).astype(BF16)
    ga_ref[...] = _sigmoid(proj(o3, o4)).astype(BF16)
    gb_ref[...] = _sigmoid(proj(o4, o4 + d)).astype(BF16)


def _inproj(x, g, perm, w_in_b, s5w, lruw):
    nb, seq, d = x.shape
    tq = TIME_CHUNK
    rows = nb * tq
    t = nb * seq
    row = lambda c: pl.BlockSpec((rows, c), lambda i: (i, 0))
    return pl.pallas_call(
        functools.partial(_inproj_kernel, s5w=s5w, lruw=lruw),
        grid=(seq // tq,),
        in_specs=[pl.BlockSpec((nb, tq, d), lambda i: (0, i, 0)), _full(g), _full(perm), _full(w_in_b)],
        out_specs=[row(s5w), row(lruw), row(lruw), row(d), row(d)],
        out_shape=[jax.ShapeDtypeStruct((t, s5w), F32), jax.ShapeDtypeStruct((t, lruw), F32),
                   jax.ShapeDtypeStruct((t, lruw), BF16), jax.ShapeDtypeStruct((t, d), BF16),
                   jax.ShapeDtypeStruct((t, d), BF16)],
        compiler_params=_params(),
        name="inproj",
    )(x, g, perm, w_in_b)


def _s5_kernel(u_ref, bm_ref, lr_ref, li_ref, cm_ref, dk_ref, wo_ref, ga_ref, o_ref,
               state_ref, sbuf_ref, *, tq, nb):
    d = o_ref.shape[-1]
    npair = bm_ref.shape[0]
    kper = LANES // (2 * S5_GROUP)

    @pl.when(pl.program_id(0) == 0)
    def _():
        state_ref[...] = jnp.zeros_like(state_ref)

    u2 = u_ref[...]
    ub = u2.astype(BF16)
    for k in range(npair):
        kb = k // kper
        bu = jnp.dot(ub[:, LANES * kb:LANES * (kb + 1)], bm_ref[k], preferred_element_type=F32)
        sbuf_ref[2 * k] = bu[:, :LANES]
        sbuf_ref[2 * k + 1] = bu[:, LANES:]

    pairs_per_loop = 4
    for kk in range(npair // pairs_per_loop):
        k0 = kk * pairs_per_loop
        lr = [jnp.broadcast_to(lr_ref[k0 + j:k0 + j + 1, :], (nb, LANES)) for j in range(pairs_per_loop)]
        li = [jnp.broadcast_to(li_ref[k0 + j:k0 + j + 1, :], (nb, LANES)) for j in range(pairs_per_loop)]
        init = tuple(state_ref[2 * k0 + j] for j in range(2 * pairs_per_loop))

        def body(it, carry, k0=k0, lr=lr, li=li):
            carry = list(carry)
            for s in range(SCAN_UNROLL):
                rows = _tile(it * SCAN_UNROLL + s)
                for j in range(pairs_per_loop):
                    sre, sim = carry[2 * j], carry[2 * j + 1]
                    nre = lr[j] * sre - li[j] * sim + sbuf_ref[2 * (k0 + j), rows, :]
                    nim = lr[j] * sim + li[j] * sre + sbuf_ref[2 * (k0 + j) + 1, rows, :]
                    sbuf_ref[2 * (k0 + j), rows, :] = nre
                    sbuf_ref[2 * (k0 + j) + 1, rows, :] = nim
                    carry[2 * j], carry[2 * j + 1] = nre, nim
            return tuple(carry)

        fin = lax.fori_loop(0, tq // SCAN_UNROLL, body, init)
        for j in range(2 * pairs_per_loop):
            state_ref[2 * k0 + j] = fin[j]

    ys = []
    for kb in range(npair // kper):
        acc = None
        for j in range(kper):
            k = kb * kper + j
            s = jnp.concatenate([sbuf_ref[2 * k], sbuf_ref[2 * k + 1]], axis=1).astype(BF16)
            p = jnp.dot(s, cm_ref[k], preferred_element_type=F32)
            acc = p if acc is None else acc + p
        ys.append(acc)
    y = jnp.concatenate(ys, axis=1) + dk_ref[...] * u2
    z = jnp.dot(_gelu(y).astype(BF16), wo_ref[...], preferred_element_type=F32)
    ya = z[:, :d] * _sigmoid(z[:, d:])
    o_ref[...] = (ya * ga_ref[...].astype(F32)).astype(BF16)


def _s5(u, bm, lr, li, cm, dk, w_out_b, ga, nb):
    t, s5w = u.shape
    d = ga.shape[-1]
    tq = TIME_CHUNK
    rows = nb * tq
    nstate = bm.shape[0] * MXU_DIM
    row = lambda c: pl.BlockSpec((rows, c), lambda i: (i, 0))
    return pl.pallas_call(
        functools.partial(_s5_kernel, tq=tq, nb=nb),
        grid=(t // rows,),
        in_specs=[row(s5w), _full(bm), _full(lr), _full(li), _full(cm), _full(dk), _full(w_out_b), row(d)],
        out_specs=row(d),
        out_shape=jax.ShapeDtypeStruct((t, d), BF16),
        scratch_shapes=[pltpu.VMEM((nstate // LANES, nb, LANES), F32),
                        pltpu.VMEM((nstate // LANES, rows, LANES), F32)],
        compiler_params=_params(),
        name="s5",
    )(u, bm, lr, li, cm, dk, w_out_b, ga)


def _lru_kernel(xl_ref, gg_ref, gb_ref, cw_ref, cb_ref, wr_ref, wi_ref, br_ref, bi_ref, sp_ref, wo_ref,
                o_ref, h_ref, xprev_ref, a_ref, b_ref, *, tq, nb):
    rows = nb * tq
    w = xl_ref.shape[-1]
    nct = w // LANES
    halo = (CONV_WIDTH - 1) * nb

    @pl.when(pl.program_id(0) == 0)
    def _():
        h_ref[...] = jnp.zeros_like(h_ref)
        xprev_ref[...] = jnp.zeros_like(xprev_ref)

    xin = xl_ref[...]
    xext = jnp.concatenate([xprev_ref[...], xin], axis=0)
    xprev_ref[...] = xin[rows - halo:, :]
    xc_all = cb_ref[...]
    for k in range(CONV_WIDTH):
        xc_all = xc_all + xext[k * nb:k * nb + rows, :] * cw_ref[k:k + 1, :]

    ntile = wr_ref.shape[0]
    per = MXU_DIM // LANES
    for j in range(ntile):
        cs = slice(MXU_DIM * j, MXU_DIM * (j + 1))
        xc = xc_all[:, cs]
        xcb = xc.astype(BF16)
        r = _sigmoid(jnp.dot(xcb, wr_ref[j], preferred_element_type=F32) + br_ref[:, cs])
        ig = _sigmoid(jnp.dot(xcb, wi_ref[j], preferred_element_type=F32) + bi_ref[:, cs])
        log_a = (-LRU_C) * r * sp_ref[:, cs]
        th = jnp.tanh(log_a)
        mult = jnp.sqrt((-2.0 * th) / (1.0 - th))
        a = jnp.exp(log_a)
        b = mult * (ig * xc)
        for i in range(per):
            a_ref[per * j + i] = a[:, LANES * i:LANES * (i + 1)]
            b_ref[per * j + i] = b[:, LANES * i:LANES * (i + 1)]

    def scan_body(it, hs):
        hs = list(hs)
        for s in range(SCAN_UNROLL):
            tr = _tile(it * SCAN_UNROLL + s)
            for c in range(nct):
                hs[c] = a_ref[c, tr, :] * hs[c] + b_ref[c, tr, :]
                b_ref[c, tr, :] = hs[c]
        return tuple(hs)

    hfin = lax.fori_loop(0, tq // SCAN_UNROLL, scan_body, tuple(h_ref[c] for c in range(nct)))
    for c in range(nct):
        h_ref[c] = hfin[c]

    y = jnp.concatenate([b_ref[c] for c in range(nct)], axis=1) * gg_ref[...].astype(F32)
    yb = jnp.dot(y.astype(BF16), wo_ref[...], preferred_element_type=F32)
    o_ref[...] = (yb * gb_ref[...].astype(F32)).astype(BF16)


def _lru(xl, gg, gb, cw, cb, wr, wi, br, bi, sp, w_out_b, nb):
    t, w = xl.shape
    d = gb.shape[-1]
    tq = TIME_CHUNK
    rows = nb * tq
    row = lambda c: pl.BlockSpec((rows, c), lambda i: (i, 0))
    return pl.pallas_call(
        functools.partial(_lru_kernel, tq=tq, nb=nb),
        grid=(t // rows,),
        in_specs=[row(w), row(w), row(d), _full(cw), _full(cb), _full(wr), _full(wi), _full(br), _full(bi),
                  _full(sp), _full(w_out_b)],
        out_specs=row(d),
        out_shape=jax.ShapeDtypeStruct((t, d), BF16),
        scratch_shapes=[pltpu.VMEM((w // LANES, nb, LANES), F32),
                        pltpu.VMEM(((CONV_WIDTH - 1) * nb, w), F32),
                        pltpu.VMEM((w // LANES, rows, LANES), F32),
                        pltpu.VMEM((w // LANES, rows, LANES), F32)],
        compiler_params=_params(),
        name="lru",
    )(xl, gg, gb, cw, cb, wr, wi, br, bi, sp, w_out_b)


def _mixroute_kernel(x_ref, ya_ref, yb_ref, pt_ref, wo_ref, g_ref, wh_ref, wl_ref, rb_ref,
                     x1_ref, h2_ref, eid_ref, gate_ref, cnt_ref):
    nb, tq, d = x_ref.shape
    tm = nb * tq
    m = (ya_ref[...].astype(F32) + yb_ref[...].astype(F32)).astype(BF16)
    m = jnp.dot(pt_ref[...], m, preferred_element_type=F32).astype(BF16)
    x1 = x_ref[...].reshape(tm, d) + jnp.dot(m, wo_ref[...], preferred_element_type=F32)
    x1_ref[...] = x1
    h = _rms(x1, g_ref[...])
    for j in range(d // LANES):
        h2_ref[pl.ds(j, tm, stride=d // LANES), :] = h[:, LANES * j:LANES * (j + 1)]

    hh = h.astype(BF16)
    hl = (h - hh.astype(F32)).astype(BF16)
    dn = (((1,), (1,)), ((), ()))
    lt = (lax.dot_general(wh_ref[...], hh, dn, preferred_element_type=F32)
          + lax.dot_general(wh_ref[...], hl, dn, preferred_element_type=F32)
          + lax.dot_general(wl_ref[...], hh, dn, preferred_element_type=F32))
    lt = lt + rb_ref[:, 0:1]

    gl = lt[0:N_GROUPS, :]
    gmax = jnp.max(gl, axis=0, keepdims=True)
    gsum = jnp.sum(jnp.exp(gl - gmax), axis=0, keepdims=True)
    p_g = 1.0 / gsum
    iota_g = lax.broadcasted_iota(jnp.int32, gl.shape, 0)
    g_idx = jnp.min(jnp.where(gl == gmax, iota_g, N_GROUPS), axis=0, keepdims=True)

    el = lt[SUBLANES:SUBLANES + EXPERTS_PER_GROUP, :]
    for g in range(1, N_GROUPS):
        lo = SUBLANES + EXPERTS_PER_GROUP * g
        el = jnp.where(g_idx == g, lt[lo:lo + EXPERTS_PER_GROUP, :], el)
    emax = jnp.max(el, axis=0, keepdims=True)
    esum = jnp.sum(jnp.exp(el - emax), axis=0, keepdims=True)
    iota_e = lax.broadcasted_iota(jnp.int32, el.shape, 0)
    i1 = jnp.min(jnp.where(el == emax, iota_e, EXPERTS_PER_GROUP), axis=0, keepdims=True)
    el2 = jnp.where(iota_e == i1, -jnp.inf, el)
    emax2 = jnp.max(el2, axis=0, keepdims=True)
    i2 = jnp.min(jnp.where(el2 == emax2, iota_e, EXPERTS_PER_GROUP), axis=0, keepdims=True)
    p1 = 1.0 / esum
    p2 = jnp.exp(emax2 - emax) / esum
    psum = p1 + p2
    g1 = p_g * (p1 / psum)
    g2 = p_g * (p2 / psum)
    e1 = g_idx * EXPERTS_PER_GROUP + i1
    e2 = g_idx * EXPERTS_PER_GROUP + i2
    eid_ref[...] = jnp.concatenate([e1, e2, jnp.zeros((SUBLANES - 2, tm), jnp.int32)], axis=0)
    gate_ref[...] = jnp.concatenate([g1, g2, jnp.zeros((SUBLANES - 2, tm), F32)], axis=0)

    @pl.when(pl.program_id(0) == 0)
    def _():
        cnt_ref[...] = jnp.zeros_like(cnt_ref)

    iota = lax.broadcasted_iota(jnp.int32, (N_EXPERTS, tm), 0)
    hits = jnp.where((iota == e1) | (iota == e2), 1.0, 0.0)
    cnt_ref[0:N_EXPERTS, :] = cnt_ref[0:N_EXPERTS, :] + jnp.sum(hits, axis=1, keepdims=True)


def _mixroute(x, ya, yb, perm_t, w_o_b, g, wr_hi, wr_lo, rbias):
    nb, seq, d = x.shape
    tq = TIME_CHUNK
    tm = nb * tq
    t = nb * seq
    sub = d // LANES
    row = lambda c: pl.BlockSpec((tm, c), lambda i: (i, 0))
    col = pl.BlockSpec((SUBLANES, tm), lambda i: (0, i))
    return pl.pallas_call(
        _mixroute_kernel,
        grid=(seq // tq,),
        in_specs=[pl.BlockSpec((nb, tq, d), lambda i: (0, i, 0)), row(d), row(d), _full(perm_t), _full(w_o_b),
                  _full(g), _full(wr_hi), _full(wr_lo), _full(rbias)],
        out_specs=[row(d), pl.BlockSpec((tm * sub, LANES), lambda i: (i, 0)), col, col,
                   pl.BlockSpec((LANES, LANES), lambda i: (0, 0))],
        out_shape=[jax.ShapeDtypeStruct((t, d), F32), jax.ShapeDtypeStruct((t * sub, LANES), F32),
                   jax.ShapeDtypeStruct((SUBLANES, t), jnp.int32), jax.ShapeDtypeStruct((SUBLANES, t), F32),
                   jax.ShapeDtypeStruct((LANES, LANES), F32)],
        compiler_params=_params(),
        name="mixroute",
    )(x, ya, yb, perm_t, w_o_b, g, wr_hi, wr_lo, rbias)


def _plan_kernel(eid_ref, cnt_ref, tri_ref, dest_ref, carry_ref, base_ref):
    tm = eid_ref.shape[1]

    @pl.when(pl.program_id(0) == 0)
    def _():
        carry_ref[...] = jnp.zeros_like(carry_ref)
        ci = cnt_ref[...].astype(jnp.int32)
        padded = (((ci + (SLOT_BLOCK - 1)) >> SLOT_SHIFT) << SLOT_SHIFT).astype(F32)
        pad_t = padded.T
        sub = lax.broadcasted_iota(jnp.int32, pad_t.shape, 0)
        lane = lax.broadcasted_iota(jnp.int32, pad_t.shape, 1)
        start = jnp.sum(jnp.where(lane < sub, pad_t, 0.0), axis=1, keepdims=True)
        base_ref[...] = jnp.broadcast_to(start, base_ref.shape)

    e0 = eid_ref[0:1, :]
    e1 = eid_ref[1:2, :]
    iota = lax.broadcasted_iota(jnp.int32, (N_EXPERTS, tm), 0)
    oh0 = iota == e0
    oh1 = iota == e1
    c = jnp.where(oh0 | oh1, 1.0, 0.0)
    slot = (jnp.dot(c.astype(BF16), tri_ref[...], preferred_element_type=F32)
            + carry_ref[:, 0:1] + base_ref[0:N_EXPERTS, 0:1])
    d0 = jnp.sum(jnp.where(oh0, slot, 0.0), axis=0, keepdims=True)
    d1 = jnp.sum(jnp.where(oh1, slot, 0.0), axis=0, keepdims=True)
    dest_ref[...] = jnp.concatenate(
        [d0.astype(jnp.int32), d1.astype(jnp.int32), jnp.zeros((SUBLANES - 2, tm), jnp.int32)], axis=0)
    carry_ref[...] = carry_ref[...] + jnp.sum(c, axis=1, keepdims=True)


def _plan(eid, cnt, tri):
    t = eid.shape[1]
    tm = tri.shape[0]
    col = pl.BlockSpec((SUBLANES, tm), lambda i: (0, i))
    return pl.pallas_call(
        _plan_kernel,
        grid=(t // tm,),
        in_specs=[col, _full(cnt), _full(tri)],
        out_specs=col,
        out_shape=jax.ShapeDtypeStruct((SUBLANES, t), jnp.int32),
        scratch_shapes=[pltpu.VMEM((N_EXPERTS, LANES), F32), pltpu.VMEM((LANES, LANES), F32)],
        compiler_params=_params(),
        name="plan",
    )(eid, cnt, tri)


def _row(ref, r):
    return ref.at[pl.ds(pl.multiple_of(r * SUBLANES, SUBLANES), SUBLANES)]


def _dispatch_kernel(d0_ref, d1_ref, cnt_ref, pst_ref, pen_ref, h2_ref, xs_ref, zero_ref, sem, zsem, *, tm):
    i = pl.program_id(0)
    base = i * tm

    def issue(r, c):
        src = _row(h2_ref, r)
        pltpu.make_async_copy(src, _row(xs_ref, d0_ref[base + r]), sem).start()
        pltpu.make_async_copy(src, _row(xs_ref, d1_ref[base + r]), sem).start()
        return c

    lax.fori_loop(0, tm, issue, 0)

    @pl.when(i == pl.num_programs(0) - 1)
    def _():
        zero_ref[...] = jnp.zeros_like(zero_ref)
        zrow = _row(zero_ref, 0)
        for e in range(N_EXPERTS):
            lo = pst_ref[e] + cnt_ref[e]
            hi = pen_ref[e]

            def zissue(s, c):
                pltpu.make_async_copy(zrow, _row(xs_ref, s), zsem).start()
                return c

            def zwait(s, c):
                pltpu.make_async_copy(zrow, _row(xs_ref, s), zsem).wait()
                return c

            lax.fori_loop(lo, hi, zissue, 0)
            lax.fori_loop(lo, hi, zwait, 0)

        brows = zero_ref.shape[0]
        nblk = xs_ref.shape[0] // brows

        def block(b):
            return xs_ref.at[pl.ds(pl.multiple_of(b * brows, brows), brows)]

        def bissue(b, c):
            pltpu.make_async_copy(zero_ref, block(b), zsem).start()
            return c

        def bwait(b, c):
            pltpu.make_async_copy(zero_ref, block(b), zsem).wait()
            return c

        first_unused = pen_ref[N_EXPERTS - 1] // (brows // SUBLANES)
        lax.fori_loop(first_unused, nblk, bissue, 0)
        lax.fori_loop(first_unused, nblk, bwait, 0)

    def drain(r, c):
        src = _row(h2_ref, r)
        pltpu.make_async_copy(src, _row(xs_ref, 0), sem).wait()
        pltpu.make_async_copy(src, _row(xs_ref, 0), sem).wait()
        return c

    lax.fori_loop(0, tm, drain, 0)


def _dispatch(d0, d1, cnt, pst, pen, h2, n_slots):
    sub = SUBLANES
    t = h2.shape[0] // sub
    tm = MOVE_TILE
    grid_spec = pltpu.PrefetchScalarGridSpec(
        num_scalar_prefetch=5,
        grid=(t // tm,),
        in_specs=[pl.BlockSpec((tm * sub, LANES), lambda i, *_: (i, 0))],
        out_specs=pl.BlockSpec(memory_space=pl.ANY),
        scratch_shapes=[pltpu.VMEM((SLOT_BLOCK * sub, LANES), F32), pltpu.SemaphoreType.DMA(()),
                        pltpu.SemaphoreType.DMA(())],
    )
    return pl.pallas_call(
        functools.partial(_dispatch_kernel, tm=tm),
        grid_spec=grid_spec,
        out_shape=jax.ShapeDtypeStruct((n_slots * sub, LANES), F32),
        compiler_params=_params(),
        name="dispatch",
    )(d0, d1, cnt, pst, pen, h2)


def _ffn_kernel(be_ref, nu_ref, xs_ref, wg_ref, wu_ref, wd_ref, ys_ref, wgb_ref, wub_ref, wdb_ref, *, blk):
    i = pl.program_id(0)
    sub = SUBLANES

    @pl.when(i < nu_ref[0])
    def _():
        prev = be_ref[jnp.maximum(i - 1, 0)]

        @pl.when((i == 0) | (be_ref[i] != prev))
        def _():
            wgb_ref[...] = wg_ref[...].astype(BF16)
            wub_ref[...] = wu_ref[...].astype(BF16)
            wdb_ref[...] = wd_ref[...].astype(BF16)

        x = jnp.concatenate([xs_ref[pl.ds(j, blk, stride=sub), :] for j in range(sub)], axis=1).astype(BF16)
        g = jnp.dot(x, wgb_ref[...], preferred_element_type=F32)
        u = jnp.dot(x, wub_ref[...], preferred_element_type=F32)
        a = (g * _sigmoid(g)) * u
        y = jnp.dot(a.astype(BF16), wdb_ref[...], preferred_element_type=F32)
        for j in range(sub):
            ys_ref[pl.ds(j, blk, stride=sub), :] = y[:, LANES * j:LANES * (j + 1)]

    @pl.when(i >= nu_ref[0])
    def _():
        ys_ref[...] = jnp.zeros_like(ys_ref)


def _ffn(block_e, n_used, xs, wg, wu, wd):
    sub = SUBLANES
    blk = SLOT_BLOCK
    n_blocks = xs.shape[0] // (sub * blk)
    ne, d, de = wg.shape

    def slot_map(i, be, nu):
        return (i, 0)

    def w_map(i, be, nu):
        return (be[jnp.minimum(i, nu[0] - 1)], 0, 0)

    grid_spec = pltpu.PrefetchScalarGridSpec(
        num_scalar_prefetch=2,
        grid=(n_blocks,),
        in_specs=[pl.BlockSpec((blk * sub, LANES), slot_map),
                  pl.BlockSpec((None, d, de), w_map), pl.BlockSpec((None, d, de), w_map),
                  pl.BlockSpec((None, de, d), w_map)],
        out_specs=pl.BlockSpec((blk * sub, LANES), slot_map),
        scratch_shapes=[pltpu.VMEM((d, de), BF16), pltpu.VMEM((d, de), BF16), pltpu.VMEM((de, d), BF16)],
    )
    return pl.pallas_call(
        functools.partial(_ffn_kernel, blk=blk),
        grid_spec=grid_spec,
        out_shape=jax.ShapeDtypeStruct(xs.shape, F32),
        compiler_params=_params(),
        name="ffn",
    )(block_e, n_used, xs, wg, wu, wd)


def _combine_kernel(d0_ref, d1_ref, x1_ref, gate_ref, ys_ref, g_ref, o_ref, y0_ref, y1_ref, sem, *, tm):
    i = pl.program_id(0)
    base = i * tm
    sub = SUBLANES

    def issue(r, c):
        pltpu.make_async_copy(_row(ys_ref, d0_ref[base + r]), _row(y0_ref, r), sem).start()
        pltpu.make_async_copy(_row(ys_ref, d1_ref[base + r]), _row(y1_ref, r), sem).start()
        return c

    lax.fori_loop(0, tm, issue, 0)

    def drain(r, c):
        pltpu.make_async_copy(_row(ys_ref, 0), _row(y0_ref, r), sem).wait()
        pltpu.make_async_copy(_row(ys_ref, 0), _row(y1_ref, r), sem).wait()
        return c

    lax.fori_loop(0, tm, drain, 0)

    y0 = jnp.concatenate([y0_ref[pl.ds(j, tm, stride=sub), :] for j in range(sub)], axis=1)
    y1 = jnp.concatenate([y1_ref[pl.ds(j, tm, stride=sub), :] for j in range(sub)], axis=1)
    gt = gate_ref[...]
    gcol = jnp.concatenate([gt] * (LANES // sub), axis=0).T
    x = x1_ref[...] + (gcol[:, 0:1] * y0 + gcol[:, 1:2] * y1)
    o_ref[...] = _rms(x, g_ref[...]).reshape(o_ref.shape)


def _combine(d0, d1, x1, gate, ys, g, nb, seq):
    t, d = x1.shape
    tm = MOVE_TILE
    tq = TIME_CHUNK
    sub = SUBLANES
    per = (nb * tq) // tm
    grid_spec = pltpu.PrefetchScalarGridSpec(
        num_scalar_prefetch=2,
        grid=(t // tm,),
        in_specs=[pl.BlockSpec((tm, d), lambda i, *_: (i, 0)),
                  pl.BlockSpec((sub, tm), lambda i, *_: (0, i)),
                  pl.BlockSpec(memory_space=pl.ANY),
                  _full(g)],
        out_specs=pl.BlockSpec((tm // tq, tq, d), lambda i, *_: (i % per, i // per, 0)),
        scratch_shapes=[pltpu.VMEM((tm * sub, LANES), F32), pltpu.VMEM((tm * sub, LANES), F32),
                        pltpu.SemaphoreType.DMA(())],
    )
    return pl.pallas_call(
        functools.partial(_combine_kernel, tm=tm),
        grid_spec=grid_spec,
        out_shape=jax.ShapeDtypeStruct((nb, seq, d), F32),
        compiler_params=_params(),
        name="combine",
    )(d0, d1, x1, gate, ys, g)


def _s5_tables(lam_re, lam_im, log_dt, b_re, b_im, c_re, c_im):
    ng, p = lam_re.shape
    npair = ng // 2
    kper = LANES // (2 * S5_GROUP)
    lam = lax.complex(lam_re, lam_im)
    dt = jnp.exp(log_dt)[:, None]
    lam_bar = jnp.exp(lam * dt)
    b_bar = ((lam_bar - 1.0) / lam)[..., None] * lax.complex(b_re, b_im)
    lr = jnp.real(lam_bar).reshape(npair, 2 * p)
    li = jnp.imag(lam_bar).reshape(npair, 2 * p)
    eye2 = jnp.eye(2, dtype=F32)
    sel = jax.nn.one_hot(jnp.arange(npair) % kper, kper, dtype=F32)

    bv = jnp.stack([jnp.real(b_bar), jnp.imag(b_bar)])
    bv = bv.reshape(2, npair, 2, p, S5_GROUP).transpose(1, 2, 4, 0, 3)
    bblk = bv[:, :, :, :, None, :] * eye2[None, :, None, None, :, None]
    bblk = bblk.reshape(npair, 2 * S5_GROUP, 4 * p)
    bm = (sel[:, :, None, None] * bblk[:, None]).reshape(npair, LANES, 4 * p)

    cv = jnp.stack([c_re, -c_im])
    cv = cv.reshape(2, npair, 2, S5_GROUP, p).transpose(1, 0, 2, 4, 3)
    cblk = cv[:, :, :, :, None, :] * eye2[None, None, :, None, :, None]
    cblk = cblk.reshape(npair, 4 * p, 2 * S5_GROUP)
    cm = (cblk[:, :, None, :] * sel[:, None, :, None]).reshape(npair, 4 * p, LANES)
    return bm.astype(BF16), lr, li, cm.astype(BF16)


def _blockdiag_tiles(w):
    nh, hi, ho = w.shape
    per = MXU_DIM // hi
    eye = jnp.eye(per, dtype=w.dtype)
    t = w.reshape(nh // per, per, hi, ho)[:, :, :, None, :] * eye[None, :, None, :, None]
    return t.reshape(nh // per, per * hi, per * ho)


def kernel(x, norm_mix, w_in, s5_lam_re, s5_lam_im, s5_log_dt, s5_b_re, s5_b_im, s5_c_re, s5_c_im, s5_d,
           w_s5_out, conv_w, conv_b, lru_w_r, lru_b_r, lru_w_i, lru_b_i, lru_lambda, w_lru_out, w_o, norm_ffn,
           router_group_w, router_group_b, router_expert_w, router_expert_b, expert_w_gate, expert_w_up,
           expert_w_down, norm_final):
    nb, seq, d = x.shape
    t = nb * seq
    assert w_in.shape[0] == 1, "one layer: the final RMSNorm is fused into the layer's combine kernel"
    l = 0
    s5w = s5_d.shape[-1]
    lruw = conv_b.shape[-1]
    rows = nb * TIME_CHUNK

    r = jnp.arange(rows)
    perm = jax.nn.one_hot((r % nb) * TIME_CHUNK + r // nb, rows, dtype=BF16)

    u, xl, gg, ga, gb = _inproj(x, norm_mix[l][None, :], perm, w_in[l].astype(BF16), s5w, lruw)

    bm, lr, li, cm = _s5_tables(s5_lam_re[l], s5_lam_im[l], s5_log_dt[l], s5_b_re[l], s5_b_im[l],
                                s5_c_re[l], s5_c_im[l])
    ya = _s5(u, bm, lr, li, cm, s5_d[l][None, :], w_s5_out[l].astype(BF16), ga, nb)

    sp = jax.nn.softplus(-lru_lambda[l])[None, :]
    yb = _lru(xl, gg, gb, conv_w[l], conv_b[l][None, :], _blockdiag_tiles(lru_w_r[l]).astype(BF16),
              _blockdiag_tiles(lru_w_i[l]).astype(BF16), lru_b_r[l][None, :], lru_b_i[l][None, :], sp,
              w_lru_out[l].astype(BF16), nb)

    gap = jnp.zeros((SUBLANES - N_GROUPS, d), F32)
    wrt = jnp.concatenate([router_group_w[l].T, gap, router_expert_w[l].T], axis=0)
    wr_hi = wrt.astype(BF16)
    wr_lo = (wrt - wr_hi.astype(F32)).astype(BF16)
    rb = jnp.concatenate([router_group_b[l], jnp.zeros((SUBLANES - N_GROUPS,), F32), router_expert_b[l]])
    rbias = jnp.broadcast_to(rb[:, None], (ROUTER_ROWS, LANES))

    x1, h2, eid, gate, cnt = _mixroute(x, ya, yb, perm.T, w_o[l].astype(BF16), norm_ffn[l][None, :],
                                       wr_hi, wr_lo, rbias)

    tri = jnp.triu(jnp.ones((rows, rows), BF16), k=1)
    dest = _plan(eid, cnt, tri)

    counts = cnt[:N_EXPERTS, 0].astype(jnp.int32)
    padded = ((counts + SLOT_BLOCK - 1) // SLOT_BLOCK) * SLOT_BLOCK
    pad_ends = jnp.cumsum(padded)
    pad_starts = pad_ends - padded
    n_blocks = -(-(2 * t) // SLOT_BLOCK) + N_EXPERTS
    n_slots = n_blocks * SLOT_BLOCK
    block_start = jnp.arange(n_blocks, dtype=jnp.int32) * SLOT_BLOCK
    block_e = jnp.minimum(jnp.sum((pad_ends[None, :] <= block_start[:, None]).astype(jnp.int32), axis=1),
                          N_EXPERTS - 1)
    n_used = pad_ends[-1:] // SLOT_BLOCK

    xs = _dispatch(dest[0], dest[1], counts, pad_starts, pad_ends, h2, n_slots)
    ys = _ffn(block_e, n_used, xs, expert_w_gate[l], expert_w_up[l], expert_w_down[l])
    return _combine(dest[0], dest[1], x1, gate, ys, norm_final[None, :], nb, seq)
```

```python
import functools
import math

import jax
import jax.numpy as jnp
from jax import lax
from jax.experimental import pallas as pl
from jax.experimental.pallas import tpu as pltpu

F32 = jnp.float32
BF16 = jnp.bfloat16

RMS_EPS = 1e-6
S5_GROUP = 16
CONV_WIDTH = 4
LRU_C = 8.0
N_GROUPS = 4
EXPERTS_PER_GROUP = 8
N_EXPERTS = N_GROUPS * EXPERTS_PER_GROUP

LANES = 128
SUBLANES = 8
MXU_DIM = 256
TIME_CHUNK = 64
SCAN_UNROLL = 4
SLOT_BLOCK = 256
SLOT_SHIFT = 8
MOVE_TILE = 256
ROUTER_ROWS = 40
VMEM_LIMIT = 56 * 1024 * 1024


def _gelu(x):
    c = math.sqrt(2.0 / math.pi)
    return 0.5 * x * (1.0 + jnp.tanh(c * (x + 0.044715 * (x * x * x))))


def _sigmoid(x):
    return 1.0 / (1.0 + jnp.exp(-x))


def _rms(x, g):
    return x * lax.rsqrt(jnp.mean(x * x, axis=-1, keepdims=True) + RMS_EPS) * g


def _params():
    return pltpu.CompilerParams(dimension_semantics=("arbitrary",), vmem_limit_bytes=VMEM_LIMIT)


def _full(a):
    return pl.BlockSpec(a.shape, lambda i, *_: (0,) * a.ndim)


def _tile(t):
    return pl.ds(pl.multiple_of(t * SUBLANES, SUBLANES), SUBLANES)


def _inproj_kernel(x_ref, g_ref, p_ref, w_ref, u_ref, xl_ref, gg_ref, ga_ref, gb_ref, *, s5w, lruw):
    nb, tq, d = x_ref.shape
    h = _rms(x_ref[...].reshape(nb * tq, d), g_ref[...])
    hb = jnp.dot(p_ref[...], h.astype(BF16), preferred_element_type=F32).astype(BF16)

    def proj(lo, hi):
        return jnp.dot(hb, w_ref[:, lo:hi], preferred_element_type=F32)

    o1 = s5w
    o2 = o1 + lruw
    o3 = o2 + lruw
    o4 = o3 + d
    u_ref[...] = proj(0, o1)
    xl_ref[...] = proj(o1, o2)
    gg_ref[...] = _gelu(proj(o2, o3)).astype(BF16)
    ga_ref[...] = _sigmoid(proj(o3, o4)).astype(BF16)
    gb_ref[...] = _sigmoid(proj(o4, o4 + d)).astype(BF16)


def _inproj(x, g, perm, w_in_b, s5w, lruw):
    nb, seq, d = x.shape
    tq = TIME_CHUNK
    rows = nb * tq
    t = nb * seq
    row = lambda c: pl.BlockSpec((rows, c), lambda i: (i, 0))
    return pl.pallas_call(
        functools.partial(_inproj_kernel, s5w=s5w, lruw=lruw),
        grid=(seq // tq,),
        in_specs=[pl.BlockSpec((nb, tq, d), lambda i: (0, i, 0)), _full(g), _full(perm), _full(w_in_b)],
        out_specs=[row(s5w), row(lruw), row(lruw), row(d), row(d)],
        out_shape=[jax.ShapeDtypeStruct((t, s5w), F32), jax.ShapeDtypeStruct((t, lruw), F32),
                   jax.ShapeDtypeStruct((t, lruw), BF16), jax.ShapeDtypeStruct((t, d), BF16),
                   jax.ShapeDtypeStruct((t, d), BF16)],
        compiler_params=_params(),
        name="inproj",
    )(x, g, perm, w_in_b)


def _s5_kernel(u_ref, bm_ref, lr_ref, li_ref, cm_ref, dk_ref, wo_ref, ga_ref, o_ref,
               state_ref, sbuf_ref, *, tq, nb):
    d = o_ref.shape[-1]
    npair = bm_ref.shape[0]
    kper = LANES // (2 * S5_GROUP)

    @pl.when(pl.program_id(0) == 0)
    def _():
        state_ref[...] = jnp.zeros_like(state_ref)

    u2 = u_ref[...]
    ub = u2.astype(BF16)
    for k in range(npair):
        kb = k // kper
        bu = jnp.dot(ub[:, LANES * kb:LANES * (kb + 1)], bm_ref[k], preferred_element_type=F32)
        sbuf_ref[2 * k] = bu[:, :LANES]
        sbuf_ref[2 * k + 1] = bu[:, LANES:]

    pairs_per_loop = 4
    for kk in range(npair // pairs_per_loop):
        k0 = kk * pairs_per_loop
        lr = [jnp.broadcast_to(lr_ref[k0 + j:k0 + j + 1, :], (nb, LANES)) for j in range(pairs_per_loop)]
        li = [jnp.broadcast_to(li_ref[k0 + j:k0 + j + 1, :], (nb, LANES)) for j in range(pairs_per_loop)]
        init = tuple(state_ref[2 * k0 + j] for j in range(2 * pairs_per_loop))

        def body(it, carry, k0=k0, lr=lr, li=li):
            carry = list(carry)
            for s in range(SCAN_UNROLL):
                rows = _tile(it * SCAN_UNROLL + s)
                for j in range(pairs_per_loop):
                    sre, sim = carry[2 * j], carry[2 * j + 1]
                    nre = lr[j] * sre - li[j] * sim + sbuf_ref[2 * (k0 + j), rows, :]
                    nim = lr[j] * sim + li[j] * sre + sbuf_ref[2 * (k0 + j) + 1, rows, :]
                    sbuf_ref[2 * (k0 + j), rows, :] = nre
                    sbuf_ref[2 * (k0 + j) + 1, rows, :] = nim
                    carry[2 * j], carry[2 * j + 1] = nre, nim
            return tuple(carry)

        fin = lax.fori_loop(0, tq // SCAN_UNROLL, body, init)
        for j in range(2 * pairs_per_loop):
            state_ref[2 * k0 + j] = fin[j]

    ys = []
    for kb in range(npair // kper):
        acc = None
        for j in range(kper):
            k = kb * kper + j
            s = jnp.concatenate([sbuf_ref[2 * k], sbuf_ref[2 * k + 1]], axis=1).astype(BF16)
            p = jnp.dot(s, cm_ref[k], preferred_element_type=F32)
            acc = p if acc is None else acc + p
        ys.append(acc)
    y = jnp.concatenate(ys, axis=1) + dk_ref[...] * u2
    z = jnp.dot(_gelu(y).astype(BF16), wo_ref[...], preferred_element_type=F32)
    ya = z[:, :d] * _sigmoid(z[:, d:])
    o_ref[...] = (ya * ga_ref[...].astype(F32)).astype(BF16)


def _s5(u, bm, lr, li, cm, dk, w_out_b, ga, nb):
    t, s5w = u.shape
    d = ga.shape[-1]
    tq = TIME_CHUNK
    rows = nb * tq
    nstate = bm.shape[0] * MXU_DIM
    row = lambda c: pl.BlockSpec((rows, c), lambda i: (i, 0))
    return pl.pallas_call(
        functools.partial(_s5_kernel, tq=tq, nb=nb),
        grid=(t // rows,),
        in_specs=[row(s5w), _full(bm), _full(lr), _full(li), _full(cm), _full(dk), _full(w_out_b), row(d)],
        out_specs=row(d),
        out_shape=jax.ShapeDtypeStruct((t, d), BF16),
        scratch_shapes=[pltpu.VMEM((nstate // LANES, nb, LANES), F32),
                        pltpu.VMEM((nstate // LANES, rows, LANES), F32)],
        compiler_params=_params(),
        name="s5",
    )(u, bm, lr, li, cm, dk, w_out_b, ga)


def _lru_kernel(xl_ref, gg_ref, gb_ref, cw_ref, cb_ref, wr_ref, wi_ref, br_ref, bi_ref, sp_ref, wo_ref,
                o_ref, h_ref, xprev_ref, a_ref, b_ref, *, tq, nb):
    rows = nb * tq
    w = xl_ref.shape[-1]
    nct = w // LANES
    halo = (CONV_WIDTH - 1) * nb

    @pl.when(pl.program_id(0) == 0)
    def _():
        h_ref[...] = jnp.zeros_like(h_ref)
        xprev_ref[...] = jnp.zeros_like(xprev_ref)

    xin = xl_ref[...]
    xext = jnp.concatenate([xprev_ref[...], xin], axis=0)
    xprev_ref[...] = xin[rows - halo:, :]
    xc_all = cb_ref[...]
    for k in range(CONV_WIDTH):
        xc_all = xc_all + xext[k * nb:k * nb + rows, :] * cw_ref[k:k + 1, :]

    ntile = wr_ref.shape[0]
    per = MXU_DIM // LANES
    for j in range(ntile):
        cs = slice(MXU_DIM * j, MXU_DIM * (j + 1))
        xc = xc_all[:, cs]
        xcb = xc.astype(BF16)
        r = _sigmoid(jnp.dot(xcb, wr_ref[j], preferred_element_type=F32) + br_ref[:, cs])
        ig = _sigmoid(jnp.dot(xcb, wi_ref[j], preferred_element_type=F32) + bi_ref[:, cs])
        log_a = (-LRU_C) * r * sp_ref[:, cs]
        th = jnp.tanh(log_a)
        mult = jnp.sqrt((-2.0 * th) / (1.0 - th))
        a = jnp.exp(log_a)
        b = mult * (ig * xc)
        for i in range(per):
            a_ref[per * j + i] = a[:, LANES * i:LANES * (i + 1)]
            b_ref[per * j + i] = b[:, LANES * i:LANES * (i + 1)]

    def scan_body(it, hs):
        hs = list(hs)
        for s in range(SCAN_UNROLL):
            tr = _tile(it * SCAN_UNROLL + s)
            for c in range(nct):
                hs[c] = a_ref[c, tr, :] * hs[c] + b_ref[c, tr, :]
                b_ref[c, tr, :] = hs[c]
        return tuple(hs)

    hfin = lax.fori_loop(0, tq // SCAN_UNROLL, scan_body, tuple(h_ref[c] for c in range(nct)))
    for c in range(nct):
        h_ref[c] = hfin[c]

    y = jnp.concatenate([b_ref[c] for c in range(nct)], axis=1) * gg_ref[...].astype(F32)
    yb = jnp.dot(y.astype(BF16), wo_ref[...], preferred_element_type=F32)
    o_ref[...] = (yb * gb_ref[...].astype(F32)).astype(BF16)


def _lru(xl, gg, gb, cw, cb, wr, wi, br, bi, sp, w_out_b, nb):
    t, w = xl.shape
    d = gb.shape[-1]
    tq = TIME_CHUNK
    rows = nb * tq
    row = lambda c: pl.BlockSpec((rows, c), lambda i: (i, 0))
    return pl.pallas_call(
        functools.partial(_lru_kernel, tq=tq, nb=nb),
        grid=(t // rows,),
        in_specs=[row(w), row(w), row(d), _full(cw), _full(cb), _full(wr), _full(wi), _full(br), _full(bi),
                  _full(sp), _full(w_out_b)],
        out_specs=row(d),
        out_shape=jax.ShapeDtypeStruct((t, d), BF16),
        scratch_shapes=[pltpu.VMEM((w // LANES, nb, LANES), F32),
                        pltpu.VMEM(((CONV_WIDTH - 1) * nb, w), F32),
                        pltpu.VMEM((w // LANES, rows, LANES), F32),
                        pltpu.VMEM((w // LANES, rows, LANES), F32)],
        compiler_params=_params(),
        name="lru",
    )(xl, gg, gb, cw, cb, wr, wi, br, bi, sp, w_out_b)


def _mixroute_kernel(x_ref, ya_ref, yb_ref, pt_ref, wo_ref, g_ref, wh_ref, wl_ref, rb_ref,
                     x1_ref, h2_ref, eid_ref, gate_ref, cnt_ref):
    nb, tq, d = x_ref.shape
    tm = nb * tq
    m = (ya_ref[...].astype(F32) + yb_ref[...].astype(F32)).astype(BF16)
    m = jnp.dot(pt_ref[...], m, preferred_element_type=F32).astype(BF16)
    x1 = x_ref[...].reshape(tm, d) + jnp.dot(m, wo_ref[...], preferred_element_type=F32)
    x1_ref[...] = x1
    h = _rms(x1, g_ref[...])

    hh = h.astype(BF16)
    h2_ref[...] = hh
    hl = (h - hh.astype(F32)).astype(BF16)
    dn = (((1,), (1,)), ((), ()))
    lt = (lax.dot_general(wh_ref[...], hh, dn, preferred_element_type=F32)
          + lax.dot_general(wh_ref[...], hl, dn, preferred_element_type=F32)
          + lax.dot_general(wl_ref[...], hh, dn, preferred_element_type=F32))
    lt = lt + rb_ref[:, 0:1]

    gl = lt[0:N_GROUPS, :]
    gmax = jnp.max(gl, axis=0, keepdims=True)
    gsum = jnp.sum(jnp.exp(gl - gmax), axis=0, keepdims=True)
    p_g = 1.0 / gsum
    iota_g = lax.broadcasted_iota(jnp.int32, gl.shape, 0)
    g_idx = jnp.min(jnp.where(gl == gmax, iota_g, N_GROUPS), axis=0, keepdims=True)

    el = lt[SUBLANES:SUBLANES + EXPERTS_PER_GROUP, :]
    for g in range(1, N_GROUPS):
        lo = SUBLANES + EXPERTS_PER_GROUP * g
        el = jnp.where(g_idx == g, lt[lo:lo + EXPERTS_PER_GROUP, :], el)
    emax = jnp.max(el, axis=0, keepdims=True)
    esum = jnp.sum(jnp.exp(el - emax), axis=0, keepdims=True)
    iota_e = lax.broadcasted_iota(jnp.int32, el.shape, 0)
    i1 = jnp.min(jnp.where(el == emax, iota_e, EXPERTS_PER_GROUP), axis=0, keepdims=True)
    el2 = jnp.where(iota_e == i1, -jnp.inf, el)
    emax2 = jnp.max(el2, axis=0, keepdims=True)
    i2 = jnp.min(jnp.where(el2 == emax2, iota_e, EXPERTS_PER_GROUP), axis=0, keepdims=True)
    p1 = 1.0 / esum
    p2 = jnp.exp(emax2 - emax) / esum
    psum = p1 + p2
    g1 = p_g * (p1 / psum)
    g2 = p_g * (p2 / psum)
    e1 = g_idx * EXPERTS_PER_GROUP + i1
    e2 = g_idx * EXPERTS_PER_GROUP + i2
    eid_ref[...] = jnp.concatenate([e1, e2, jnp.zeros((SUBLANES - 2, tm), jnp.int32)], axis=0)
    gate_ref[...] = jnp.concatenate([g1, g2, jnp.zeros((SUBLANES - 2, tm), F32)], axis=0)

    iota = lax.broadcasted_iota(jnp.int32, (LANES, tm), 0)
    hits = jnp.where((iota == e1) | (iota == e2), 1.0, 0.0).astype(BF16)
    cnt_ref[...] = lax.dot_general(jnp.ones((SUBLANES, tm), BF16), hits, dn, preferred_element_type=F32)


def _mixroute(x, ya, yb, perm_t, w_o_b, g, wr_hi, wr_lo, rbias):
    nb, seq, d = x.shape
    tq = TIME_CHUNK
    tm = nb * tq
    t = nb * seq
    row = lambda c: pl.BlockSpec((tm, c), lambda i: (i, 0))
    col = pl.BlockSpec((SUBLANES, tm), lambda i: (0, i))
    return pl.pallas_call(
        _mixroute_kernel,
        grid=(seq // tq,),
        in_specs=[pl.BlockSpec((nb, tq, d), lambda i: (0, i, 0)), row(d), row(d), _full(perm_t), _full(w_o_b),
                  _full(g), _full(wr_hi), _full(wr_lo), _full(rbias)],
        out_specs=[row(d), row(d), col, col, pl.BlockSpec((SUBLANES, LANES), lambda i: (i, 0))],
        out_shape=[jax.ShapeDtypeStruct((t, d), F32), jax.ShapeDtypeStruct((t, d), BF16),
                   jax.ShapeDtypeStruct((SUBLANES, t), jnp.int32), jax.ShapeDtypeStruct((SUBLANES, t), F32),
                   jax.ShapeDtypeStruct((seq // tq * SUBLANES, LANES), F32)],
        compiler_params=_params(),
        name="mixroute",
    )(x, ya, yb, perm_t, w_o_b, g, wr_hi, wr_lo, rbias)


def _tokens(ref, tok, n):
    return ref.at[pl.ds(pl.multiple_of(tok * SUBLANES, SUBLANES), n * SUBLANES)]


def _strip_copies(n, src, src_tok, dst, dst_tok, sem, max_tokens, wait=False):
    bit = 1 << (max_tokens.bit_length() - 1)
    while bit >= 1:
        done = n & (-2 * bit)

        @pl.when((n & bit) != 0)
        def _(bit=bit, done=done):
            cp = pltpu.make_async_copy(_tokens(src, src_tok + done, bit), _tokens(dst, dst_tok + done, bit), sem)
            if wait:
                cp.wait()
            else:
                cp.start()

        bit >>= 1


def _tile_positions(eid_ref, offcol_ref, tri_ref):
    tm = eid_ref.shape[1]
    e0 = eid_ref[0:1, :]
    e1 = eid_ref[1:2, :]
    iota = lax.broadcasted_iota(jnp.int32, (N_EXPERTS, tm), 0)
    oh0 = iota == e0
    oh1 = iota == e1
    c = jnp.where(oh0 | oh1, 1.0, 0.0)
    before = jnp.dot(c.astype(BF16), tri_ref[...], preferred_element_type=F32) + offcol_ref[:, 0:1]
    p0 = jnp.sum(jnp.where(oh0, before, 0.0), axis=0, keepdims=True)
    p1 = jnp.sum(jnp.where(oh1, before, 0.0), axis=0, keepdims=True)
    return p0, p1


def _dispatch_kernel(cnt_ref, off_ref, gst_ref, tot_ref, pst_ref, pen_ref,
                     h2_ref, eid_ref, offcol_ref, tri_ref, xs_ref, pos_ref,
                     stage0_ref, stage1_ref, zero_ref, sem, zsem):
    i = pl.program_id(0)
    last = pl.num_programs(0) - 1
    tm = eid_ref.shape[1]
    na = 2 * tm
    d = h2_ref.shape[1]

    p0, p1 = _tile_positions(eid_ref, offcol_ref, tri_ref)
    pos_ref[...] = jnp.concatenate([p0, p1, jnp.zeros((SUBLANES - 2, tm), F32)], axis=0)
    p0i = p0.astype(jnp.int32)
    p1i = p1.astype(jnp.int32)
    rid = lax.broadcasted_iota(jnp.int32, (na, tm), 0)
    onehot = jnp.where((rid == p0i) | (rid == p1i), 1.0, 0.0).astype(BF16)
    srt = jnp.dot(onehot, h2_ref[...], preferred_element_type=F32)

    def step(stage, s):
        @pl.when(i >= 2)
        def _():
            pltpu.make_async_copy(stage, _tokens(xs_ref, 0, na), sem.at[s]).wait()

        for j in range(d // LANES):
            stage[pl.ds(j, na, stride=d // LANES), :] = srt[:, LANES * j:LANES * (j + 1)]

        def per_expert(e, c):
            k = i * N_EXPERTS + e
            _strip_copies(cnt_ref[k], stage, off_ref[k], xs_ref, gst_ref[k], sem.at[s], na)
            return c

        lax.fori_loop(0, N_EXPERTS, per_expert, 0)

    @pl.when(i % 2 == 0)
    def _():
        step(stage0_ref, 0)

    @pl.when(i % 2 == 1)
    def _():
        step(stage1_ref, 1)

    @pl.when(i == last)
    def _():
        zero_ref[...] = jnp.zeros_like(zero_ref)
        for wait in (False, True):
            for e in range(N_EXPERTS):
                _strip_copies(pen_ref[e] - pst_ref[e] - tot_ref[e], zero_ref, 0, xs_ref,
                              pst_ref[e] + tot_ref[e], zsem, SLOT_BLOCK - 1, wait=wait)

        nblk = xs_ref.shape[0] // zero_ref.shape[0]

        def bcopy(b):
            return pltpu.make_async_copy(zero_ref, _tokens(xs_ref, b * SLOT_BLOCK, SLOT_BLOCK), zsem)

        first_unused = pen_ref[N_EXPERTS - 1] // SLOT_BLOCK
        lax.fori_loop(first_unused, nblk, lambda b, c: (bcopy(b).start(), c)[1], 0)
        lax.fori_loop(first_unused, nblk, lambda b, c: (bcopy(b).wait(), c)[1], 0)

        @pl.when(i >= 1)
        def _():
            @pl.when(i % 2 == 0)
            def _():
                pltpu.make_async_copy(stage1_ref, _tokens(xs_ref, 0, na), sem.at[1]).wait()

            @pl.when(i % 2 == 1)
            def _():
                pltpu.make_async_copy(stage0_ref, _tokens(xs_ref, 0, na), sem.at[0]).wait()

        @pl.when(i % 2 == 0)
        def _():
            pltpu.make_async_copy(stage0_ref, _tokens(xs_ref, 0, na), sem.at[0]).wait()

        @pl.when(i % 2 == 1)
        def _():
            pltpu.make_async_copy(stage1_ref, _tokens(xs_ref, 0, na), sem.at[1]).wait()


def _dispatch(cnt_t, off_t, gst_t, tot, pst, pen, h2, eid, offcol, tri, n_slots):
    sub = SUBLANES
    t, d = h2.shape
    tm = tri.shape[0]
    grid_spec = pltpu.PrefetchScalarGridSpec(
        num_scalar_prefetch=6,
        grid=(t // tm,),
        in_specs=[pl.BlockSpec((tm, d), lambda i, *_: (i, 0)),
                  pl.BlockSpec((sub, tm), lambda i, *_: (0, i)),
                  pl.BlockSpec((None, N_EXPERTS, LANES), lambda i, *_: (i, 0, 0)),
                  _full(tri)],
        out_specs=[pl.BlockSpec(memory_space=pl.ANY), pl.BlockSpec((sub, tm), lambda i, *_: (0, i))],
        scratch_shapes=[pltpu.VMEM((2 * tm * sub, LANES), F32), pltpu.VMEM((2 * tm * sub, LANES), F32),
                        pltpu.VMEM((SLOT_BLOCK * sub, LANES), F32), pltpu.SemaphoreType.DMA((2,)),
                        pltpu.SemaphoreType.DMA(())],
    )
    return pl.pallas_call(
        _dispatch_kernel,
        grid_spec=grid_spec,
        out_shape=[jax.ShapeDtypeStruct((n_slots * sub, LANES), F32), jax.ShapeDtypeStruct((sub, t), F32)],
        compiler_params=_params(),
        name="dispatch",
    )(cnt_t, off_t, gst_t, tot, pst, pen, h2, eid, offcol, tri)


def _ffn_kernel(be_ref, nu_ref, xs_ref, wg_ref, wu_ref, wd_ref, ys_ref, wgb_ref, wub_ref, wdb_ref, *, blk):
    i = pl.program_id(0)
    sub = SUBLANES

    @pl.when(i < nu_ref[0])
    def _():
        prev = be_ref[jnp.maximum(i - 1, 0)]

        @pl.when((i == 0) | (be_ref[i] != prev))
        def _():
            wgb_ref[...] = wg_ref[...].astype(BF16)
            wub_ref[...] = wu_ref[...].astype(BF16)
            wdb_ref[...] = wd_ref[...].astype(BF16)

        x = jnp.concatenate([xs_ref[pl.ds(j, blk, stride=sub), :] for j in range(sub)], axis=1).astype(BF16)
        g = jnp.dot(x, wgb_ref[...], preferred_element_type=F32)
        u = jnp.dot(x, wub_ref[...], preferred_element_type=F32)
        a = (g * _sigmoid(g)) * u
        y = jnp.dot(a.astype(BF16), wdb_ref[...], preferred_element_type=F32)
        for j in range(sub):
            ys_ref[pl.ds(j, blk, stride=sub), :] = y[:, LANES * j:LANES * (j + 1)]

    @pl.when(i >= nu_ref[0])
    def _():
        ys_ref[...] = jnp.zeros_like(ys_ref)


def _ffn(block_e, n_used, xs, wg, wu, wd):
    sub = SUBLANES
    blk = SLOT_BLOCK
    n_blocks = xs.shape[0] // (sub * blk)
    ne, d, de = wg.shape

    def slot_map(i, be, nu):
        return (i, 0)

    def w_map(i, be, nu):
        return (be[jnp.minimum(i, nu[0] - 1)], 0, 0)

    grid_spec = pltpu.PrefetchScalarGridSpec(
        num_scalar_prefetch=2,
        grid=(n_blocks,),
        in_specs=[pl.BlockSpec((blk * sub, LANES), slot_map),
                  pl.BlockSpec((None, d, de), w_map), pl.BlockSpec((None, d, de), w_map),
                  pl.BlockSpec((None, de, d), w_map)],
        out_specs=pl.BlockSpec((blk * sub, LANES), slot_map),
        scratch_shapes=[pltpu.VMEM((d, de), BF16), pltpu.VMEM((d, de), BF16), pltpu.VMEM((de, d), BF16)],
    )
    return pl.pallas_call(
        functools.partial(_ffn_kernel, blk=blk),
        grid_spec=grid_spec,
        out_shape=jax.ShapeDtypeStruct(xs.shape, F32),
        compiler_params=_params(),
        name="ffn",
    )(block_e, n_used, xs, wg, wu, wd)


def _combine_kernel(cnt_ref, off_ref, gst_ref, x1_ref, pos_ref, gate_ref, ys_ref, g_ref, o_ref,
                    buf0_ref, buf1_ref, sem):
    i = pl.program_id(0)
    n = pl.num_programs(0)
    tm, d = x1_ref.shape
    na = 2 * tm
    sub = SUBLANES

    def fetch(tile, buf, s):
        def per_expert(e, c):
            k = tile * N_EXPERTS + e
            _strip_copies(cnt_ref[k], ys_ref, gst_ref[k], buf, off_ref[k], sem.at[s], na)
            return c

        lax.fori_loop(0, N_EXPERTS, per_expert, 0)

    @pl.when(i == 0)
    def _():
        fetch(0, buf0_ref, 0)

    @pl.when((i + 1 < n) & (i % 2 == 0))
    def _():
        fetch(i + 1, buf1_ref, 1)

    @pl.when((i + 1 < n) & (i % 2 == 1))
    def _():
        fetch(i + 1, buf0_ref, 0)

    rows4 = jnp.concatenate([pos_ref[0:2, :], gate_ref[0:2, :], jnp.zeros((sub - 4, tm), F32)], axis=0)
    cols = jnp.concatenate([rows4] * (LANES // sub), axis=0).T
    lane = lax.broadcasted_iota(jnp.int32, (tm, na), 1)
    pick0 = jnp.where(lane == cols[:, 0:1].astype(jnp.int32), 1.0, 0.0).astype(BF16)
    pick1 = jnp.where(lane == cols[:, 1:2].astype(jnp.int32), 1.0, 0.0).astype(BF16)

    def finish(buf, s):
        pltpu.make_async_copy(_tokens(ys_ref, 0, na), buf, sem.at[s]).wait()
        y = jnp.concatenate([buf[pl.ds(j, na, stride=sub), :] for j in range(d // LANES)], axis=1).astype(BF16)
        y0 = jnp.dot(pick0, y, preferred_element_type=F32)
        y1 = jnp.dot(pick1, y, preferred_element_type=F32)
        x = x1_ref[...] + (cols[:, 2:3] * y0 + cols[:, 3:4] * y1)
        o_ref[...] = _rms(x, g_ref[...]).reshape(o_ref.shape)

    @pl.when(i % 2 == 0)
    def _():
        finish(buf0_ref, 0)

    @pl.when(i % 2 == 1)
    def _():
        finish(buf1_ref, 1)


def _combine(cnt_t, off_t, gst_t, x1, pos, gate, ys, g, nb, seq):
    t, d = x1.shape
    tq = TIME_CHUNK
    tm = nb * tq
    sub = SUBLANES
    grid_spec = pltpu.PrefetchScalarGridSpec(
        num_scalar_prefetch=3,
        grid=(t // tm,),
        in_specs=[pl.BlockSpec((tm, d), lambda i, *_: (i, 0)),
                  pl.BlockSpec((sub, tm), lambda i, *_: (0, i)),
                  pl.BlockSpec((sub, tm), lambda i, *_: (0, i)),
                  pl.BlockSpec(memory_space=pl.ANY),
                  _full(g)],
        out_specs=pl.BlockSpec((nb, tq, d), lambda i, *_: (0, i, 0)),
        scratch_shapes=[pltpu.VMEM((2 * tm * sub, LANES), F32), pltpu.VMEM((2 * tm * sub, LANES), F32),
                        pltpu.SemaphoreType.DMA((2,))],
    )
    return pl.pallas_call(
        _combine_kernel,
        grid_spec=grid_spec,
        out_shape=jax.ShapeDtypeStruct((nb, seq, d), F32),
        compiler_params=_params(),
        name="combine",
    )(cnt_t, off_t, gst_t, x1, pos, gate, ys, g)


def _s5_tables(lam_re, lam_im, log_dt, b_re, b_im, c_re, c_im):
    ng, p = lam_re.shape
    npair = ng // 2
    kper = LANES // (2 * S5_GROUP)
    lam = lax.complex(lam_re, lam_im)
    dt = jnp.exp(log_dt)[:, None]
    lam_bar = jnp.exp(lam * dt)
    b_bar = ((lam_bar - 1.0) / lam)[..., None] * lax.complex(b_re, b_im)
    lr = jnp.real(lam_bar).reshape(npair, 2 * p)
    li = jnp.imag(lam_bar).reshape(npair, 2 * p)
    eye2 = jnp.eye(2, dtype=F32)
    sel = jax.nn.one_hot(jnp.arange(npair) % kper, kper, dtype=F32)

    bv = jnp.stack([jnp.real(b_bar), jnp.imag(b_bar)])
    bv = bv.reshape(2, npair, 2, p, S5_GROUP).transpose(1, 2, 4, 0, 3)
    bblk = bv[:, :, :, :, None, :] * eye2[None, :, None, None, :, None]
    bblk = bblk.reshape(npair, 2 * S5_GROUP, 4 * p)
    bm = (sel[:, :, None, None] * bblk[:, None]).reshape(npair, LANES, 4 * p)

    cv = jnp.stack([c_re, -c_im])
    cv = cv.reshape(2, npair, 2, S5_GROUP, p).transpose(1, 0, 2, 4, 3)
    cblk = cv[:, :, :, :, None, :] * eye2[None, None, :, None, :, None]
    cblk = cblk.reshape(npair, 4 * p, 2 * S5_GROUP)
    cm = (cblk[:, :, None, :] * sel[:, None, :, None]).reshape(npair, 4 * p, LANES)
    return bm.astype(BF16), lr, li, cm.astype(BF16)


def _blockdiag_tiles(w):
    nh, hi, ho = w.shape
    per = MXU_DIM // hi
    eye = jnp.eye(per, dtype=w.dtype)
    t = w.reshape(nh // per, per, hi, ho)[:, :, :, None, :] * eye[None, :, None, :, None]
    return t.reshape(nh // per, per * hi, per * ho)


def kernel(x, norm_mix, w_in, s5_lam_re, s5_lam_im, s5_log_dt, s5_b_re, s5_b_im, s5_c_re, s5_c_im, s5_d,
           w_s5_out, conv_w, conv_b, lru_w_r, lru_b_r, lru_w_i, lru_b_i, lru_lambda, w_lru_out, w_o, norm_ffn,
           router_group_w, router_group_b, router_expert_w, router_expert_b, expert_w_gate, expert_w_up,
           expert_w_down, norm_final):
    nb, seq, d = x.shape
    t = nb * seq
    assert w_in.shape[0] == 1, "one layer: the final RMSNorm is fused into the layer's combine kernel"
    l = 0
    s5w = s5_d.shape[-1]
    lruw = conv_b.shape[-1]
    rows = nb * TIME_CHUNK

    r = jnp.arange(rows)
    perm = jax.nn.one_hot((r % nb) * TIME_CHUNK + r // nb, rows, dtype=BF16)

    u, xl, gg, ga, gb = _inproj(x, norm_mix[l][None, :], perm, w_in[l].astype(BF16), s5w, lruw)

    bm, lr, li, cm = _s5_tables(s5_lam_re[l], s5_lam_im[l], s5_log_dt[l], s5_b_re[l], s5_b_im[l],
                                s5_c_re[l], s5_c_im[l])
    ya = _s5(u, bm, lr, li, cm, s5_d[l][None, :], w_s5_out[l].astype(BF16), ga, nb)

    sp = jax.nn.softplus(-lru_lambda[l])[None, :]
    yb = _lru(xl, gg, gb, conv_w[l], conv_b[l][None, :], _blockdiag_tiles(lru_w_r[l]).astype(BF16),
              _blockdiag_tiles(lru_w_i[l]).astype(BF16), lru_b_r[l][None, :], lru_b_i[l][None, :], sp,
              w_lru_out[l].astype(BF16), nb)

    gap = jnp.zeros((SUBLANES - N_GROUPS, d), F32)
    wrt = jnp.concatenate([router_group_w[l].T, gap, router_expert_w[l].T], axis=0)
    wr_hi = wrt.astype(BF16)
    wr_lo = (wrt - wr_hi.astype(F32)).astype(BF16)
    rb = jnp.concatenate([router_group_b[l], jnp.zeros((SUBLANES - N_GROUPS,), F32), router_expert_b[l]])
    rbias = jnp.broadcast_to(rb[:, None], (ROUTER_ROWS, LANES))

    x1, h2, eid, gate, tcnt = _mixroute(x, ya, yb, perm.T, w_o[l].astype(BF16), norm_ffn[l][None, :],
                                        wr_hi, wr_lo, rbias)

    n_tiles = seq // TIME_CHUNK
    cnt_t = tcnt.reshape(n_tiles, SUBLANES, LANES)[:, 0, :N_EXPERTS].astype(jnp.int32)
    off_t = jnp.cumsum(cnt_t, axis=1) - cnt_t
    counts = jnp.sum(cnt_t, axis=0)
    padded = ((counts + SLOT_BLOCK - 1) // SLOT_BLOCK) * SLOT_BLOCK
    pad_ends = jnp.cumsum(padded)
    pad_starts = pad_ends - padded
    gst_t = pad_starts[None, :] + jnp.cumsum(cnt_t, axis=0) - cnt_t
    offcol = jnp.broadcast_to(off_t.astype(F32)[:, :, None], (n_tiles, N_EXPERTS, LANES))
    n_blocks = -(-(2 * t) // SLOT_BLOCK) + N_EXPERTS
    n_slots = n_blocks * SLOT_BLOCK
    block_start = jnp.arange(n_blocks, dtype=jnp.int32) * SLOT_BLOCK
    block_e = jnp.minimum(jnp.sum((pad_ends[None, :] <= block_start[:, None]).astype(jnp.int32), axis=1),
                          N_EXPERTS - 1)
    n_used = pad_ends[-1:] // SLOT_BLOCK
    cnt_f, off_f, gst_f = cnt_t.reshape(-1), off_t.reshape(-1), gst_t.reshape(-1)

    tri = jnp.triu(jnp.ones((rows, rows), BF16), k=1)
    xs, pos = _dispatch(cnt_f, off_f, gst_f, counts, pad_starts, pad_ends, h2, eid, offcol, tri, n_slots)
    ys = _ffn(block_e, n_used, xs, expert_w_gate[l], expert_w_up[l], expert_w_down[l])
    return _combine(cnt_f, off_f, gst_f, x1, pos, gate, ys, norm_final[None, :], nb, seq)
```

```python
import functools
import math

import jax
import jax.numpy as jnp
from jax import lax
from jax.experimental import pallas as pl
from jax.experimental.pallas import tpu as pltpu

F32 = jnp.float32
BF16 = jnp.bfloat16

RMS_EPS = 1e-6
S5_GROUP = 16
CONV_WIDTH = 4
LRU_C = 8.0
N_GROUPS = 4
EXPERTS_PER_GROUP = 8
N_EXPERTS = N_GROUPS * EXPERTS_PER_GROUP

LANES = 128
SUBLANES = 8
MXU_DIM = 256
TIME_CHUNK = 64
SCAN_UNROLL = 4
SLOT_BLOCK = 512
RARE_STRIP = 64
ROUTER_ROWS = 40
VMEM_LIMIT = 56 * 1024 * 1024


def _gelu(x):
    c = math.sqrt(2.0 / math.pi)
    return 0.5 * x * (1.0 + jnp.tanh(c * (x + 0.044715 * (x * x * x))))


def _sigmoid(x):
    return 0.5 * jnp.tanh(0.5 * x) + 0.5


def _rms(x, g):
    return x * lax.rsqrt(jnp.mean(x * x, axis=-1, keepdims=True) + RMS_EPS) * g


def _params():
    return pltpu.CompilerParams(dimension_semantics=("arbitrary",), vmem_limit_bytes=VMEM_LIMIT)


def _full(a):
    return pl.BlockSpec(a.shape, lambda i, *_: (0,) * a.ndim)


def _tile(t):
    return pl.ds(pl.multiple_of(t * SUBLANES, SUBLANES), SUBLANES)


def _inproj_kernel(x_ref, g_ref, p_ref, w_ref, u_ref, xl_ref, gg_ref, ga_ref, gb_ref, *, s5w, lruw):
    nb, tq, d = x_ref.shape
    h = _rms(x_ref[...].reshape(nb * tq, d), g_ref[...])
    hb = jnp.dot(p_ref[...], h.astype(BF16), preferred_element_type=F32).astype(BF16)

    def proj(lo, hi):
        return jnp.dot(hb, w_ref[:, lo:hi], preferred_element_type=F32)

    o1 = s5w
    o2 = o1 + lruw
    o3 = o2 + lruw
    o4 = o3 + d
    u_ref[...] = proj(0, o1)
    xl_ref[...] = proj(o1, o2)
    gg_ref[...] = _gelu(proj(o2, o3)).astype(BF16)
    ga_ref[...] = _sigmoid(proj(o3, o4)).astype(BF16)
    gb_ref[...] = _sigmoid(proj(o4, o4 + d)).astype(BF16)


def _inproj(x, g, perm, w_in_b, s5w, lruw):
    nb, seq, d = x.shape
    tq = TIME_CHUNK
    rows = nb * tq
    t = nb * seq
    row = lambda c: pl.BlockSpec((rows, c), lambda i: (i, 0))
    return pl.pallas_call(
        functools.partial(_inproj_kernel, s5w=s5w, lruw=lruw),
        grid=(seq // tq,),
        in_specs=[pl.BlockSpec((nb, tq, d), lambda i: (0, i, 0)), _full(g), _full(perm), _full(w_in_b)],
        out_specs=[row(s5w), row(lruw), row(lruw), row(d), row(d)],
        out_shape=[jax.ShapeDtypeStruct((t, s5w), F32), jax.ShapeDtypeStruct((t, lruw), F32),
                   jax.ShapeDtypeStruct((t, lruw), BF16), jax.ShapeDtypeStruct((t, d), BF16),
                   jax.ShapeDtypeStruct((t, d), BF16)],
        compiler_params=_params(),
        name="inproj",
    )(x, g, perm, w_in_b)


def _s5_kernel(u_ref, bm_ref, lr_ref, li_ref, cm_ref, dk_ref, wo_ref, ga_ref, o_ref,
               state_ref, sbuf_ref, *, tq, nb):
    d = o_ref.shape[-1]
    npair = bm_ref.shape[0]
    kper = LANES // (2 * S5_GROUP)

    @pl.when(pl.program_id(0) == 0)
    def _():
        state_ref[...] = jnp.zeros_like(state_ref)

    u2 = u_ref[...]
    ub = u2.astype(BF16)
    for k in range(npair):
        kb = k // kper
        bu = jnp.dot(ub[:, LANES * kb:LANES * (kb + 1)], bm_ref[k], preferred_element_type=F32)
        sbuf_ref[2 * k] = bu[:, :LANES]
        sbuf_ref[2 * k + 1] = bu[:, LANES:]

    pairs_per_loop = 4
    for kk in range(npair // pairs_per_loop):
        k0 = kk * pairs_per_loop
        lr = [jnp.broadcast_to(lr_ref[k0 + j:k0 + j + 1, :], (nb, LANES)) for j in range(pairs_per_loop)]
        li = [jnp.broadcast_to(li_ref[k0 + j:k0 + j + 1, :], (nb, LANES)) for j in range(pairs_per_loop)]
        init = tuple(state_ref[2 * k0 + j] for j in range(2 * pairs_per_loop))

        def body(it, carry, k0=k0, lr=lr, li=li):
            carry = list(carry)
            for s in range(SCAN_UNROLL):
                rows = _tile(it * SCAN_UNROLL + s)
                for j in range(pairs_per_loop):
                    sre, sim = carry[2 * j], carry[2 * j + 1]
                    nre = lr[j] * sre - li[j] * sim + sbuf_ref[2 * (k0 + j), rows, :]
                    nim = lr[j] * sim + li[j] * sre + sbuf_ref[2 * (k0 + j) + 1, rows, :]
                    sbuf_ref[2 * (k0 + j), rows, :] = nre
                    sbuf_ref[2 * (k0 + j) + 1, rows, :] = nim
                    carry[2 * j], carry[2 * j + 1] = nre, nim
            return tuple(carry)

        fin = lax.fori_loop(0, tq // SCAN_UNROLL, body, init)
        for j in range(2 * pairs_per_loop):
            state_ref[2 * k0 + j] = fin[j]

    ys = []
    for kb in range(npair // kper):
        acc = None
        for j in range(kper):
            k = kb * kper + j
            s = jnp.concatenate([sbuf_ref[2 * k], sbuf_ref[2 * k + 1]], axis=1).astype(BF16)
            p = jnp.dot(s, cm_ref[k], preferred_element_type=F32)
            acc = p if acc is None else acc + p
        ys.append(acc)
    y = jnp.concatenate(ys, axis=1) + dk_ref[...] * u2
    z = jnp.dot(_gelu(y).astype(BF16), wo_ref[...], preferred_element_type=F32)
    ya = z[:, :d] * _sigmoid(z[:, d:])
    o_ref[...] = (ya * ga_ref[...].astype(F32)).astype(BF16)


def _s5(u, bm, lr, li, cm, dk, w_out_b, ga, nb):
    t, s5w = u.shape
    d = ga.shape[-1]
    tq = TIME_CHUNK
    rows = nb * tq
    nstate = bm.shape[0] * MXU_DIM
    row = lambda c: pl.BlockSpec((rows, c), lambda i: (i, 0))
    return pl.pallas_call(
        functools.partial(_s5_kernel, tq=tq, nb=nb),
        grid=(t // rows,),
        in_specs=[row(s5w), _full(bm), _full(lr), _full(li), _full(cm), _full(dk), _full(w_out_b), row(d)],
        out_specs=row(d),
        out_shape=jax.ShapeDtypeStruct((t, d), BF16),
        scratch_shapes=[pltpu.VMEM((nstate // LANES, nb, LANES), F32),
                        pltpu.VMEM((nstate // LANES, rows, LANES), F32)],
        compiler_params=_params(),
        name="s5",
    )(u, bm, lr, li, cm, dk, w_out_b, ga)


def _lru_kernel(xl_ref, gg_ref, gb_ref, cw_ref, cb_ref, wr_ref, wi_ref, br_ref, bi_ref, sp_ref, wo_ref,
                o_ref, h_ref, xprev_ref, a_ref, b_ref, *, tq, nb):
    rows = nb * tq
    w = xl_ref.shape[-1]
    nct = w // LANES
    halo = (CONV_WIDTH - 1) * nb

    @pl.when(pl.program_id(0) == 0)
    def _():
        h_ref[...] = jnp.zeros_like(h_ref)
        xprev_ref[...] = jnp.zeros_like(xprev_ref)

    xin = xl_ref[...]
    xext = jnp.concatenate([xprev_ref[...], xin], axis=0)
    xprev_ref[...] = xin[rows - halo:, :]
    xc_all = cb_ref[...]
    for k in range(CONV_WIDTH):
        xc_all = xc_all + xext[k * nb:k * nb + rows, :] * cw_ref[k:k + 1, :]

    ntile = wr_ref.shape[0]
    per = MXU_DIM // LANES
    for j in range(ntile):
        cs = slice(MXU_DIM * j, MXU_DIM * (j + 1))
        xc = xc_all[:, cs]
        xcb = xc.astype(BF16)
        r = _sigmoid(jnp.dot(xcb, wr_ref[j], preferred_element_type=F32) + br_ref[:, cs])
        ig = _sigmoid(jnp.dot(xcb, wi_ref[j], preferred_element_type=F32) + bi_ref[:, cs])
        log_a = (-LRU_C) * r * sp_ref[:, cs]
        th = jnp.tanh(log_a)
        mult = jnp.sqrt((-2.0 * th) / (1.0 - th))
        a = jnp.exp(log_a)
        b = mult * (ig * xc)
        for i in range(per):
            a_ref[per * j + i] = a[:, LANES * i:LANES * (i + 1)]
            b_ref[per * j + i] = b[:, LANES * i:LANES * (i + 1)]

    def scan_body(it, hs):
        hs = list(hs)
        for s in range(SCAN_UNROLL):
            tr = _tile(it * SCAN_UNROLL + s)
            for c in range(nct):
                hs[c] = a_ref[c, tr, :] * hs[c] + b_ref[c, tr, :]
                b_ref[c, tr, :] = hs[c]
        return tuple(hs)

    hfin = lax.fori_loop(0, tq // SCAN_UNROLL, scan_body, tuple(h_ref[c] for c in range(nct)))
    for c in range(nct):
        h_ref[c] = hfin[c]

    y = jnp.concatenate([b_ref[c] for c in range(nct)], axis=1) * gg_ref[...].astype(F32)
    yb = jnp.dot(y.astype(BF16), wo_ref[...], preferred_element_type=F32)
    o_ref[...] = (yb * gb_ref[...].astype(F32)).astype(BF16)


def _lru(xl, gg, gb, cw, cb, wr, wi, br, bi, sp, w_out_b, nb):
    t, w = xl.shape
    d = gb.shape[-1]
    tq = TIME_CHUNK
    rows = nb * tq
    row = lambda c: pl.BlockSpec((rows, c), lambda i: (i, 0))
    return pl.pallas_call(
        functools.partial(_lru_kernel, tq=tq, nb=nb),
        grid=(t // rows,),
        in_specs=[row(w), row(w), row(d), _full(cw), _full(cb), _full(wr), _full(wi), _full(br), _full(bi),
                  _full(sp), _full(w_out_b)],
        out_specs=row(d),
        out_shape=jax.ShapeDtypeStruct((t, d), BF16),
        scratch_shapes=[pltpu.VMEM((w // LANES, nb, LANES), F32),
                        pltpu.VMEM(((CONV_WIDTH - 1) * nb, w), F32),
                        pltpu.VMEM((w // LANES, rows, LANES), F32),
                        pltpu.VMEM((w // LANES, rows, LANES), F32)],
        compiler_params=_params(),
        name="lru",
    )(xl, gg, gb, cw, cb, wr, wi, br, bi, sp, w_out_b)


def _mixroute_kernel(x_ref, ya_ref, yb_ref, pt_ref, wo_ref, g_ref, wh_ref, wl_ref, rb_ref,
                     x1_ref, h2_ref, eid_ref, gate_ref, cnt_ref):
    nb, tq, d = x_ref.shape
    tm = nb * tq
    m = (ya_ref[...].astype(F32) + yb_ref[...].astype(F32)).astype(BF16)
    m = jnp.dot(pt_ref[...], m, preferred_element_type=F32).astype(BF16)
    x1 = x_ref[...].reshape(tm, d) + jnp.dot(m, wo_ref[...], preferred_element_type=F32)
    x1_ref[...] = x1
    h = _rms(x1, g_ref[...])

    hh = h.astype(BF16)
    h2_ref[...] = hh
    hl = (h - hh.astype(F32)).astype(BF16)
    dn = (((1,), (1,)), ((), ()))
    lt = (lax.dot_general(wh_ref[...], hh, dn, preferred_element_type=F32)
          + lax.dot_general(wh_ref[...], hl, dn, preferred_element_type=F32)
          + lax.dot_general(wl_ref[...], hh, dn, preferred_element_type=F32))
    lt = lt + rb_ref[:, 0:1]

    gl = lt[0:N_GROUPS, :]
    gmax = jnp.max(gl, axis=0, keepdims=True)
    gsum = jnp.sum(jnp.exp(gl - gmax), axis=0, keepdims=True)
    p_g = 1.0 / gsum
    iota_g = lax.broadcasted_iota(jnp.int32, gl.shape, 0)
    g_idx = jnp.min(jnp.where(gl == gmax, iota_g, N_GROUPS), axis=0, keepdims=True)

    el = lt[SUBLANES:SUBLANES + EXPERTS_PER_GROUP, :]
    for g in range(1, N_GROUPS):
        lo = SUBLANES + EXPERTS_PER_GROUP * g
        el = jnp.where(g_idx == g, lt[lo:lo + EXPERTS_PER_GROUP, :], el)
    emax = jnp.max(el, axis=0, keepdims=True)
    esum = jnp.sum(jnp.exp(el - emax), axis=0, keepdims=True)
    iota_e = lax.broadcasted_iota(jnp.int32, el.shape, 0)
    i1 = jnp.min(jnp.where(el == emax, iota_e, EXPERTS_PER_GROUP), axis=0, keepdims=True)
    el2 = jnp.where(iota_e == i1, -jnp.inf, el)
    emax2 = jnp.max(el2, axis=0, keepdims=True)
    i2 = jnp.min(jnp.where(el2 == emax2, iota_e, EXPERTS_PER_GROUP), axis=0, keepdims=True)
    p1 = 1.0 / esum
    p2 = jnp.exp(emax2 - emax) / esum
    psum = p1 + p2
    g1 = p_g * (p1 / psum)
    g2 = p_g * (p2 / psum)
    e1 = g_idx * EXPERTS_PER_GROUP + i1
    e2 = g_idx * EXPERTS_PER_GROUP + i2
    eid_ref[...] = jnp.concatenate([e1, e2, jnp.zeros((SUBLANES - 2, tm), jnp.int32)], axis=0)
    gate_ref[...] = jnp.concatenate([g1, g2, jnp.zeros((SUBLANES - 2, tm), F32)], axis=0)

    iota = lax.broadcasted_iota(jnp.int32, (LANES, tm), 0)
    hits = jnp.where((iota == e1) | (iota == e2), 1.0, 0.0).astype(BF16)
    cnt_ref[...] = lax.dot_general(jnp.ones((SUBLANES, tm), BF16), hits, dn, preferred_element_type=F32)


def _mixroute(x, ya, yb, perm_t, w_o_b, g, wr_hi, wr_lo, rbias):
    nb, seq, d = x.shape
    tq = TIME_CHUNK
    tm = nb * tq
    t = nb * seq
    row = lambda c: pl.BlockSpec((tm, c), lambda i: (i, 0))
    col = pl.BlockSpec((SUBLANES, tm), lambda i: (0, i))
    return pl.pallas_call(
        _mixroute_kernel,
        grid=(seq // tq,),
        in_specs=[pl.BlockSpec((nb, tq, d), lambda i: (0, i, 0)), row(d), row(d), _full(perm_t), _full(w_o_b),
                  _full(g), _full(wr_hi), _full(wr_lo), _full(rbias)],
        out_specs=[row(d), row(d), col, col, pl.BlockSpec((SUBLANES, LANES), lambda i: (i, 0))],
        out_shape=[jax.ShapeDtypeStruct((t, d), F32), jax.ShapeDtypeStruct((t, d), BF16),
                   jax.ShapeDtypeStruct((SUBLANES, t), jnp.int32), jax.ShapeDtypeStruct((SUBLANES, t), F32),
                   jax.ShapeDtypeStruct((seq // tq * SUBLANES, LANES), F32)],
        compiler_params=_params(),
        name="mixroute",
    )(x, ya, yb, perm_t, w_o_b, g, wr_hi, wr_lo, rbias)


def _tokens(ref, tok, n):
    return ref.at[pl.ds(pl.multiple_of(tok * SUBLANES, SUBLANES), n * SUBLANES)]


def _strip_copies(n, src, src_tok, dst, dst_tok, sem, max_tokens, wait=False):
    def piece(bit):
        done = n & (-2 * bit)

        @pl.when((n & bit) != 0)
        def _():
            cp = pltpu.make_async_copy(_tokens(src, src_tok + done, bit), _tokens(dst, dst_tok + done, bit), sem)
            if wait:
                cp.wait()
            else:
                cp.start()

    bits = [1 << b for b in reversed(range(max_tokens.bit_length()))]
    rare = [b for b in bits if b >= RARE_STRIP]
    if rare:
        @pl.when(n >= RARE_STRIP)
        def _():
            for b in rare:
                piece(b)
    for b in bits:
        if b < RARE_STRIP:
            piece(b)


def _tile_positions(eid_ref, offcol_ref, tri_ref):
    tm = eid_ref.shape[1]
    e0 = eid_ref[0:1, :]
    e1 = eid_ref[1:2, :]
    iota = lax.broadcasted_iota(jnp.int32, (N_EXPERTS, tm), 0)
    oh0 = iota == e0
    oh1 = iota == e1
    c = jnp.where(oh0 | oh1, 1.0, 0.0)
    before = jnp.dot(c.astype(BF16), tri_ref[...], preferred_element_type=F32) + offcol_ref[:, 0:1]
    p0 = jnp.sum(jnp.where(oh0, before, 0.0), axis=0, keepdims=True)
    p1 = jnp.sum(jnp.where(oh1, before, 0.0), axis=0, keepdims=True)
    return p0, p1


def _dispatch_kernel(cnt_ref, off_ref, gst_ref, tot_ref, pst_ref, pen_ref,
                     h2_ref, eid_ref, offcol_ref, tri_ref, xs_ref, pos_ref,
                     stage0_ref, stage1_ref, zero_ref, sem, zsem):
    i = pl.program_id(0)
    last = pl.num_programs(0) - 1
    tm = eid_ref.shape[1]
    na = 2 * tm
    d = h2_ref.shape[1]

    p0, p1 = _tile_positions(eid_ref, offcol_ref, tri_ref)
    pos_ref[...] = jnp.concatenate([p0, p1, jnp.zeros((SUBLANES - 2, tm), F32)], axis=0)
    p0i = p0.astype(jnp.int32)
    p1i = p1.astype(jnp.int32)
    rid = lax.broadcasted_iota(jnp.int32, (na, tm), 0)
    onehot = jnp.where((rid == p0i) | (rid == p1i), 1.0, 0.0).astype(BF16)
    srt = jnp.dot(onehot, h2_ref[...], preferred_element_type=F32)

    def step(stage, s):
        @pl.when(i >= 2)
        def _():
            pltpu.make_async_copy(stage, _tokens(xs_ref, 0, na), sem.at[s]).wait()

        for j in range(d // LANES):
            stage[pl.ds(j, na, stride=d // LANES), :] = srt[:, LANES * j:LANES * (j + 1)]

        def per_expert(e, c):
            k = i * N_EXPERTS + e
            _strip_copies(cnt_ref[k], stage, off_ref[k], xs_ref, gst_ref[k], sem.at[s], na)
            return c

        lax.fori_loop(0, N_EXPERTS, per_expert, 0)

    @pl.when(i % 2 == 0)
    def _():
        step(stage0_ref, 0)

    @pl.when(i % 2 == 1)
    def _():
        step(stage1_ref, 1)

    @pl.when(i == last)
    def _():
        zero_ref[...] = jnp.zeros_like(zero_ref)
        for wait in (False, True):
            for e in range(N_EXPERTS):
                _strip_copies(pen_ref[e] - pst_ref[e] - tot_ref[e], zero_ref, 0, xs_ref,
                              pst_ref[e] + tot_ref[e], zsem, SLOT_BLOCK - 1, wait=wait)

        nblk = xs_ref.shape[0] // zero_ref.shape[0]

        def bcopy(b):
            return pltpu.make_async_copy(zero_ref, _tokens(xs_ref, b * SLOT_BLOCK, SLOT_BLOCK), zsem)

        first_unused = pen_ref[N_EXPERTS - 1] // SLOT_BLOCK
        lax.fori_loop(first_unused, nblk, lambda b, c: (bcopy(b).start(), c)[1], 0)
        lax.fori_loop(first_unused, nblk, lambda b, c: (bcopy(b).wait(), c)[1], 0)

        @pl.when(i >= 1)
        def _():
            @pl.when(i % 2 == 0)
            def _():
                pltpu.make_async_copy(stage1_ref, _tokens(xs_ref, 0, na), sem.at[1]).wait()

            @pl.when(i % 2 == 1)
            def _():
                pltpu.make_async_copy(stage0_ref, _tokens(xs_ref, 0, na), sem.at[0]).wait()

        @pl.when(i % 2 == 0)
        def _():
            pltpu.make_async_copy(stage0_ref, _tokens(xs_ref, 0, na), sem.at[0]).wait()

        @pl.when(i % 2 == 1)
        def _():
            pltpu.make_async_copy(stage1_ref, _tokens(xs_ref, 0, na), sem.at[1]).wait()


def _dispatch(cnt_t, off_t, gst_t, tot, pst, pen, h2, eid, offcol, tri, n_slots):
    sub = SUBLANES
    t, d = h2.shape
    tm = tri.shape[0]
    grid_spec = pltpu.PrefetchScalarGridSpec(
        num_scalar_prefetch=6,
        grid=(t // tm,),
        in_specs=[pl.BlockSpec((tm, d), lambda i, *_: (i, 0)),
                  pl.BlockSpec((sub, tm), lambda i, *_: (0, i)),
                  pl.BlockSpec((None, N_EXPERTS, LANES), lambda i, *_: (i, 0, 0)),
                  _full(tri)],
        out_specs=[pl.BlockSpec(memory_space=pl.ANY), pl.BlockSpec((sub, tm), lambda i, *_: (0, i))],
        scratch_shapes=[pltpu.VMEM((2 * tm * sub, LANES), F32), pltpu.VMEM((2 * tm * sub, LANES), F32),
                        pltpu.VMEM((SLOT_BLOCK * sub, LANES), F32), pltpu.SemaphoreType.DMA((2,)),
                        pltpu.SemaphoreType.DMA(())],
    )
    return pl.pallas_call(
        _dispatch_kernel,
        grid_spec=grid_spec,
        out_shape=[jax.ShapeDtypeStruct((n_slots * sub, LANES), F32), jax.ShapeDtypeStruct((sub, t), F32)],
        compiler_params=_params(),
        name="dispatch",
    )(cnt_t, off_t, gst_t, tot, pst, pen, h2, eid, offcol, tri)


def _ffn_kernel(be_ref, nu_ref, xs_ref, wg_ref, wu_ref, wd_ref, ys_ref, wgb_ref, wub_ref, wdb_ref, *, blk):
    i = pl.program_id(0)
    sub = SUBLANES

    @pl.when(i < nu_ref[0])
    def _():
        prev = be_ref[jnp.maximum(i - 1, 0)]

        @pl.when((i == 0) | (be_ref[i] != prev))
        def _():
            wgb_ref[...] = wg_ref[...].astype(BF16)
            wub_ref[...] = wu_ref[...].astype(BF16)
            wdb_ref[...] = wd_ref[...].astype(BF16)

        x = jnp.concatenate([xs_ref[pl.ds(j, blk, stride=sub), :] for j in range(sub)], axis=1).astype(BF16)
        g = jnp.dot(x, wgb_ref[...], preferred_element_type=F32)
        u = jnp.dot(x, wub_ref[...], preferred_element_type=F32)
        a = (g * _sigmoid(g)) * u
        y = jnp.dot(a.astype(BF16), wdb_ref[...], preferred_element_type=F32)
        for j in range(sub):
            ys_ref[pl.ds(j, blk, stride=sub), :] = y[:, LANES * j:LANES * (j + 1)]

    @pl.when(i >= nu_ref[0])
    def _():
        ys_ref[...] = jnp.zeros_like(ys_ref)


def _ffn(block_e, n_used, xs, wg, wu, wd):
    sub = SUBLANES
    blk = SLOT_BLOCK
    n_blocks = xs.shape[0] // (sub * blk)
    ne, d, de = wg.shape

    def slot_map(i, be, nu):
        return (i, 0)

    def w_map(i, be, nu):
        return (be[jnp.minimum(i, nu[0] - 1)], 0, 0)

    grid_spec = pltpu.PrefetchScalarGridSpec(
        num_scalar_prefetch=2,
        grid=(n_blocks,),
        in_specs=[pl.BlockSpec((blk * sub, LANES), slot_map),
                  pl.BlockSpec((None, d, de), w_map), pl.BlockSpec((None, d, de), w_map),
                  pl.BlockSpec((None, de, d), w_map)],
        out_specs=pl.BlockSpec((blk * sub, LANES), slot_map),
        scratch_shapes=[pltpu.VMEM((d, de), BF16), pltpu.VMEM((d, de), BF16), pltpu.VMEM((de, d), BF16)],
    )
    return pl.pallas_call(
        functools.partial(_ffn_kernel, blk=blk),
        grid_spec=grid_spec,
        out_shape=jax.ShapeDtypeStruct(xs.shape, F32),
        compiler_params=_params(),
        name="ffn",
    )(block_e, n_used, xs, wg, wu, wd)


def _combine_kernel(cnt_ref, off_ref, gst_ref, x1_ref, pos_ref, gate_ref, ys_ref, g_ref, o_ref,
                    buf0_ref, buf1_ref, sem):
    i = pl.program_id(0)
    n = pl.num_programs(0)
    tm, d = x1_ref.shape
    na = 2 * tm
    sub = SUBLANES

    def fetch(tile, buf, s):
        def per_expert(e, c):
            k = tile * N_EXPERTS + e
            _strip_copies(cnt_ref[k], ys_ref, gst_ref[k], buf, off_ref[k], sem.at[s], na)
            return c

        lax.fori_loop(0, N_EXPERTS, per_expert, 0)

    @pl.when(i == 0)
    def _():
        fetch(0, buf0_ref, 0)

    @pl.when((i + 1 < n) & (i % 2 == 0))
    def _():
        fetch(i + 1, buf1_ref, 1)

    @pl.when((i + 1 < n) & (i % 2 == 1))
    def _():
        fetch(i + 1, buf0_ref, 0)

    rows4 = jnp.concatenate([pos_ref[0:2, :], gate_ref[0:2, :], jnp.zeros((sub - 4, tm), F32)], axis=0)
    cols = jnp.concatenate([rows4] * (LANES // sub), axis=0).T
    lane = lax.broadcasted_iota(jnp.int32, (tm, na), 1)
    pick0 = jnp.where(lane == cols[:, 0:1].astype(jnp.int32), 1.0, 0.0).astype(BF16)
    pick1 = jnp.where(lane == cols[:, 1:2].astype(jnp.int32), 1.0, 0.0).astype(BF16)

    def finish(buf, s):
        pltpu.make_async_copy(_tokens(ys_ref, 0, na), buf, sem.at[s]).wait()
        y = jnp.concatenate([buf[pl.ds(j, na, stride=sub), :] for j in range(d // LANES)], axis=1).astype(BF16)
        y0 = jnp.dot(pick0, y, preferred_element_type=F32)
        y1 = jnp.dot(pick1, y, preferred_element_type=F32)
        x = x1_ref[...] + (cols[:, 2:3] * y0 + cols[:, 3:4] * y1)
        o_ref[...] = _rms(x, g_ref[...]).reshape(o_ref.shape)

    @pl.when(i % 2 == 0)
    def _():
        finish(buf0_ref, 0)

    @pl.when(i % 2 == 1)
    def _():
        finish(buf1_ref, 1)


def _combine(cnt_t, off_t, gst_t, x1, pos, gate, ys, g, nb, seq):
    t, d = x1.shape
    tq = TIME_CHUNK
    tm = nb * tq
    sub = SUBLANES
    grid_spec = pltpu.PrefetchScalarGridSpec(
        num_scalar_prefetch=3,
        grid=(t // tm,),
        in_specs=[pl.BlockSpec((tm, d), lambda i, *_: (i, 0)),
                  pl.BlockSpec((sub, tm), lambda i, *_: (0, i)),
                  pl.BlockSpec((sub, tm), lambda i, *_: (0, i)),
                  pl.BlockSpec(memory_space=pl.ANY),
                  _full(g)],
        out_specs=pl.BlockSpec((nb, tq, d), lambda i, *_: (0, i, 0)),
        scratch_shapes=[pltpu.VMEM((2 * tm * sub, LANES), F32), pltpu.VMEM((2 * tm * sub, LANES), F32),
                        pltpu.SemaphoreType.DMA((2,))],
    )
    return pl.pallas_call(
        _combine_kernel,
        grid_spec=grid_spec,
        out_shape=jax.ShapeDtypeStruct((nb, seq, d), F32),
        compiler_params=_params(),
        name="combine",
    )(cnt_t, off_t, gst_t, x1, pos, gate, ys, g)


def _s5_tables(lam_re, lam_im, log_dt, b_re, b_im, c_re, c_im):
    ng, p = lam_re.shape
    npair = ng // 2
    kper = LANES // (2 * S5_GROUP)
    lam = lax.complex(lam_re, lam_im)
    dt = jnp.exp(log_dt)[:, None]
    lam_bar = jnp.exp(lam * dt)
    b_bar = ((lam_bar - 1.0) / lam)[..., None] * lax.complex(b_re, b_im)
    lr = jnp.real(lam_bar).reshape(npair, 2 * p)
    li = jnp.imag(lam_bar).reshape(npair, 2 * p)
    eye2 = jnp.eye(2, dtype=F32)
    sel = jax.nn.one_hot(jnp.arange(npair) % kper, kper, dtype=F32)

    bv = jnp.stack([jnp.real(b_bar), jnp.imag(b_bar)])
    bv = bv.reshape(2, npair, 2, p, S5_GROUP).transpose(1, 2, 4, 0, 3)
    bblk = bv[:, :, :, :, None, :] * eye2[None, :, None, None, :, None]
    bblk = bblk.reshape(npair, 2 * S5_GROUP, 4 * p)
    bm = (sel[:, :, None, None] * bblk[:, None]).reshape(npair, LANES, 4 * p)

    cv = jnp.stack([c_re, -c_im])
    cv = cv.reshape(2, npair, 2, S5_GROUP, p).transpose(1, 0, 2, 4, 3)
    cblk = cv[:, :, :, :, None, :] * eye2[None, None, :, None, :, None]
    cblk = cblk.reshape(npair, 4 * p, 2 * S5_GROUP)
    cm = (cblk[:, :, None, :] * sel[:, None, :, None]).reshape(npair, 4 * p, LANES)
    return bm.astype(BF16), lr, li, cm.astype(BF16)


def _blockdiag_tiles(w):
    nh, hi, ho = w.shape
    per = MXU_DIM // hi
    eye = jnp.eye(per, dtype=w.dtype)
    t = w.reshape(nh // per, per, hi, ho)[:, :, :, None, :] * eye[None, :, None, :, None]
    return t.reshape(nh // per, per * hi, per * ho)


def kernel(x, norm_mix, w_in, s5_lam_re, s5_lam_im, s5_log_dt, s5_b_re, s5_b_im, s5_c_re, s5_c_im, s5_d,
           w_s5_out, conv_w, conv_b, lru_w_r, lru_b_r, lru_w_i, lru_b_i, lru_lambda, w_lru_out, w_o, norm_ffn,
           router_group_w, router_group_b, router_expert_w, router_expert_b, expert_w_gate, expert_w_up,
           expert_w_down, norm_final):
    nb, seq, d = x.shape
    t = nb * seq
    assert w_in.shape[0] == 1, "one layer: the final RMSNorm is fused into the layer's combine kernel"
    l = 0
    s5w = s5_d.shape[-1]
    lruw = conv_b.shape[-1]
    rows = nb * TIME_CHUNK

    r = jnp.arange(rows)
    perm = jax.nn.one_hot((r % nb) * TIME_CHUNK + r // nb, rows, dtype=BF16)

    u, xl, gg, ga, gb = _inproj(x, norm_mix[l][None, :], perm, w_in[l].astype(BF16), s5w, lruw)

    bm, lr, li, cm = _s5_tables(s5_lam_re[l], s5_lam_im[l], s5_log_dt[l], s5_b_re[l], s5_b_im[l],
                                s5_c_re[l], s5_c_im[l])
    ya = _s5(u, bm, lr, li, cm, s5_d[l][None, :], w_s5_out[l].astype(BF16), ga, nb)

    sp = jax.nn.softplus(-lru_lambda[l])[None, :]
    yb = _lru(xl, gg, gb, conv_w[l], conv_b[l][None, :], _blockdiag_tiles(lru_w_r[l]).astype(BF16),
              _blockdiag_tiles(lru_w_i[l]).astype(BF16), lru_b_r[l][None, :], lru_b_i[l][None, :], sp,
              w_lru_out[l].astype(BF16), nb)

    gap = jnp.zeros((SUBLANES - N_GROUPS, d), F32)
    wrt = jnp.concatenate([router_group_w[l].T, gap, router_expert_w[l].T], axis=0)
    wr_hi = wrt.astype(BF16)
    wr_lo = (wrt - wr_hi.astype(F32)).astype(BF16)
    rb = jnp.concatenate([router_group_b[l], jnp.zeros((SUBLANES - N_GROUPS,), F32), router_expert_b[l]])
    rbias = jnp.broadcast_to(rb[:, None], (ROUTER_ROWS, LANES))

    x1, h2, eid, gate, tcnt = _mixroute(x, ya, yb, perm.T, w_o[l].astype(BF16), norm_ffn[l][None, :],
                                        wr_hi, wr_lo, rbias)

    n_tiles = seq // TIME_CHUNK
    cnt_t = tcnt.reshape(n_tiles, SUBLANES, LANES)[:, 0, :N_EXPERTS].astype(jnp.int32)
    off_t = jnp.cumsum(cnt_t, axis=1) - cnt_t
    counts = jnp.sum(cnt_t, axis=0)
    padded = ((counts + SLOT_BLOCK - 1) // SLOT_BLOCK) * SLOT_BLOCK
    pad_ends = jnp.cumsum(padded)
    pad_starts = pad_ends - padded
    gst_t = pad_starts[None, :] + jnp.cumsum(cnt_t, axis=0) - cnt_t
    offcol = jnp.broadcast_to(off_t.astype(F32)[:, :, None], (n_tiles, N_EXPERTS, LANES))
    n_blocks = -(-(2 * t) // SLOT_BLOCK) + N_EXPERTS
    n_slots = n_blocks * SLOT_BLOCK
    block_start = jnp.arange(n_blocks, dtype=jnp.int32) * SLOT_BLOCK
    block_e = jnp.minimum(jnp.sum((pad_ends[None, :] <= block_start[:, None]).astype(jnp.int32), axis=1),
                          N_EXPERTS - 1)
    n_used = pad_ends[-1:] // SLOT_BLOCK
    cnt_f, off_f, gst_f = cnt_t.reshape(-1), off_t.reshape(-1), gst_t.reshape(-1)

    tri = jnp.triu(jnp.ones((rows, rows), BF16), k=1)
    xs, pos = _dispatch(cnt_f, off_f, gst_f, counts, pad_starts, pad_ends, h2, eid, offcol, tri, n_slots)
    ys = _ffn(block_e, n_used, xs, expert_w_gate[l], expert_w_up[l], expert_w_down[l])
    return _combine(cnt_f, off_f, gst_f, x1, pos, gate, ys, norm_final[None, :], nb, seq)
```

```python
import functools
import math

import jax
import jax.numpy as jnp
from jax import lax
from jax.experimental import pallas as pl
from jax.experimental.pallas import tpu as pltpu

F32 = jnp.float32
BF16 = jnp.bfloat16

RMS_EPS = 1e-6
S5_GROUP = 16
CONV_WIDTH = 4
LRU_C = 8.0
N_GROUPS = 4
EXPERTS_PER_GROUP = 8
N_EXPERTS = N_GROUPS * EXPERTS_PER_GROUP

LANES = 128
SUBLANES = 8
MXU_DIM = 256
TIME_CHUNK = 64
SCAN_UNROLL = 4
SLOT_BLOCK = 512
RARE_STRIP = 64
ROUTER_ROWS = 40
VMEM_LIMIT = 56 * 1024 * 1024


def _gelu(x):
    c = math.sqrt(2.0 / math.pi)
    return 0.5 * x * (1.0 + jnp.tanh(c * (x + 0.044715 * (x * x * x))))


def _sigmoid(x):
    return 0.5 * jnp.tanh(0.5 * x) + 0.5


def _rms(x, g):
    return x * lax.rsqrt(jnp.mean(x * x, axis=-1, keepdims=True) + RMS_EPS) * g


def _params():
    return pltpu.CompilerParams(dimension_semantics=("arbitrary",), vmem_limit_bytes=VMEM_LIMIT)


def _full(a):
    return pl.BlockSpec(a.shape, lambda i, *_: (0,) * a.ndim)


def _tile(t):
    return pl.ds(pl.multiple_of(t * SUBLANES, SUBLANES), SUBLANES)


def _inproj_kernel(x_ref, g_ref, p_ref, w_ref, u_ref, xl_ref, gg_ref, ga_ref, gb_ref, *, s5w, lruw):
    nb, tq, d = x_ref.shape
    h = _rms(x_ref[...].reshape(nb * tq, d), g_ref[...])
    hb = jnp.dot(p_ref[...], h.astype(BF16), preferred_element_type=F32).astype(BF16)

    def proj(lo, hi):
        return jnp.dot(hb, w_ref[:, lo:hi], preferred_element_type=F32)

    o1 = s5w
    o2 = o1 + lruw
    o3 = o2 + lruw
    o4 = o3 + d
    u_ref[...] = proj(0, o1)
    xl_ref[...] = proj(o1, o2)
    gg_ref[...] = _gelu(proj(o2, o3)).astype(BF16)
    ga_ref[...] = _sigmoid(proj(o3, o4)).astype(BF16)
    gb_ref[...] = _sigmoid(proj(o4, o4 + d)).astype(BF16)


def _inproj(x, g, perm, w_in_b, s5w, lruw):
    nb, seq, d = x.shape
    tq = TIME_CHUNK
    rows = nb * tq
    t = nb * seq
    row = lambda c: pl.BlockSpec((rows, c), lambda i: (i, 0))
    return pl.pallas_call(
        functools.partial(_inproj_kernel, s5w=s5w, lruw=lruw),
        grid=(seq // tq,),
        in_specs=[pl.BlockSpec((nb, tq, d), lambda i: (0, i, 0)), _full(g), _full(perm), _full(w_in_b)],
        out_specs=[row(s5w), row(lruw), row(lruw), row(d), row(d)],
        out_shape=[jax.ShapeDtypeStruct((t, s5w), F32), jax.ShapeDtypeStruct((t, lruw), F32),
                   jax.ShapeDtypeStruct((t, lruw), BF16), jax.ShapeDtypeStruct((t, d), BF16),
                   jax.ShapeDtypeStruct((t, d), BF16)],
        compiler_params=_params(),
        name="inproj",
    )(x, g, perm, w_in_b)


def _mixers_kernel(u_ref, xl_ref, gg_ref, ga_ref, gb_ref,
                   bm_ref, lr_ref, li_ref, cm_ref, dk_ref, wo_ref,
                   cw_ref, cb_ref, wr_ref, wi_ref, br_ref, bi_ref, sp_ref, wlo_ref,
                   o_ref, state_ref, sbuf_ref, h_ref, xprev_ref, a_ref, b_ref, *, tq, nb):
    d = o_ref.shape[-1]
    rows = nb * tq
    npair = bm_ref.shape[0]
    kper = LANES // (2 * S5_GROUP)
    w = xl_ref.shape[-1]
    nct = w // LANES
    halo = (CONV_WIDTH - 1) * nb

    @pl.when(pl.program_id(0) == 0)
    def _():
        state_ref[...] = jnp.zeros_like(state_ref)
        h_ref[...] = jnp.zeros_like(h_ref)
        xprev_ref[...] = jnp.zeros_like(xprev_ref)

    xin = xl_ref[...]
    xext = jnp.concatenate([xprev_ref[...], xin], axis=0)
    xprev_ref[...] = xin[rows - halo:, :]
    xc_all = cb_ref[...]
    for k in range(CONV_WIDTH):
        xc_all = xc_all + xext[k * nb:k * nb + rows, :] * cw_ref[k:k + 1, :]

    per = MXU_DIM // LANES
    for j in range(wr_ref.shape[0]):
        cs = slice(MXU_DIM * j, MXU_DIM * (j + 1))
        xc = xc_all[:, cs]
        xcb = xc.astype(BF16)
        r = _sigmoid(jnp.dot(xcb, wr_ref[j], preferred_element_type=F32) + br_ref[:, cs])
        ig = _sigmoid(jnp.dot(xcb, wi_ref[j], preferred_element_type=F32) + bi_ref[:, cs])
        log_a = (-LRU_C) * r * sp_ref[:, cs]
        th = jnp.tanh(log_a)
        mult = jnp.sqrt((-2.0 * th) / (1.0 - th))
        a = jnp.exp(log_a)
        b = mult * (ig * xc)
        for i in range(per):
            a_ref[per * j + i] = a[:, LANES * i:LANES * (i + 1)]
            b_ref[per * j + i] = b[:, LANES * i:LANES * (i + 1)]

    u2 = u_ref[...]
    ub = u2.astype(BF16)
    for k in range(npair):
        kb = k // kper
        bu = jnp.dot(ub[:, LANES * kb:LANES * (kb + 1)], bm_ref[k], preferred_element_type=F32)
        sbuf_ref[2 * k] = bu[:, :LANES]
        sbuf_ref[2 * k + 1] = bu[:, LANES:]

    pairs_per_loop = 4
    for kk in range(npair // pairs_per_loop):
        k0 = kk * pairs_per_loop
        lr = [jnp.broadcast_to(lr_ref[k0 + j:k0 + j + 1, :], (nb, LANES)) for j in range(pairs_per_loop)]
        li = [jnp.broadcast_to(li_ref[k0 + j:k0 + j + 1, :], (nb, LANES)) for j in range(pairs_per_loop)]
        init = tuple(state_ref[2 * k0 + j] for j in range(2 * pairs_per_loop))

        def body(it, carry, k0=k0, lr=lr, li=li):
            carry = list(carry)
            for s in range(SCAN_UNROLL):
                rows = _tile(it * SCAN_UNROLL + s)
                for j in range(pairs_per_loop):
                    sre, sim = carry[2 * j], carry[2 * j + 1]
                    nre = lr[j] * sre - li[j] * sim + sbuf_ref[2 * (k0 + j), rows, :]
                    nim = lr[j] * sim + li[j] * sre + sbuf_ref[2 * (k0 + j) + 1, rows, :]
                    sbuf_ref[2 * (k0 + j), rows, :] = nre
                    sbuf_ref[2 * (k0 + j) + 1, rows, :] = nim
                    carry[2 * j], carry[2 * j + 1] = nre, nim
            return tuple(carry)

        fin = lax.fori_loop(0, tq // SCAN_UNROLL, body, init)
        for j in range(2 * pairs_per_loop):
            state_ref[2 * k0 + j] = fin[j]

    def scan_body(it, hs):
        hs = list(hs)
        for s in range(SCAN_UNROLL):
            tr = _tile(it * SCAN_UNROLL + s)
            for c in range(nct):
                hs[c] = a_ref[c, tr, :] * hs[c] + b_ref[c, tr, :]
                b_ref[c, tr, :] = hs[c]
        return tuple(hs)

    hfin = lax.fori_loop(0, tq // SCAN_UNROLL, scan_body, tuple(h_ref[c] for c in range(nct)))
    for c in range(nct):
        h_ref[c] = hfin[c]

    ys = []
    for kb in range(npair // kper):
        acc = None
        for j in range(kper):
            k = kb * kper + j
            s = jnp.concatenate([sbuf_ref[2 * k], sbuf_ref[2 * k + 1]], axis=1).astype(BF16)
            p = jnp.dot(s, cm_ref[k], preferred_element_type=F32)
            acc = p if acc is None else acc + p
        ys.append(acc)
    y = jnp.concatenate(ys, axis=1) + dk_ref[...] * u2
    z = jnp.dot(_gelu(y).astype(BF16), wo_ref[...], preferred_element_type=F32)
    ya = z[:, :d] * _sigmoid(z[:, d:])

    yl = jnp.concatenate([b_ref[c] for c in range(nct)], axis=1) * gg_ref[...].astype(F32)
    yb = jnp.dot(yl.astype(BF16), wlo_ref[...], preferred_element_type=F32)
    o_ref[...] = (ya * ga_ref[...].astype(F32) + yb * gb_ref[...].astype(F32)).astype(BF16)


def _mixers(u, xl, gg, ga, gb, s5p, lrup, nb):
    t, s5w = u.shape
    w = xl.shape[-1]
    d = ga.shape[-1]
    tq = TIME_CHUNK
    rows = nb * tq
    nstate = s5p[0].shape[0] * MXU_DIM
    row = lambda c: pl.BlockSpec((rows, c), lambda i: (i, 0))
    params = tuple(s5p) + tuple(lrup)
    return pl.pallas_call(
        functools.partial(_mixers_kernel, tq=tq, nb=nb),
        grid=(t // rows,),
        in_specs=[row(s5w), row(w), row(w), row(d), row(d)] + [_full(p) for p in params],
        out_specs=row(d),
        out_shape=jax.ShapeDtypeStruct((t, d), BF16),
        scratch_shapes=[pltpu.VMEM((nstate // LANES, nb, LANES), F32),
                        pltpu.VMEM((nstate // LANES, rows, LANES), F32),
                        pltpu.VMEM((w // LANES, nb, LANES), F32),
                        pltpu.VMEM(((CONV_WIDTH - 1) * nb, w), F32),
                        pltpu.VMEM((w // LANES, rows, LANES), F32),
                        pltpu.VMEM((w // LANES, rows, LANES), F32)],
        compiler_params=_params(),
        name="mixers",
    )(u, xl, gg, ga, gb, *params)


def _mixroute_kernel(x_ref, m_ref, pt_ref, wo_ref, g_ref, wh_ref, wl_ref, rb_ref,
                     x1_ref, h2_ref, eid_ref, gate_ref, cnt_ref):
    nb, tq, d = x_ref.shape
    tm = nb * tq
    m = jnp.dot(pt_ref[...], m_ref[...], preferred_element_type=F32).astype(BF16)
    x1 = x_ref[...].reshape(tm, d) + jnp.dot(m, wo_ref[...], preferred_element_type=F32)
    x1_ref[...] = x1
    h = _rms(x1, g_ref[...])

    hh = h.astype(BF16)
    h2_ref[...] = hh
    hl = (h - hh.astype(F32)).astype(BF16)
    dn = (((1,), (1,)), ((), ()))
    lt = (lax.dot_general(wh_ref[...], hh, dn, preferred_element_type=F32)
          + lax.dot_general(wh_ref[...], hl, dn, preferred_element_type=F32)
          + lax.dot_general(wl_ref[...], hh, dn, preferred_element_type=F32))
    lt = lt + rb_ref[:, 0:1]

    gl = lt[0:N_GROUPS, :]
    gmax = jnp.max(gl, axis=0, keepdims=True)
    gsum = jnp.sum(jnp.exp(gl - gmax), axis=0, keepdims=True)
    p_g = 1.0 / gsum
    iota_g = lax.broadcasted_iota(jnp.int32, gl.shape, 0)
    g_idx = jnp.min(jnp.where(gl == gmax, iota_g, N_GROUPS), axis=0, keepdims=True)

    el = lt[SUBLANES:SUBLANES + EXPERTS_PER_GROUP, :]
    for g in range(1, N_GROUPS):
        lo = SUBLANES + EXPERTS_PER_GROUP * g
        el = jnp.where(g_idx == g, lt[lo:lo + EXPERTS_PER_GROUP, :], el)
    emax = jnp.max(el, axis=0, keepdims=True)
    esum = jnp.sum(jnp.exp(el - emax), axis=0, keepdims=True)
    iota_e = lax.broadcasted_iota(jnp.int32, el.shape, 0)
    i1 = jnp.min(jnp.where(el == emax, iota_e, EXPERTS_PER_GROUP), axis=0, keepdims=True)
    el2 = jnp.where(iota_e == i1, -jnp.inf, el)
    emax2 = jnp.max(el2, axis=0, keepdims=True)
    i2 = jnp.min(jnp.where(el2 == emax2, iota_e, EXPERTS_PER_GROUP), axis=0, keepdims=True)
    p1 = 1.0 / esum
    p2 = jnp.exp(emax2 - emax) / esum
    psum = p1 + p2
    g1 = p_g * (p1 / psum)
    g2 = p_g * (p2 / psum)
    e1 = g_idx * EXPERTS_PER_GROUP + i1
    e2 = g_idx * EXPERTS_PER_GROUP + i2
    eid_ref[...] = jnp.concatenate([e1, e2, jnp.zeros((SUBLANES - 2, tm), jnp.int32)], axis=0)
    gate_ref[...] = jnp.concatenate([g1, g2, jnp.zeros((SUBLANES - 2, tm), F32)], axis=0)

    iota = lax.broadcasted_iota(jnp.int32, (LANES, tm), 0)
    hits = jnp.where((iota == e1) | (iota == e2), 1.0, 0.0).astype(BF16)
    cnt_ref[...] = lax.dot_general(jnp.ones((SUBLANES, tm), BF16), hits, dn, preferred_element_type=F32)


def _mixroute(x, m, perm_t, w_o_b, g, wr_hi, wr_lo, rbias):
    nb, seq, d = x.shape
    tq = TIME_CHUNK
    tm = nb * tq
    t = nb * seq
    row = lambda c: pl.BlockSpec((tm, c), lambda i: (i, 0))
    col = pl.BlockSpec((SUBLANES, tm), lambda i: (0, i))
    return pl.pallas_call(
        _mixroute_kernel,
        grid=(seq // tq,),
        in_specs=[pl.BlockSpec((nb, tq, d), lambda i: (0, i, 0)), row(d), _full(perm_t), _full(w_o_b),
                  _full(g), _full(wr_hi), _full(wr_lo), _full(rbias)],
        out_specs=[row(d), row(d), col, col, pl.BlockSpec((SUBLANES, LANES), lambda i: (i, 0))],
        out_shape=[jax.ShapeDtypeStruct((t, d), F32), jax.ShapeDtypeStruct((t, d), BF16),
                   jax.ShapeDtypeStruct((SUBLANES, t), jnp.int32), jax.ShapeDtypeStruct((SUBLANES, t), F32),
                   jax.ShapeDtypeStruct((seq // tq * SUBLANES, LANES), F32)],
        compiler_params=_params(),
        name="mixroute",
    )(x, m, perm_t, w_o_b, g, wr_hi, wr_lo, rbias)


def _tokens(ref, tok, n):
    return ref.at[pl.ds(pl.multiple_of(tok * SUBLANES, SUBLANES), n * SUBLANES)]


def _strip_copies(n, src, src_tok, dst, dst_tok, sem, max_tokens, wait=False):
    def piece(bit):
        done = n & (-2 * bit)

        @pl.when((n & bit) != 0)
        def _():
            cp = pltpu.make_async_copy(_tokens(src, src_tok + done, bit), _tokens(dst, dst_tok + done, bit), sem)
            if wait:
                cp.wait()
            else:
                cp.start()

    bits = [1 << b for b in reversed(range(max_tokens.bit_length()))]
    rare = [b for b in bits if b >= RARE_STRIP]
    if rare:
        @pl.when(n >= RARE_STRIP)
        def _():
            for b in rare:
                piece(b)
    for b in bits:
        if b < RARE_STRIP:
            piece(b)


def _tile_positions(eid_ref, offcol_ref, tri_ref):
    tm = eid_ref.shape[1]
    e0 = eid_ref[0:1, :]
    e1 = eid_ref[1:2, :]
    iota = lax.broadcasted_iota(jnp.int32, (N_EXPERTS, tm), 0)
    oh0 = iota == e0
    oh1 = iota == e1
    c = jnp.where(oh0 | oh1, 1.0, 0.0)
    before = jnp.dot(c.astype(BF16), tri_ref[...], preferred_element_type=F32) + offcol_ref[:, 0:1]
    p0 = jnp.sum(jnp.where(oh0, before, 0.0), axis=0, keepdims=True)
    p1 = jnp.sum(jnp.where(oh1, before, 0.0), axis=0, keepdims=True)
    return p0, p1


def _dispatch_kernel(cnt_ref, off_ref, gst_ref, tot_ref, pst_ref, pen_ref,
                     h2_ref, eid_ref, offcol_ref, tri_ref, xs_ref, pos_ref,
                     stage0_ref, stage1_ref, zero_ref, sem, zsem):
    i = pl.program_id(0)
    last = pl.num_programs(0) - 1
    tm = eid_ref.shape[1]
    na = 2 * tm
    d = h2_ref.shape[1]

    p0, p1 = _tile_positions(eid_ref, offcol_ref, tri_ref)
    pos_ref[...] = jnp.concatenate([p0, p1, jnp.zeros((SUBLANES - 2, tm), F32)], axis=0)
    p0i = p0.astype(jnp.int32)
    p1i = p1.astype(jnp.int32)
    rid = lax.broadcasted_iota(jnp.int32, (na, tm), 0)
    onehot = jnp.where((rid == p0i) | (rid == p1i), 1.0, 0.0).astype(BF16)
    srt = jnp.dot(onehot, h2_ref[...], preferred_element_type=F32)

    def step(stage, s):
        @pl.when(i >= 2)
        def _():
            pltpu.make_async_copy(stage, _tokens(xs_ref, 0, na), sem.at[s]).wait()

        for j in range(d // LANES):
            stage[pl.ds(j, na, stride=d // LANES), :] = srt[:, LANES * j:LANES * (j + 1)]

        def per_expert(e, c):
            k = i * N_EXPERTS + e
            _strip_copies(cnt_ref[k], stage, off_ref[k], xs_ref, gst_ref[k], sem.at[s], na)
            return c

        lax.fori_loop(0, N_EXPERTS, per_expert, 0)

    @pl.when(i % 2 == 0)
    def _():
        step(stage0_ref, 0)

    @pl.when(i % 2 == 1)
    def _():
        step(stage1_ref, 1)

    @pl.when(i == last)
    def _():
        zero_ref[...] = jnp.zeros_like(zero_ref)
        for wait in (False, True):
            for e in range(N_EXPERTS):
                _strip_copies(pen_ref[e] - pst_ref[e] - tot_ref[e], zero_ref, 0, xs_ref,
                              pst_ref[e] + tot_ref[e], zsem, SLOT_BLOCK - 1, wait=wait)

        nblk = xs_ref.shape[0] // zero_ref.shape[0]

        def bcopy(b):
            return pltpu.make_async_copy(zero_ref, _tokens(xs_ref, b * SLOT_BLOCK, SLOT_BLOCK), zsem)

        first_unused = pen_ref[N_EXPERTS - 1] // SLOT_BLOCK
        lax.fori_loop(first_unused, nblk, lambda b, c: (bcopy(b).start(), c)[1], 0)
        lax.fori_loop(first_unused, nblk, lambda b, c: (bcopy(b).wait(), c)[1], 0)

        @pl.when(i >= 1)
        def _():
            @pl.when(i % 2 == 0)
            def _():
                pltpu.make_async_copy(stage1_ref, _tokens(xs_ref, 0, na), sem.at[1]).wait()

            @pl.when(i % 2 == 1)
            def _():
                pltpu.make_async_copy(stage0_ref, _tokens(xs_ref, 0, na), sem.at[0]).wait()

        @pl.when(i % 2 == 0)
        def _():
            pltpu.make_async_copy(stage0_ref, _tokens(xs_ref, 0, na), sem.at[0]).wait()

        @pl.when(i % 2 == 1)
        def _():
            pltpu.make_async_copy(stage1_ref, _tokens(xs_ref, 0, na), sem.at[1]).wait()


def _dispatch(cnt_t, off_t, gst_t, tot, pst, pen, h2, eid, offcol, tri, n_slots):
    sub = SUBLANES
    t, d = h2.shape
    tm = tri.shape[0]
    grid_spec = pltpu.PrefetchScalarGridSpec(
        num_scalar_prefetch=6,
        grid=(t // tm,),
        in_specs=[pl.BlockSpec((tm, d), lambda i, *_: (i, 0)),
                  pl.BlockSpec((sub, tm), lambda i, *_: (0, i)),
                  pl.BlockSpec((None, N_EXPERTS, LANES), lambda i, *_: (i, 0, 0)),
                  _full(tri)],
        out_specs=[pl.BlockSpec(memory_space=pl.ANY), pl.BlockSpec((sub, tm), lambda i, *_: (0, i))],
        scratch_shapes=[pltpu.VMEM((2 * tm * sub, LANES), F32), pltpu.VMEM((2 * tm * sub, LANES), F32),
                        pltpu.VMEM((SLOT_BLOCK * sub, LANES), F32), pltpu.SemaphoreType.DMA((2,)),
                        pltpu.SemaphoreType.DMA(())],
    )
    return pl.pallas_call(
        _dispatch_kernel,
        grid_spec=grid_spec,
        out_shape=[jax.ShapeDtypeStruct((n_slots * sub, LANES), F32), jax.ShapeDtypeStruct((sub, t), F32)],
        compiler_params=_params(),
        name="dispatch",
    )(cnt_t, off_t, gst_t, tot, pst, pen, h2, eid, offcol, tri)


def _ffn_kernel(be_ref, nu_ref, xs_ref, wg_ref, wu_ref, wd_ref, ys_ref, wgb_ref, wub_ref, wdb_ref, *, blk):
    i = pl.program_id(0)
    sub = SUBLANES

    @pl.when(i < nu_ref[0])
    def _():
        prev = be_ref[jnp.maximum(i - 1, 0)]

        @pl.when((i == 0) | (be_ref[i] != prev))
        def _():
            wgb_ref[...] = wg_ref[...].astype(BF16)
            wub_ref[...] = wu_ref[...].astype(BF16)
            wdb_ref[...] = wd_ref[...].astype(BF16)

        x = jnp.concatenate([xs_ref[pl.ds(j, blk, stride=sub), :] for j in range(sub)], axis=1).astype(BF16)
        g = jnp.dot(x, wgb_ref[...], preferred_element_type=F32)
        u = jnp.dot(x, wub_ref[...], preferred_element_type=F32)
        a = (g * _sigmoid(g)) * u
        y = jnp.dot(a.astype(BF16), wdb_ref[...], preferred_element_type=F32)
        for j in range(sub):
            ys_ref[pl.ds(j, blk, stride=sub), :] = y[:, LANES * j:LANES * (j + 1)]

    @pl.when(i >= nu_ref[0])
    def _():
        ys_ref[...] = jnp.zeros_like(ys_ref)


def _ffn(block_e, n_used, xs, wg, wu, wd):
    sub = SUBLANES
    blk = SLOT_BLOCK
    n_blocks = xs.shape[0] // (sub * blk)
    ne, d, de = wg.shape

    def slot_map(i, be, nu):
        return (i, 0)

    def w_map(i, be, nu):
        return (be[jnp.minimum(i, nu[0] - 1)], 0, 0)

    grid_spec = pltpu.PrefetchScalarGridSpec(
        num_scalar_prefetch=2,
        grid=(n_blocks,),
        in_specs=[pl.BlockSpec((blk * sub, LANES), slot_map),
                  pl.BlockSpec((None, d, de), w_map), pl.BlockSpec((None, d, de), w_map),
                  pl.BlockSpec((None, de, d), w_map)],
        out_specs=pl.BlockSpec((blk * sub, LANES), slot_map),
        scratch_shapes=[pltpu.VMEM((d, de), BF16), pltpu.VMEM((d, de), BF16), pltpu.VMEM((de, d), BF16)],
    )
    return pl.pallas_call(
        functools.partial(_ffn_kernel, blk=blk),
        grid_spec=grid_spec,
        out_shape=jax.ShapeDtypeStruct(xs.shape, F32),
        compiler_params=_params(),
        name="ffn",
    )(block_e, n_used, xs, wg, wu, wd)


def _combine_kernel(cnt_ref, off_ref, gst_ref, x1_ref, pos_ref, gate_ref, ys_ref, g_ref, o_ref,
                    buf0_ref, buf1_ref, sem):
    i = pl.program_id(0)
    n = pl.num_programs(0)
    tm, d = x1_ref.shape
    na = 2 * tm
    sub = SUBLANES

    def fetch(tile, buf, s):
        def per_expert(e, c):
            k = tile * N_EXPERTS + e
            _strip_copies(cnt_ref[k], ys_ref, gst_ref[k], buf, off_ref[k], sem.at[s], na)
            return c

        lax.fori_loop(0, N_EXPERTS, per_expert, 0)

    @pl.when(i == 0)
    def _():
        fetch(0, buf0_ref, 0)

    @pl.when((i + 1 < n) & (i % 2 == 0))
    def _():
        fetch(i + 1, buf1_ref, 1)

    @pl.when((i + 1 < n) & (i % 2 == 1))
    def _():
        fetch(i + 1, buf0_ref, 0)

    rows4 = jnp.concatenate([pos_ref[0:2, :], gate_ref[0:2, :], jnp.zeros((sub - 4, tm), F32)], axis=0)
    cols = jnp.concatenate([rows4] * (LANES // sub), axis=0).T
    lane = lax.broadcasted_iota(jnp.int32, (tm, na), 1)
    pick0 = jnp.where(lane == cols[:, 0:1].astype(jnp.int32), 1.0, 0.0).astype(BF16)
    pick1 = jnp.where(lane == cols[:, 1:2].astype(jnp.int32), 1.0, 0.0).astype(BF16)

    def finish(buf, s):
        pltpu.make_async_copy(_tokens(ys_ref, 0, na), buf, sem.at[s]).wait()
        y = jnp.concatenate([buf[pl.ds(j, na, stride=sub), :] for j in range(d // LANES)], axis=1).astype(BF16)
        y0 = jnp.dot(pick0, y, preferred_element_type=F32)
        y1 = jnp.dot(pick1, y, preferred_element_type=F32)
        x = x1_ref[...] + (cols[:, 2:3] * y0 + cols[:, 3:4] * y1)
        o_ref[...] = _rms(x, g_ref[...]).reshape(o_ref.shape)

    @pl.when(i % 2 == 0)
    def _():
        finish(buf0_ref, 0)

    @pl.when(i % 2 == 1)
    def _():
        finish(buf1_ref, 1)


def _combine(cnt_t, off_t, gst_t, x1, pos, gate, ys, g, nb, seq):
    t, d = x1.shape
    tq = TIME_CHUNK
    tm = nb * tq
    sub = SUBLANES
    grid_spec = pltpu.PrefetchScalarGridSpec(
        num_scalar_prefetch=3,
        grid=(t // tm,),
        in_specs=[pl.BlockSpec((tm, d), lambda i, *_: (i, 0)),
                  pl.BlockSpec((sub, tm), lambda i, *_: (0, i)),
                  pl.BlockSpec((sub, tm), lambda i, *_: (0, i)),
                  pl.BlockSpec(memory_space=pl.ANY),
                  _full(g)],
        out_specs=pl.BlockSpec((nb, tq, d), lambda i, *_: (0, i, 0)),
        scratch_shapes=[pltpu.VMEM((2 * tm * sub, LANES), F32), pltpu.VMEM((2 * tm * sub, LANES), F32),
                        pltpu.SemaphoreType.DMA((2,))],
    )
    return pl.pallas_call(
        _combine_kernel,
        grid_spec=grid_spec,
        out_shape=jax.ShapeDtypeStruct((nb, seq, d), F32),
        compiler_params=_params(),
        name="combine",
    )(cnt_t, off_t, gst_t, x1, pos, gate, ys, g)


def _s5_tables(lam_re, lam_im, log_dt, b_re, b_im, c_re, c_im):
    ng, p = lam_re.shape
    npair = ng // 2
    kper = LANES // (2 * S5_GROUP)
    lam = lax.complex(lam_re, lam_im)
    dt = jnp.exp(log_dt)[:, None]
    lam_bar = jnp.exp(lam * dt)
    b_bar = ((lam_bar - 1.0) / lam)[..., None] * lax.complex(b_re, b_im)
    lr = jnp.real(lam_bar).reshape(npair, 2 * p)
    li = jnp.imag(lam_bar).reshape(npair, 2 * p)
    eye2 = jnp.eye(2, dtype=F32)
    sel = jax.nn.one_hot(jnp.arange(npair) % kper, kper, dtype=F32)

    bv = jnp.stack([jnp.real(b_bar), jnp.imag(b_bar)])
    bv = bv.reshape(2, npair, 2, p, S5_GROUP).transpose(1, 2, 4, 0, 3)
    bblk = bv[:, :, :, :, None, :] * eye2[None, :, None, None, :, None]
    bblk = bblk.reshape(npair, 2 * S5_GROUP, 4 * p)
    bm = (sel[:, :, None, None] * bblk[:, None]).reshape(npair, LANES, 4 * p)

    cv = jnp.stack([c_re, -c_im])
    cv = cv.reshape(2, npair, 2, S5_GROUP, p).transpose(1, 0, 2, 4, 3)
    cblk = cv[:, :, :, :, None, :] * eye2[None, None, :, None, :, None]
    cblk = cblk.reshape(npair, 4 * p, 2 * S5_GROUP)
    cm = (cblk[:, :, None, :] * sel[:, None, :, None]).reshape(npair, 4 * p, LANES)
    return bm.astype(BF16), lr, li, cm.astype(BF16)


def _blockdiag_tiles(w):
    nh, hi, ho = w.shape
    per = MXU_DIM // hi
    eye = jnp.eye(per, dtype=w.dtype)
    t = w.reshape(nh // per, per, hi, ho)[:, :, :, None, :] * eye[None, :, None, :, None]
    return t.reshape(nh // per, per * hi, per * ho)


def kernel(x, norm_mix, w_in, s5_lam_re, s5_lam_im, s5_log_dt, s5_b_re, s5_b_im, s5_c_re, s5_c_im, s5_d,
           w_s5_out, conv_w, conv_b, lru_w_r, lru_b_r, lru_w_i, lru_b_i, lru_lambda, w_lru_out, w_o, norm_ffn,
           router_group_w, router_group_b, router_expert_w, router_expert_b, expert_w_gate, expert_w_up,
           expert_w_down, norm_final):
    nb, seq, d = x.shape
    t = nb * seq
    assert w_in.shape[0] == 1, "one layer: the final RMSNorm is fused into the layer's combine kernel"
    l = 0
    s5w = s5_d.shape[-1]
    lruw = conv_b.shape[-1]
    rows = nb * TIME_CHUNK

    r = jnp.arange(rows)
    perm = jax.nn.one_hot((r % nb) * TIME_CHUNK + r // nb, rows, dtype=BF16)

    u, xl, gg, ga, gb = _inproj(x, norm_mix[l][None, :], perm, w_in[l].astype(BF16), s5w, lruw)

    bm, lr, li, cm = _s5_tables(s5_lam_re[l], s5_lam_im[l], s5_log_dt[l], s5_b_re[l], s5_b_im[l],
                                s5_c_re[l], s5_c_im[l])
    s5p = (bm, lr, li, cm, s5_d[l][None, :], w_s5_out[l].astype(BF16))
    sp = jax.nn.softplus(-lru_lambda[l])[None, :]
    lrup = (conv_w[l], conv_b[l][None, :], _blockdiag_tiles(lru_w_r[l]).astype(BF16),
            _blockdiag_tiles(lru_w_i[l]).astype(BF16), lru_b_r[l][None, :], lru_b_i[l][None, :], sp,
            w_lru_out[l].astype(BF16))
    mixed = _mixers(u, xl, gg, ga, gb, s5p, lrup, nb)

    gap = jnp.zeros((SUBLANES - N_GROUPS, d), F32)
    wrt = jnp.concatenate([router_group_w[l].T, gap, router_expert_w[l].T], axis=0)
    wr_hi = wrt.astype(BF16)
    wr_lo = (wrt - wr_hi.astype(F32)).astype(BF16)
    rb = jnp.concatenate([router_group_b[l], jnp.zeros((SUBLANES - N_GROUPS,), F32), router_expert_b[l]])
    rbias = jnp.broadcast_to(rb[:, None], (ROUTER_ROWS, LANES))

    x1, h2, eid, gate, tcnt = _mixroute(x, mixed, perm.T, w_o[l].astype(BF16), norm_ffn[l][None, :],
                                        wr_hi, wr_lo, rbias)

    n_tiles = seq // TIME_CHUNK
    cnt_t = tcnt.reshape(n_tiles, SUBLANES, LANES)[:, 0, :N_EXPERTS].astype(jnp.int32)
    off_t = jnp.cumsum(cnt_t, axis=1) - cnt_t
    counts = jnp.sum(cnt_t, axis=0)
    padded = ((counts + SLOT_BLOCK - 1) // SLOT_BLOCK) * SLOT_BLOCK
    pad_ends = jnp.cumsum(padded)
    pad_starts = pad_ends - padded
    gst_t = pad_starts[None, :] + jnp.cumsum(cnt_t, axis=0) - cnt_t
    offcol = jnp.broadcast_to(off_t.astype(F32)[:, :, None], (n_tiles, N_EXPERTS, LANES))
    n_blocks = -(-(2 * t) // SLOT_BLOCK) + N_EXPERTS
    n_slots = n_blocks * SLOT_BLOCK
    block_start = jnp.arange(n_blocks, dtype=jnp.int32) * SLOT_BLOCK
    block_e = jnp.minimum(jnp.sum((pad_ends[None, :] <= block_start[:, None]).astype(jnp.int32), axis=1),
                          N_EXPERTS - 1)
    n_used = pad_ends[-1:] // SLOT_BLOCK
    cnt_f, off_f, gst_f = cnt_t.reshape(-1), off_t.reshape(-1), gst_t.reshape(-1)

    tri = jnp.triu(jnp.ones((rows, rows), BF16), k=1)
    xs, pos = _dispatch(cnt_f, off_f, gst_f, counts, pad_starts, pad_ends, h2, eid, offcol, tri, n_slots)
    ys = _ffn(block_e, n_used, xs, expert_w_gate[l], expert_w_up[l], expert_w_down[l])
    return _combine(cnt_f, off_f, gst_f, x1, pos, gate, ys, norm_final[None, :], nb, seq)
```

```python
import functools
import math

import jax
import jax.numpy as jnp
from jax import lax
from jax.experimental import pallas as pl
from jax.experimental.pallas import tpu as pltpu

F32 = jnp.float32
BF16 = jnp.bfloat16

RMS_EPS = 1e-6
S5_GROUP = 16
CONV_WIDTH = 4
LRU_C = 8.0
N_GROUPS = 4
EXPERTS_PER_GROUP = 8
N_EXPERTS = N_GROUPS * EXPERTS_PER_GROUP

LANES = 128
SUBLANES = 8
MXU_DIM = 256
TIME_CHUNK = 64
SCAN_UNROLL = 4
SLOT_BLOCK = 512
RARE_STRIP = 64
ROUTER_ROWS = 40
VMEM_LIMIT = 56 * 1024 * 1024


def _gelu(x):
    c = math.sqrt(2.0 / math.pi)
    return 0.5 * x * (1.0 + jnp.tanh(c * (x + 0.044715 * (x * x * x))))


def _sigmoid(x):
    return 0.5 * jnp.tanh(0.5 * x) + 0.5


def _rms(x, g):
    return x * lax.rsqrt(jnp.mean(x * x, axis=-1, keepdims=True) + RMS_EPS) * g


def _params():
    return pltpu.CompilerParams(dimension_semantics=("arbitrary",), vmem_limit_bytes=VMEM_LIMIT)


def _full(a):
    return pl.BlockSpec(a.shape, lambda i, *_: (0,) * a.ndim)


def _tile(t):
    return pl.ds(pl.multiple_of(t * SUBLANES, SUBLANES), SUBLANES)


def _inproj_kernel(x_ref, g_ref, p_ref, w_ref, cw_ref, cb_ref, wr_ref, wi_ref, br_ref, bi_ref, sp_ref,
                   u_ref, a_ref, b_ref, gg_ref, ga_ref, gb_ref, xprev_ref, *, s5w, lruw):
    nb, tq, d = x_ref.shape
    rows = nb * tq
    halo = (CONV_WIDTH - 1) * nb

    @pl.when(pl.program_id(0) == 0)
    def _():
        xprev_ref[...] = jnp.zeros_like(xprev_ref)

    h = _rms(x_ref[...].reshape(rows, d), g_ref[...])
    hb = jnp.dot(p_ref[...], h.astype(BF16), preferred_element_type=F32).astype(BF16)

    def proj(lo, hi):
        return jnp.dot(hb, w_ref[:, lo:hi], preferred_element_type=F32)

    o1 = s5w
    o2 = o1 + lruw
    o3 = o2 + lruw
    o4 = o3 + d
    xin = proj(o1, o2)
    xext = jnp.concatenate([xprev_ref[...], xin], axis=0)
    xprev_ref[...] = xin[rows - halo:, :]
    xc_all = cb_ref[...]
    for k in range(CONV_WIDTH):
        xc_all = xc_all + xext[k * nb:k * nb + rows, :] * cw_ref[k:k + 1, :]

    def lru_gates(j):
        cs = slice(MXU_DIM * j, MXU_DIM * (j + 1))
        xc = xc_all[:, cs]
        xcb = xc.astype(BF16)
        r = _sigmoid(jnp.dot(xcb, wr_ref[j], preferred_element_type=F32) + br_ref[:, cs])
        ig = _sigmoid(jnp.dot(xcb, wi_ref[j], preferred_element_type=F32) + bi_ref[:, cs])
        log_a = (-LRU_C) * r * sp_ref[:, cs]
        th = jnp.tanh(log_a)
        mult = jnp.sqrt((-2.0 * th) / (1.0 - th))
        a_ref[:, cs] = jnp.exp(log_a)
        b_ref[:, cs] = mult * (ig * xc)

    others = [lambda: u_ref.__setitem__(Ellipsis, proj(0, o1)),
              lambda: gg_ref.__setitem__(Ellipsis, _gelu(proj(o2, o3)).astype(BF16)),
              lambda: ga_ref.__setitem__(Ellipsis, _sigmoid(proj(o3, o4)).astype(BF16)),
              lambda: gb_ref.__setitem__(Ellipsis, _sigmoid(proj(o4, o4 + d)).astype(BF16))]
    ntile = wr_ref.shape[0]
    for j in range(max(ntile, len(others))):
        if j < ntile:
            lru_gates(j)
        if j < len(others):
            others[j]()


def _inproj(x, g, perm, w_in_b, lrup, s5w, lruw):
    nb, seq, d = x.shape
    tq = TIME_CHUNK
    rows = nb * tq
    t = nb * seq
    row = lambda c: pl.BlockSpec((rows, c), lambda i: (i, 0))
    return pl.pallas_call(
        functools.partial(_inproj_kernel, s5w=s5w, lruw=lruw),
        grid=(seq // tq,),
        in_specs=[pl.BlockSpec((nb, tq, d), lambda i: (0, i, 0)), _full(g), _full(perm), _full(w_in_b)]
        + [_full(p) for p in lrup],
        out_specs=[row(s5w), row(lruw), row(lruw), row(lruw), row(d), row(d)],
        out_shape=[jax.ShapeDtypeStruct((t, s5w), F32), jax.ShapeDtypeStruct((t, lruw), F32),
                   jax.ShapeDtypeStruct((t, lruw), F32), jax.ShapeDtypeStruct((t, lruw), BF16),
                   jax.ShapeDtypeStruct((t, d), BF16), jax.ShapeDtypeStruct((t, d), BF16)],
        scratch_shapes=[pltpu.VMEM(((CONV_WIDTH - 1) * nb, lruw), F32)],
        compiler_params=_params(),
        name="inproj",
    )(x, g, perm, w_in_b, *lrup)


def _mixers_kernel(u_ref, a_ref, b_ref, gg_ref, ga_ref, gb_ref,
                   bm_ref, lr_ref, li_ref, cm_ref, dk_ref, wo_ref, wlo_ref,
                   o_ref, state_ref, sbuf_ref, h_ref, hbuf_ref, *, tq, nb):
    d = o_ref.shape[-1]
    npair = bm_ref.shape[0]
    kper = LANES // (2 * S5_GROUP)

    @pl.when(pl.program_id(0) == 0)
    def _():
        state_ref[...] = jnp.zeros_like(state_ref)
        h_ref[...] = jnp.zeros_like(h_ref)

    u2 = u_ref[...]
    ub = u2.astype(BF16)
    for k in range(npair):
        kb = k // kper
        bu = jnp.dot(ub[:, LANES * kb:LANES * (kb + 1)], bm_ref[k], preferred_element_type=F32)
        sbuf_ref[2 * k] = bu[:, :LANES]
        sbuf_ref[2 * k + 1] = bu[:, LANES:]

    pairs_per_loop = 4
    for kk in range(npair // pairs_per_loop):
        k0 = kk * pairs_per_loop
        lr = [jnp.broadcast_to(lr_ref[k0 + j:k0 + j + 1, :], (nb, LANES)) for j in range(pairs_per_loop)]
        li = [jnp.broadcast_to(li_ref[k0 + j:k0 + j + 1, :], (nb, LANES)) for j in range(pairs_per_loop)]
        init = tuple(state_ref[2 * k0 + j] for j in range(2 * pairs_per_loop))

        def body(it, carry, k0=k0, lr=lr, li=li):
            carry = list(carry)
            for s in range(SCAN_UNROLL):
                rows = _tile(it * SCAN_UNROLL + s)
                for j in range(pairs_per_loop):
                    sre, sim = carry[2 * j], carry[2 * j + 1]
                    nre = lr[j] * sre - li[j] * sim + sbuf_ref[2 * (k0 + j), rows, :]
                    nim = lr[j] * sim + li[j] * sre + sbuf_ref[2 * (k0 + j) + 1, rows, :]
                    sbuf_ref[2 * (k0 + j), rows, :] = nre
                    sbuf_ref[2 * (k0 + j) + 1, rows, :] = nim
                    carry[2 * j], carry[2 * j + 1] = nre, nim
            return tuple(carry)

        fin = lax.fori_loop(0, tq // SCAN_UNROLL, body, init)
        for j in range(2 * pairs_per_loop):
            state_ref[2 * k0 + j] = fin[j]

    def scan_body(it, h):
        for s in range(SCAN_UNROLL):
            tr = _tile(it * SCAN_UNROLL + s)
            h = a_ref[tr, :] * h + b_ref[tr, :]
            hbuf_ref[tr, :] = h
        return h

    h_ref[...] = lax.fori_loop(0, tq // SCAN_UNROLL, scan_body, h_ref[...])

    ys = []
    for kb in range(npair // kper):
        acc = None
        for j in range(kper):
            k = kb * kper + j
            s = jnp.concatenate([sbuf_ref[2 * k], sbuf_ref[2 * k + 1]], axis=1).astype(BF16)
            p = jnp.dot(s, cm_ref[k], preferred_element_type=F32)
            acc = p if acc is None else acc + p
        ys.append(acc)
    y = jnp.concatenate(ys, axis=1) + dk_ref[...] * u2
    z = jnp.dot(_gelu(y).astype(BF16), wo_ref[...], preferred_element_type=F32)
    ya = z[:, :d] * _sigmoid(z[:, d:])

    yl = hbuf_ref[...] * gg_ref[...].astype(F32)
    yb = jnp.dot(yl.astype(BF16), wlo_ref[...], preferred_element_type=F32)
    o_ref[...] = (ya * ga_ref[...].astype(F32) + yb * gb_ref[...].astype(F32)).astype(BF16)


def _mixers(u, a, b, gg, ga, gb, s5p, w_lru_out_b, nb):
    t, s5w = u.shape
    w = a.shape[-1]
    d = ga.shape[-1]
    tq = TIME_CHUNK
    rows = nb * tq
    nstate = s5p[0].shape[0] * MXU_DIM
    row = lambda c: pl.BlockSpec((rows, c), lambda i: (i, 0))
    params = tuple(s5p) + (w_lru_out_b,)
    return pl.pallas_call(
        functools.partial(_mixers_kernel, tq=tq, nb=nb),
        grid=(t // rows,),
        in_specs=[row(s5w), row(w), row(w), row(w), row(d), row(d)] + [_full(p) for p in params],
        out_specs=row(d),
        out_shape=jax.ShapeDtypeStruct((t, d), BF16),
        scratch_shapes=[pltpu.VMEM((nstate // LANES, nb, LANES), F32),
                        pltpu.VMEM((nstate // LANES, rows, LANES), F32),
                        pltpu.VMEM((nb, w), F32),
                        pltpu.VMEM((rows, w), F32)],
        compiler_params=_params(),
        name="mixers",
    )(u, a, b, gg, ga, gb, *params)


def _mixroute_kernel(x_ref, m_ref, pt_ref, wo_ref, g_ref, wh_ref, wl_ref, rb_ref,
                     x1_ref, h2_ref, eid_ref, gate_ref, cnt_ref):
    nb, tq, d = x_ref.shape
    tm = nb * tq
    m = jnp.dot(pt_ref[...], m_ref[...], preferred_element_type=F32).astype(BF16)
    x1 = x_ref[...].reshape(tm, d) + jnp.dot(m, wo_ref[...], preferred_element_type=F32)
    x1_ref[...] = x1
    h = _rms(x1, g_ref[...])

    hh = h.astype(BF16)
    h2_ref[...] = hh
    hl = (h - hh.astype(F32)).astype(BF16)
    dn = (((1,), (1,)), ((), ()))
    lt = (lax.dot_general(wh_ref[...], hh, dn, preferred_element_type=F32)
          + lax.dot_general(wh_ref[...], hl, dn, preferred_element_type=F32)
          + lax.dot_general(wl_ref[...], hh, dn, preferred_element_type=F32))
    lt = lt + rb_ref[:, 0:1]

    gl = lt[0:N_GROUPS, :]
    gmax = jnp.max(gl, axis=0, keepdims=True)
    gsum = jnp.sum(jnp.exp(gl - gmax), axis=0, keepdims=True)
    p_g = 1.0 / gsum
    iota_g = lax.broadcasted_iota(jnp.int32, gl.shape, 0)
    g_idx = jnp.min(jnp.where(gl == gmax, iota_g, N_GROUPS), axis=0, keepdims=True)

    el = lt[SUBLANES:SUBLANES + EXPERTS_PER_GROUP, :]
    for g in range(1, N_GROUPS):
        lo = SUBLANES + EXPERTS_PER_GROUP * g
        el = jnp.where(g_idx == g, lt[lo:lo + EXPERTS_PER_GROUP, :], el)
    emax = jnp.max(el, axis=0, keepdims=True)
    esum = jnp.sum(jnp.exp(el - emax), axis=0, keepdims=True)
    iota_e = lax.broadcasted_iota(jnp.int32, el.shape, 0)
    i1 = jnp.min(jnp.where(el == emax, iota_e, EXPERTS_PER_GROUP), axis=0, keepdims=True)
    el2 = jnp.where(iota_e == i1, -jnp.inf, el)
    emax2 = jnp.max(el2, axis=0, keepdims=True)
    i2 = jnp.min(jnp.where(el2 == emax2, iota_e, EXPERTS_PER_GROUP), axis=0, keepdims=True)
    p1 = 1.0 / esum
    p2 = jnp.exp(emax2 - emax) / esum
    psum = p1 + p2
    g1 = p_g * (p1 / psum)
    g2 = p_g * (p2 / psum)
    e1 = g_idx * EXPERTS_PER_GROUP + i1
    e2 = g_idx * EXPERTS_PER_GROUP + i2
    eid_ref[...] = jnp.concatenate([e1, e2, jnp.zeros((SUBLANES - 2, tm), jnp.int32)], axis=0)
    gate_ref[...] = jnp.concatenate([g1, g2, jnp.zeros((SUBLANES - 2, tm), F32)], axis=0)

    iota = lax.broadcasted_iota(jnp.int32, (LANES, tm), 0)
    hits = jnp.where((iota == e1) | (iota == e2), 1.0, 0.0).astype(BF16)
    cnt_ref[...] = lax.dot_general(jnp.ones((SUBLANES, tm), BF16), hits, dn, preferred_element_type=F32)


def _mixroute(x, m, perm_t, w_o_b, g, wr_hi, wr_lo, rbias):
    nb, seq, d = x.shape
    tq = TIME_CHUNK
    tm = nb * tq
    t = nb * seq
    row = lambda c: pl.BlockSpec((tm, c), lambda i: (i, 0))
    col = pl.BlockSpec((SUBLANES, tm), lambda i: (0, i))
    return pl.pallas_call(
        _mixroute_kernel,
        grid=(seq // tq,),
        in_specs=[pl.BlockSpec((nb, tq, d), lambda i: (0, i, 0)), row(d), _full(perm_t), _full(w_o_b),
                  _full(g), _full(wr_hi), _full(wr_lo), _full(rbias)],
        out_specs=[row(d), row(d), col, col, pl.BlockSpec((SUBLANES, LANES), lambda i: (i, 0))],
        out_shape=[jax.ShapeDtypeStruct((t, d), F32), jax.ShapeDtypeStruct((t, d), BF16),
                   jax.ShapeDtypeStruct((SUBLANES, t), jnp.int32), jax.ShapeDtypeStruct((SUBLANES, t), F32),
                   jax.ShapeDtypeStruct((seq // tq * SUBLANES, LANES), F32)],
        compiler_params=_params(),
        name="mixroute",
    )(x, m, perm_t, w_o_b, g, wr_hi, wr_lo, rbias)


def _tokens(ref, tok, n):
    return ref.at[pl.ds(pl.multiple_of(tok * SUBLANES, SUBLANES), n * SUBLANES)]


def _strip_copies(n, src, src_tok, dst, dst_tok, sem, max_tokens, wait=False):
    def piece(bit):
        done = n & (-2 * bit)

        @pl.when((n & bit) != 0)
        def _():
            cp = pltpu.make_async_copy(_tokens(src, src_tok + done, bit), _tokens(dst, dst_tok + done, bit), sem)
            if wait:
                cp.wait()
            else:
                cp.start()

    bits = [1 << b for b in reversed(range(max_tokens.bit_length()))]
    rare = [b for b in bits if b >= RARE_STRIP]
    if rare:
        @pl.when(n >= RARE_STRIP)
        def _():
            for b in rare:
                piece(b)
    for b in bits:
        if b < RARE_STRIP:
            piece(b)


def _tile_positions(eid_ref, offcol_ref, tri_ref):
    tm = eid_ref.shape[1]
    e0 = eid_ref[0:1, :]
    e1 = eid_ref[1:2, :]
    iota = lax.broadcasted_iota(jnp.int32, (N_EXPERTS, tm), 0)
    oh0 = iota == e0
    oh1 = iota == e1
    c = jnp.where(oh0 | oh1, 1.0, 0.0)
    before = jnp.dot(c.astype(BF16), tri_ref[...], preferred_element_type=F32) + offcol_ref[:, 0:1]
    p0 = jnp.sum(jnp.where(oh0, before, 0.0), axis=0, keepdims=True)
    p1 = jnp.sum(jnp.where(oh1, before, 0.0), axis=0, keepdims=True)
    return p0, p1


def _dispatch_kernel(cnt_ref, off_ref, gst_ref, tot_ref, pst_ref, pen_ref,
                     h2_ref, eid_ref, offcol_ref, tri_ref, xs_ref, pos_ref,
                     stage0_ref, stage1_ref, zero_ref, sem, zsem):
    i = pl.program_id(0)
    last = pl.num_programs(0) - 1
    tm = eid_ref.shape[1]
    na = 2 * tm
    d = h2_ref.shape[1]

    p0, p1 = _tile_positions(eid_ref, offcol_ref, tri_ref)
    pos_ref[...] = jnp.concatenate([p0, p1, jnp.zeros((SUBLANES - 2, tm), F32)], axis=0)
    p0i = p0.astype(jnp.int32)
    p1i = p1.astype(jnp.int32)
    rid = lax.broadcasted_iota(jnp.int32, (na, tm), 0)
    onehot = jnp.where((rid == p0i) | (rid == p1i), 1.0, 0.0).astype(BF16)
    srt = jnp.dot(onehot, h2_ref[...], preferred_element_type=F32)

    def step(stage, s):
        @pl.when(i >= 2)
        def _():
            pltpu.make_async_copy(stage, _tokens(xs_ref, 0, na), sem.at[s]).wait()

        for j in range(d // LANES):
            stage[pl.ds(j, na, stride=d // LANES), :] = srt[:, LANES * j:LANES * (j + 1)]

        def per_expert(e, c):
            k = i * N_EXPERTS + e
            _strip_copies(cnt_ref[k], stage, off_ref[k], xs_ref, gst_ref[k], sem.at[s], na)
            return c

        lax.fori_loop(0, N_EXPERTS, per_expert, 0)

    @pl.when(i % 2 == 0)
    def _():
        step(stage0_ref, 0)

    @pl.when(i % 2 == 1)
    def _():
        step(stage1_ref, 1)

    @pl.when(i == last)
    def _():
        zero_ref[...] = jnp.zeros_like(zero_ref)
        for wait in (False, True):
            for e in range(N_EXPERTS):
                _strip_copies(pen_ref[e] - pst_ref[e] - tot_ref[e], zero_ref, 0, xs_ref,
                              pst_ref[e] + tot_ref[e], zsem, SLOT_BLOCK - 1, wait=wait)

        nblk = xs_ref.shape[0] // zero_ref.shape[0]

        def bcopy(b):
            return pltpu.make_async_copy(zero_ref, _tokens(xs_ref, b * SLOT_BLOCK, SLOT_BLOCK), zsem)

        first_unused = pen_ref[N_EXPERTS - 1] // SLOT_BLOCK
        lax.fori_loop(first_unused, nblk, lambda b, c: (bcopy(b).start(), c)[1], 0)
        lax.fori_loop(first_unused, nblk, lambda b, c: (bcopy(b).wait(), c)[1], 0)

        @pl.when(i >= 1)
        def _():
            @pl.when(i % 2 == 0)
            def _():
                pltpu.make_async_copy(stage1_ref, _tokens(xs_ref, 0, na), sem.at[1]).wait()

            @pl.when(i % 2 == 1)
            def _():
                pltpu.make_async_copy(stage0_ref, _tokens(xs_ref, 0, na), sem.at[0]).wait()

        @pl.when(i % 2 == 0)
        def _():
            pltpu.make_async_copy(stage0_ref, _tokens(xs_ref, 0, na), sem.at[0]).wait()

        @pl.when(i % 2 == 1)
        def _():
            pltpu.make_async_copy(stage1_ref, _tokens(xs_ref, 0, na), sem.at[1]).wait()


def _dispatch(cnt_t, off_t, gst_t, tot, pst, pen, h2, eid, offcol, tri, n_slots):
    sub = SUBLANES
    t, d = h2.shape
    tm = tri.shape[0]
    grid_spec = pltpu.PrefetchScalarGridSpec(
        num_scalar_prefetch=6,
        grid=(t // tm,),
        in_specs=[pl.BlockSpec((tm, d), lambda i, *_: (i, 0)),
                  pl.BlockSpec((sub, tm), lambda i, *_: (0, i)),
                  pl.BlockSpec((None, N_EXPERTS, LANES), lambda i, *_: (i, 0, 0)),
                  _full(tri)],
        out_specs=[pl.BlockSpec(memory_space=pl.ANY), pl.BlockSpec((sub, tm), lambda i, *_: (0, i))],
        scratch_shapes=[pltpu.VMEM((2 * tm * sub, LANES), F32), pltpu.VMEM((2 * tm * sub, LANES), F32),
                        pltpu.VMEM((SLOT_BLOCK * sub, LANES), F32), pltpu.SemaphoreType.DMA((2,)),
                        pltpu.SemaphoreType.DMA(())],
    )
    return pl.pallas_call(
        _dispatch_kernel,
        grid_spec=grid_spec,
        out_shape=[jax.ShapeDtypeStruct((n_slots * sub, LANES), F32), jax.ShapeDtypeStruct((sub, t), F32)],
        compiler_params=_params(),
        name="dispatch",
    )(cnt_t, off_t, gst_t, tot, pst, pen, h2, eid, offcol, tri)


def _ffn_kernel(be_ref, nu_ref, xs_ref, wg_ref, wu_ref, wd_ref, ys_ref, wgb_ref, wub_ref, wdb_ref, *, blk):
    i = pl.program_id(0)
    sub = SUBLANES

    @pl.when(i < nu_ref[0])
    def _():
        prev = be_ref[jnp.maximum(i - 1, 0)]

        @pl.when((i == 0) | (be_ref[i] != prev))
        def _():
            wgb_ref[...] = wg_ref[...].astype(BF16)
            wub_ref[...] = wu_ref[...].astype(BF16)
            wdb_ref[...] = wd_ref[...].astype(BF16)

        x = jnp.concatenate([xs_ref[pl.ds(j, blk, stride=sub), :] for j in range(sub)], axis=1).astype(BF16)
        g = jnp.dot(x, wgb_ref[...], preferred_element_type=F32)
        u = jnp.dot(x, wub_ref[...], preferred_element_type=F32)
        a = (g * _sigmoid(g)) * u
        y = jnp.dot(a.astype(BF16), wdb_ref[...], preferred_element_type=F32)
        for j in range(sub):
            ys_ref[pl.ds(j, blk, stride=sub), :] = y[:, LANES * j:LANES * (j + 1)]

    @pl.when(i >= nu_ref[0])
    def _():
        ys_ref[...] = jnp.zeros_like(ys_ref)


def _ffn(block_e, n_used, xs, wg, wu, wd):
    sub = SUBLANES
    blk = SLOT_BLOCK
    n_blocks = xs.shape[0] // (sub * blk)
    ne, d, de = wg.shape

    def slot_map(i, be, nu):
        return (i, 0)

    def w_map(i, be, nu):
        return (be[jnp.minimum(i, nu[0] - 1)], 0, 0)

    grid_spec = pltpu.PrefetchScalarGridSpec(
        num_scalar_prefetch=2,
        grid=(n_blocks,),
        in_specs=[pl.BlockSpec((blk * sub, LANES), slot_map),
                  pl.BlockSpec((None, d, de), w_map), pl.BlockSpec((None, d, de), w_map),
                  pl.BlockSpec((None, de, d), w_map)],
        out_specs=pl.BlockSpec((blk * sub, LANES), slot_map),
        scratch_shapes=[pltpu.VMEM((d, de), BF16), pltpu.VMEM((d, de), BF16), pltpu.VMEM((de, d), BF16)],
    )
    return pl.pallas_call(
        functools.partial(_ffn_kernel, blk=blk),
        grid_spec=grid_spec,
        out_shape=jax.ShapeDtypeStruct(xs.shape, F32),
        compiler_params=_params(),
        name="ffn",
    )(block_e, n_used, xs, wg, wu, wd)


def _combine_kernel(cnt_ref, off_ref, gst_ref, x1_ref, pos_ref, gate_ref, ys_ref, g_ref, o_ref,
                    buf0_ref, buf1_ref, sem):
    i = pl.program_id(0)
    n = pl.num_programs(0)
    tm, d = x1_ref.shape
    na = 2 * tm
    sub = SUBLANES

    def fetch(tile, buf, s):
        def per_expert(e, c):
            k = tile * N_EXPERTS + e
            _strip_copies(cnt_ref[k], ys_ref, gst_ref[k], buf, off_ref[k], sem.at[s], na)
            return c

        lax.fori_loop(0, N_EXPERTS, per_expert, 0)

    @pl.when(i == 0)
    def _():
        fetch(0, buf0_ref, 0)

    @pl.when((i + 1 < n) & (i % 2 == 0))
    def _():
        fetch(i + 1, buf1_ref, 1)

    @pl.when((i + 1 < n) & (i % 2 == 1))
    def _():
        fetch(i + 1, buf0_ref, 0)

    rows4 = jnp.concatenate([pos_ref[0:2, :], gate_ref[0:2, :], jnp.zeros((sub - 4, tm), F32)], axis=0)
    cols = jnp.concatenate([rows4] * (LANES // sub), axis=0).T
    lane = lax.broadcasted_iota(jnp.int32, (tm, na), 1)
    pick0 = jnp.where(lane == cols[:, 0:1].astype(jnp.int32), 1.0, 0.0).astype(BF16)
    pick1 = jnp.where(lane == cols[:, 1:2].astype(jnp.int32), 1.0, 0.0).astype(BF16)

    def finish(buf, s):
        pltpu.make_async_copy(_tokens(ys_ref, 0, na), buf, sem.at[s]).wait()
        y = jnp.concatenate([buf[pl.ds(j, na, stride=sub), :] for j in range(d // LANES)], axis=1).astype(BF16)
        y0 = jnp.dot(pick0, y, preferred_element_type=F32)
        y1 = jnp.dot(pick1, y, preferred_element_type=F32)
        x = x1_ref[...] + (cols[:, 2:3] * y0 + cols[:, 3:4] * y1)
        o_ref[...] = _rms(x, g_ref[...]).reshape(o_ref.shape)

    @pl.when(i % 2 == 0)
    def _():
        finish(buf0_ref, 0)

    @pl.when(i % 2 == 1)
    def _():
        finish(buf1_ref, 1)


def _combine(cnt_t, off_t, gst_t, x1, pos, gate, ys, g, nb, seq):
    t, d = x1.shape
    tq = TIME_CHUNK
    tm = nb * tq
    sub = SUBLANES
    grid_spec = pltpu.PrefetchScalarGridSpec(
        num_scalar_prefetch=3,
        grid=(t // tm,),
        in_specs=[pl.BlockSpec((tm, d), lambda i, *_: (i, 0)),
                  pl.BlockSpec((sub, tm), lambda i, *_: (0, i)),
                  pl.BlockSpec((sub, tm), lambda i, *_: (0, i)),
                  pl.BlockSpec(memory_space=pl.ANY),
                  _full(g)],
        out_specs=pl.BlockSpec((nb, tq, d), lambda i, *_: (0, i, 0)),
        scratch_shapes=[pltpu.VMEM((2 * tm * sub, LANES), F32), pltpu.VMEM((2 * tm * sub, LANES), F32),
                        pltpu.SemaphoreType.DMA((2,))],
    )
    return pl.pallas_call(
        _combine_kernel,
        grid_spec=grid_spec,
        out_shape=jax.ShapeDtypeStruct((nb, seq, d), F32),
        compiler_params=_params(),
        name="combine",
    )(cnt_t, off_t, gst_t, x1, pos, gate, ys, g)


def _s5_tables(lam_re, lam_im, log_dt, b_re, b_im, c_re, c_im):
    ng, p = lam_re.shape
    npair = ng // 2
    kper = LANES // (2 * S5_GROUP)
    lam = lax.complex(lam_re, lam_im)
    dt = jnp.exp(log_dt)[:, None]
    lam_bar = jnp.exp(lam * dt)
    b_bar = ((lam_bar - 1.0) / lam)[..., None] * lax.complex(b_re, b_im)
    lr = jnp.real(lam_bar).reshape(npair, 2 * p)
    li = jnp.imag(lam_bar).reshape(npair, 2 * p)
    eye2 = jnp.eye(2, dtype=F32)
    sel = jax.nn.one_hot(jnp.arange(npair) % kper, kper, dtype=F32)

    bv = jnp.stack([jnp.real(b_bar), jnp.imag(b_bar)])
    bv = bv.reshape(2, npair, 2, p, S5_GROUP).transpose(1, 2, 4, 0, 3)
    bblk = bv[:, :, :, :, None, :] * eye2[None, :, None, None, :, None]
    bblk = bblk.reshape(npair, 2 * S5_GROUP, 4 * p)
    bm = (sel[:, :, None, None] * bblk[:, None]).reshape(npair, LANES, 4 * p)

    cv = jnp.stack([c_re, -c_im])
    cv = cv.reshape(2, npair, 2, S5_GROUP, p).transpose(1, 0, 2, 4, 3)
    cblk = cv[:, :, :, :, None, :] * eye2[None, None, :, None, :, None]
    cblk = cblk.reshape(npair, 4 * p, 2 * S5_GROUP)
    cm = (cblk[:, :, None, :] * sel[:, None, :, None]).reshape(npair, 4 * p, LANES)
    return bm.astype(BF16), lr, li, cm.astype(BF16)


def _blockdiag_tiles(w):
    nh, hi, ho = w.shape
    per = MXU_DIM // hi
    eye = jnp.eye(per, dtype=w.dtype)
    t = w.reshape(nh // per, per, hi, ho)[:, :, :, None, :] * eye[None, :, None, :, None]
    return t.reshape(nh // per, per * hi, per * ho)


def kernel(x, norm_mix, w_in, s5_lam_re, s5_lam_im, s5_log_dt, s5_b_re, s5_b_im, s5_c_re, s5_c_im, s5_d,
           w_s5_out, conv_w, conv_b, lru_w_r, lru_b_r, lru_w_i, lru_b_i, lru_lambda, w_lru_out, w_o, norm_ffn,
           router_group_w, router_group_b, router_expert_w, router_expert_b, expert_w_gate, expert_w_up,
           expert_w_down, norm_final):
    nb, seq, d = x.shape
    t = nb * seq
    assert w_in.shape[0] == 1, "one layer: the final RMSNorm is fused into the layer's combine kernel"
    l = 0
    s5w = s5_d.shape[-1]
    lruw = conv_b.shape[-1]
    rows = nb * TIME_CHUNK

    r = jnp.arange(rows)
    perm = jax.nn.one_hot((r % nb) * TIME_CHUNK + r // nb, rows, dtype=BF16)

    sp = jax.nn.softplus(-lru_lambda[l])[None, :]
    lrup = (conv_w[l], conv_b[l][None, :], _blockdiag_tiles(lru_w_r[l]).astype(BF16),
            _blockdiag_tiles(lru_w_i[l]).astype(BF16), lru_b_r[l][None, :], lru_b_i[l][None, :], sp)
    u, a, b, gg, ga, gb = _inproj(x, norm_mix[l][None, :], perm, w_in[l].astype(BF16), lrup, s5w, lruw)

    bm, lr, li, cm = _s5_tables(s5_lam_re[l], s5_lam_im[l], s5_log_dt[l], s5_b_re[l], s5_b_im[l],
                                s5_c_re[l], s5_c_im[l])
    s5p = (bm, lr, li, cm, s5_d[l][None, :], w_s5_out[l].astype(BF16))
    mixed = _mixers(u, a, b, gg, ga, gb, s5p, w_lru_out[l].astype(BF16), nb)

    gap = jnp.zeros((SUBLANES - N_GROUPS, d), F32)
    wrt = jnp.concatenate([router_group_w[l].T, gap, router_expert_w[l].T], axis=0)
    wr_hi = wrt.astype(BF16)
    wr_lo = (wrt - wr_hi.astype(F32)).astype(BF16)
    rb = jnp.concatenate([router_group_b[l], jnp.zeros((SUBLANES - N_GROUPS,), F32), router_expert_b[l]])
    rbias = jnp.broadcast_to(rb[:, None], (ROUTER_ROWS, LANES))

    x1, h2, eid, gate, tcnt = _mixroute(x, mixed, perm.T, w_o[l].astype(BF16), norm_ffn[l][None, :],
                                        wr_hi, wr_lo, rbias)

    n_tiles = seq // TIME_CHUNK
    cnt_t = tcnt.reshape(n_tiles, SUBLANES, LANES)[:, 0, :N_EXPERTS].astype(jnp.int32)
    off_t = jnp.cumsum(cnt_t, axis=1) - cnt_t
    counts = jnp.sum(cnt_t, axis=0)
    padded = ((counts + SLOT_BLOCK - 1) // SLOT_BLOCK) * SLOT_BLOCK
    pad_ends = jnp.cumsum(padded)
    pad_starts = pad_ends - padded
    gst_t = pad_starts[None, :] + jnp.cumsum(cnt_t, axis=0) - cnt_t
    offcol = jnp.broadcast_to(off_t.astype(F32)[:, :, None], (n_tiles, N_EXPERTS, LANES))
    n_blocks = -(-(2 * t) // SLOT_BLOCK) + N_EXPERTS
    n_slots = n_blocks * SLOT_BLOCK
    block_start = jnp.arange(n_blocks, dtype=jnp.int32) * SLOT_BLOCK
    block_e = jnp.minimum(jnp.sum((pad_ends[None, :] <= block_start[:, None]).astype(jnp.int32), axis=1),
                          N_EXPERTS - 1)
    n_used = pad_ends[-1:] // SLOT_BLOCK
    cnt_f, off_f, gst_f = cnt_t.reshape(-1), off_t.reshape(-1), gst_t.reshape(-1)

    tri = jnp.triu(jnp.ones((rows, rows), BF16), k=1)
    xs, pos = _dispatch(cnt_f, off_f, gst_f, counts, pad_starts, pad_ends, h2, eid, offcol, tri, n_slots)
    ys = _ffn(block_e, n_used, xs, expert_w_gate[l], expert_w_up[l], expert_w_down[l])
    return _combine(cnt_f, off_f, gst_f, x1, pos, gate, ys, norm_final[None, :], nb, seq)
```

```python
import functools
import math

import jax
import jax.numpy as jnp
from jax import lax
from jax.experimental import pallas as pl
from jax.experimental.pallas import tpu as pltpu

F32 = jnp.float32
BF16 = jnp.bfloat16

RMS_EPS = 1e-6
S5_GROUP = 16
CONV_WIDTH = 4
LRU_C = 8.0
N_GROUPS = 4
EXPERTS_PER_GROUP = 8
N_EXPERTS = N_GROUPS * EXPERTS_PER_GROUP

LANES = 128
SUBLANES = 8
MXU_DIM = 256
TIME_CHUNK = 64
SCAN_UNROLL = 4
SLOT_BLOCK = 512
RARE_STRIP = 64
ROUTER_ROWS = 40
VMEM_LIMIT = 56 * 1024 * 1024


def _gelu(x):
    c = math.sqrt(2.0 / math.pi)
    return 0.5 * x * (1.0 + jnp.tanh(c * (x + 0.044715 * (x * x * x))))


def _sigmoid(x):
    return 0.5 * jnp.tanh(0.5 * x) + 0.5


def _rms(x, g):
    return x * lax.rsqrt(jnp.mean(x * x, axis=-1, keepdims=True) + RMS_EPS) * g


def _params():
    return pltpu.CompilerParams(dimension_semantics=("arbitrary",), vmem_limit_bytes=VMEM_LIMIT)


def _full(a):
    return pl.BlockSpec(a.shape, lambda i, *_: (0,) * a.ndim)


def _tile(t):
    return pl.ds(pl.multiple_of(t * SUBLANES, SUBLANES), SUBLANES)


def _inproj_kernel(x_ref, g_ref, p_ref, w_ref, cw_ref, cb_ref, wr_ref, wi_ref, br_ref, bi_ref, sp_ref,
                   u_ref, a_ref, b_ref, gg_ref, ga_ref, gb_ref, xprev_ref, *, s5w, lruw):
    nb, tq, d = x_ref.shape
    rows = nb * tq
    halo = (CONV_WIDTH - 1) * nb

    @pl.when(pl.program_id(0) == 0)
    def _():
        xprev_ref[...] = jnp.zeros_like(xprev_ref)

    h = _rms(x_ref[...].reshape(rows, d), g_ref[...])
    hb = jnp.dot(p_ref[...], h.astype(BF16), preferred_element_type=F32).astype(BF16)

    def proj(lo, hi):
        return jnp.dot(hb, w_ref[:, lo:hi], preferred_element_type=F32)

    o1 = s5w
    o2 = o1 + lruw
    o3 = o2 + lruw
    o4 = o3 + d
    xin = proj(o1, o2)
    xext = jnp.concatenate([xprev_ref[...], xin], axis=0)
    xprev_ref[...] = xin[rows - halo:, :]
    xc_all = cb_ref[...]
    for k in range(CONV_WIDTH):
        xc_all = xc_all + xext[k * nb:k * nb + rows, :] * cw_ref[k:k + 1, :]

    def lru_gates(j):
        cs = slice(MXU_DIM * j, MXU_DIM * (j + 1))
        xc = xc_all[:, cs]
        xcb = xc.astype(BF16)
        r = _sigmoid(jnp.dot(xcb, wr_ref[j], preferred_element_type=F32) + br_ref[:, cs])
        ig = _sigmoid(jnp.dot(xcb, wi_ref[j], preferred_element_type=F32) + bi_ref[:, cs])
        log_a = (-LRU_C) * r * sp_ref[:, cs]
        th = jnp.tanh(log_a)
        mult = jnp.sqrt((-2.0 * th) / (1.0 - th))
        a_ref[:, cs] = jnp.exp(log_a)
        b_ref[:, cs] = mult * (ig * xc)

    others = [lambda: u_ref.__setitem__(Ellipsis, proj(0, o1)),
              lambda: gg_ref.__setitem__(Ellipsis, _gelu(proj(o2, o3)).astype(BF16)),
              lambda: ga_ref.__setitem__(Ellipsis, _sigmoid(proj(o3, o4)).astype(BF16)),
              lambda: gb_ref.__setitem__(Ellipsis, _sigmoid(proj(o4, o4 + d)).astype(BF16))]
    ntile = wr_ref.shape[0]
    for j in range(max(ntile, len(others))):
        if j < ntile:
            lru_gates(j)
        if j < len(others):
            others[j]()


def _inproj(x, g, perm, w_in_b, lrup, s5w, lruw):
    nb, seq, d = x.shape
    tq = TIME_CHUNK
    rows = nb * tq
    t = nb * seq
    row = lambda c: pl.BlockSpec((rows, c), lambda i: (i, 0))
    return pl.pallas_call(
        functools.partial(_inproj_kernel, s5w=s5w, lruw=lruw),
        grid=(seq // tq,),
        in_specs=[pl.BlockSpec((nb, tq, d), lambda i: (0, i, 0)), _full(g), _full(perm), _full(w_in_b)]
        + [_full(p) for p in lrup],
        out_specs=[row(s5w), row(lruw), row(lruw), row(lruw), row(d), row(d)],
        out_shape=[jax.ShapeDtypeStruct((t, s5w), F32), jax.ShapeDtypeStruct((t, lruw), F32),
                   jax.ShapeDtypeStruct((t, lruw), F32), jax.ShapeDtypeStruct((t, lruw), BF16),
                   jax.ShapeDtypeStruct((t, d), BF16), jax.ShapeDtypeStruct((t, d), BF16)],
        scratch_shapes=[pltpu.VMEM(((CONV_WIDTH - 1) * nb, lruw), F32)],
        compiler_params=_params(),
        name="inproj",
    )(x, g, perm, w_in_b, *lrup)


def _mixers_kernel(u_ref, a_ref, b_ref, gg_ref, ga_ref, gb_ref,
                   bm_ref, lr_ref, li_ref, cm_ref, dk_ref, wo_ref, wlo_ref,
                   o_ref, state_ref, sbuf_ref, h_ref, hbuf_ref, *, tq, nb):
    d = o_ref.shape[-1]
    npair = bm_ref.shape[0]
    kper = LANES // (2 * S5_GROUP)

    @pl.when(pl.program_id(0) == 0)
    def _():
        state_ref[...] = jnp.zeros_like(state_ref)
        h_ref[...] = jnp.zeros_like(h_ref)

    u2 = u_ref[...]
    ub = u2.astype(BF16)
    for k in range(npair):
        kb = k // kper
        bu = jnp.dot(ub[:, LANES * kb:LANES * (kb + 1)], bm_ref[k], preferred_element_type=F32)
        sbuf_ref[2 * k] = bu[:, :LANES]
        sbuf_ref[2 * k + 1] = bu[:, LANES:]

    pairs_per_loop = 4
    for kk in range(npair // pairs_per_loop):
        k0 = kk * pairs_per_loop
        lr = [jnp.broadcast_to(lr_ref[k0 + j:k0 + j + 1, :], (nb, LANES)) for j in range(pairs_per_loop)]
        li = [jnp.broadcast_to(li_ref[k0 + j:k0 + j + 1, :], (nb, LANES)) for j in range(pairs_per_loop)]
        init = tuple(state_ref[2 * k0 + j] for j in range(2 * pairs_per_loop))

        def body(it, carry, k0=k0, lr=lr, li=li):
            carry = list(carry)
            for s in range(SCAN_UNROLL):
                rows = _tile(it * SCAN_UNROLL + s)
                for j in range(pairs_per_loop):
                    sre, sim = carry[2 * j], carry[2 * j + 1]
                    nre = lr[j] * sre - li[j] * sim + sbuf_ref[2 * (k0 + j), rows, :]
                    nim = lr[j] * sim + li[j] * sre + sbuf_ref[2 * (k0 + j) + 1, rows, :]
                    sbuf_ref[2 * (k0 + j), rows, :] = nre
                    sbuf_ref[2 * (k0 + j) + 1, rows, :] = nim
                    carry[2 * j], carry[2 * j + 1] = nre, nim
            return tuple(carry)

        fin = lax.fori_loop(0, tq // SCAN_UNROLL, body, init)
        for j in range(2 * pairs_per_loop):
            state_ref[2 * k0 + j] = fin[j]

    def scan_body(it, h):
        for s in range(SCAN_UNROLL):
            tr = _tile(it * SCAN_UNROLL + s)
            h = a_ref[tr, :] * h + b_ref[tr, :]
            hbuf_ref[tr, :] = h
        return h

    h_ref[...] = lax.fori_loop(0, tq // SCAN_UNROLL, scan_body, h_ref[...])

    ys = []
    for kb in range(npair // kper):
        acc = None
        for j in range(kper):
            k = kb * kper + j
            s = jnp.concatenate([sbuf_ref[2 * k], sbuf_ref[2 * k + 1]], axis=1).astype(BF16)
            p = jnp.dot(s, cm_ref[k], preferred_element_type=F32)
            acc = p if acc is None else acc + p
        ys.append(acc)
    y = jnp.concatenate(ys, axis=1) + dk_ref[...] * u2
    z = jnp.dot(_gelu(y).astype(BF16), wo_ref[...], preferred_element_type=F32)
    ya = z[:, :d] * _sigmoid(z[:, d:])

    yl = hbuf_ref[...] * gg_ref[...].astype(F32)
    yb = jnp.dot(yl.astype(BF16), wlo_ref[...], preferred_element_type=F32)
    o_ref[...] = (ya * ga_ref[...].astype(F32) + yb * gb_ref[...].astype(F32)).astype(BF16)


def _mixers(u, a, b, gg, ga, gb, s5p, w_lru_out_b, nb):
    t, s5w = u.shape
    w = a.shape[-1]
    d = ga.shape[-1]
    tq = TIME_CHUNK
    rows = nb * tq
    nstate = s5p[0].shape[0] * MXU_DIM
    row = lambda c: pl.BlockSpec((rows, c), lambda i: (i, 0))
    params = tuple(s5p) + (w_lru_out_b,)
    return pl.pallas_call(
        functools.partial(_mixers_kernel, tq=tq, nb=nb),
        grid=(t // rows,),
        in_specs=[row(s5w), row(w), row(w), row(w), row(d), row(d)] + [_full(p) for p in params],
        out_specs=row(d),
        out_shape=jax.ShapeDtypeStruct((t, d), BF16),
        scratch_shapes=[pltpu.VMEM((nstate // LANES, nb, LANES), F32),
                        pltpu.VMEM((nstate // LANES, rows, LANES), F32),
                        pltpu.VMEM((nb, w), F32),
                        pltpu.VMEM((rows, w), F32)],
        compiler_params=_params(),
        name="mixers",
    )(u, a, b, gg, ga, gb, *params)


def _mixroute_kernel(x_ref, m_ref, pt_ref, wo_ref, g_ref, wh_ref, wl_ref, rb_ref,
                     x1_ref, h2_ref, eid_ref, gate_ref, cnt_ref):
    nb, tq, d = x_ref.shape
    tm = nb * tq
    m = jnp.dot(pt_ref[...], m_ref[...], preferred_element_type=F32).astype(BF16)
    x1 = x_ref[...].reshape(tm, d) + jnp.dot(m, wo_ref[...], preferred_element_type=F32)
    x1_ref[...] = x1
    h = _rms(x1, g_ref[...])

    hh = h.astype(BF16)
    h2_ref[...] = hh
    hl = (h - hh.astype(F32)).astype(BF16)
    dn = (((1,), (1,)), ((), ()))
    lt = (lax.dot_general(wh_ref[...], hh, dn, preferred_element_type=F32)
          + lax.dot_general(wh_ref[...], hl, dn, preferred_element_type=F32)
          + lax.dot_general(wl_ref[...], hh, dn, preferred_element_type=F32))
    lt = lt + rb_ref[:, 0:1]

    gl = lt[0:N_GROUPS, :]
    gmax = jnp.max(gl, axis=0, keepdims=True)
    gsum = jnp.sum(jnp.exp(gl - gmax), axis=0, keepdims=True)
    p_g = 1.0 / gsum
    iota_g = lax.broadcasted_iota(jnp.int32, gl.shape, 0)
    g_idx = jnp.min(jnp.where(gl == gmax, iota_g, N_GROUPS), axis=0, keepdims=True)

    el = lt[SUBLANES:SUBLANES + EXPERTS_PER_GROUP, :]
    for g in range(1, N_GROUPS):
        lo = SUBLANES + EXPERTS_PER_GROUP * g
        el = jnp.where(g_idx == g, lt[lo:lo + EXPERTS_PER_GROUP, :], el)
    emax = jnp.max(el, axis=0, keepdims=True)
    esum = jnp.sum(jnp.exp(el - emax), axis=0, keepdims=True)
    iota_e = lax.broadcasted_iota(jnp.int32, el.shape, 0)
    i1 = jnp.min(jnp.where(el == emax, iota_e, EXPERTS_PER_GROUP), axis=0, keepdims=True)
    el2 = jnp.where(iota_e == i1, -jnp.inf, el)
    emax2 = jnp.max(el2, axis=0, keepdims=True)
    i2 = jnp.min(jnp.where(el2 == emax2, iota_e, EXPERTS_PER_GROUP), axis=0, keepdims=True)
    p1 = 1.0 / esum
    p2 = jnp.exp(emax2 - emax) / esum
    psum = p1 + p2
    g1 = p_g * (p1 / psum)
    g2 = p_g * (p2 / psum)
    e1 = g_idx * EXPERTS_PER_GROUP + i1
    e2 = g_idx * EXPERTS_PER_GROUP + i2
    eid_ref[...] = jnp.concatenate([e1, e2, jnp.zeros((SUBLANES - 2, tm), jnp.int32)], axis=0)
    gate_ref[...] = jnp.concatenate([g1, g2, jnp.zeros((SUBLANES - 2, tm), F32)], axis=0)

    iota = lax.broadcasted_iota(jnp.int32, (LANES, tm), 0)
    hits = jnp.where((iota == e1) | (iota == e2), 1.0, 0.0).astype(BF16)
    cnt_ref[...] = lax.dot_general(jnp.ones((SUBLANES, tm), BF16), hits, dn, preferred_element_type=F32)


def _mixroute(x, m, perm_t, w_o_b, g, wr_hi, wr_lo, rbias):
    nb, seq, d = x.shape
    tq = TIME_CHUNK
    tm = nb * tq
    t = nb * seq
    row = lambda c: pl.BlockSpec((tm, c), lambda i: (i, 0))
    col = pl.BlockSpec((SUBLANES, tm), lambda i: (0, i))
    return pl.pallas_call(
        _mixroute_kernel,
        grid=(seq // tq,),
        in_specs=[pl.BlockSpec((nb, tq, d), lambda i: (0, i, 0)), row(d), _full(perm_t), _full(w_o_b),
                  _full(g), _full(wr_hi), _full(wr_lo), _full(rbias)],
        out_specs=[row(d), row(d), col, col, pl.BlockSpec((SUBLANES, LANES), lambda i: (i, 0))],
        out_shape=[jax.ShapeDtypeStruct((t, d), F32), jax.ShapeDtypeStruct((t, d), BF16),
                   jax.ShapeDtypeStruct((SUBLANES, t), jnp.int32), jax.ShapeDtypeStruct((SUBLANES, t), F32),
                   jax.ShapeDtypeStruct((seq // tq * SUBLANES, LANES), F32)],
        compiler_params=_params(),
        name="mixroute",
    )(x, m, perm_t, w_o_b, g, wr_hi, wr_lo, rbias)


def _tokens(ref, tok, n):
    return ref.at[pl.ds(pl.multiple_of(tok * SUBLANES, SUBLANES), n * SUBLANES)]


def _strip_copies(n, src, src_tok, dst, dst_tok, sem, max_tokens, wait=False):
    del max_tokens

    @pl.when(n > 0)
    def _():
        cp = pltpu.make_async_copy(_tokens(src, src_tok, n), _tokens(dst, dst_tok, n), sem)
        if wait:
            cp.wait()
        else:
            cp.start()


def _tile_positions(eid_ref, offcol_ref, tri_ref):
    tm = eid_ref.shape[1]
    e0 = eid_ref[0:1, :]
    e1 = eid_ref[1:2, :]
    iota = lax.broadcasted_iota(jnp.int32, (N_EXPERTS, tm), 0)
    oh0 = iota == e0
    oh1 = iota == e1
    c = jnp.where(oh0 | oh1, 1.0, 0.0)
    before = jnp.dot(c.astype(BF16), tri_ref[...], preferred_element_type=F32) + offcol_ref[:, 0:1]
    p0 = jnp.sum(jnp.where(oh0, before, 0.0), axis=0, keepdims=True)
    p1 = jnp.sum(jnp.where(oh1, before, 0.0), axis=0, keepdims=True)
    return p0, p1


def _dispatch_kernel(cnt_ref, off_ref, gst_ref, tot_ref, pst_ref, pen_ref,
                     h2_ref, eid_ref, offcol_ref, tri_ref, xs_ref, pos_ref,
                     stage0_ref, stage1_ref, zero_ref, sem, zsem):
    i = pl.program_id(0)
    last = pl.num_programs(0) - 1
    tm = eid_ref.shape[1]
    na = 2 * tm
    d = h2_ref.shape[1]

    p0, p1 = _tile_positions(eid_ref, offcol_ref, tri_ref)
    pos_ref[...] = jnp.concatenate([p0, p1, jnp.zeros((SUBLANES - 2, tm), F32)], axis=0)
    p0i = p0.astype(jnp.int32)
    p1i = p1.astype(jnp.int32)
    rid = lax.broadcasted_iota(jnp.int32, (na, tm), 0)
    onehot = jnp.where((rid == p0i) | (rid == p1i), 1.0, 0.0).astype(BF16)
    srt = jnp.dot(onehot, h2_ref[...], preferred_element_type=F32)

    def step(stage, s):
        @pl.when(i >= 2)
        def _():
            pltpu.make_async_copy(stage, _tokens(xs_ref, 0, na), sem.at[s]).wait()

        for j in range(d // LANES):
            stage[pl.ds(j, na, stride=d // LANES), :] = srt[:, LANES * j:LANES * (j + 1)]

        def per_expert(e, c):
            k = i * N_EXPERTS + e
            _strip_copies(cnt_ref[k], stage, off_ref[k], xs_ref, gst_ref[k], sem.at[s], na)
            return c

        lax.fori_loop(0, N_EXPERTS, per_expert, 0)

    @pl.when(i % 2 == 0)
    def _():
        step(stage0_ref, 0)

    @pl.when(i % 2 == 1)
    def _():
        step(stage1_ref, 1)

    @pl.when(i == last)
    def _():
        zero_ref[...] = jnp.zeros_like(zero_ref)
        for wait in (False, True):
            for e in range(N_EXPERTS):
                _strip_copies(pen_ref[e] - pst_ref[e] - tot_ref[e], zero_ref, 0, xs_ref,
                              pst_ref[e] + tot_ref[e], zsem, SLOT_BLOCK - 1, wait=wait)

        nblk = xs_ref.shape[0] // zero_ref.shape[0]

        def bcopy(b):
            return pltpu.make_async_copy(zero_ref, _tokens(xs_ref, b * SLOT_BLOCK, SLOT_BLOCK), zsem)

        first_unused = pen_ref[N_EXPERTS - 1] // SLOT_BLOCK
        lax.fori_loop(first_unused, nblk, lambda b, c: (bcopy(b).start(), c)[1], 0)
        lax.fori_loop(first_unused, nblk, lambda b, c: (bcopy(b).wait(), c)[1], 0)

        @pl.when(i >= 1)
        def _():
            @pl.when(i % 2 == 0)
            def _():
                pltpu.make_async_copy(stage1_ref, _tokens(xs_ref, 0, na), sem.at[1]).wait()

            @pl.when(i % 2 == 1)
            def _():
                pltpu.make_async_copy(stage0_ref, _tokens(xs_ref, 0, na), sem.at[0]).wait()

        @pl.when(i % 2 == 0)
        def _():
            pltpu.make_async_copy(stage0_ref, _tokens(xs_ref, 0, na), sem.at[0]).wait()

        @pl.when(i % 2 == 1)
        def _():
            pltpu.make_async_copy(stage1_ref, _tokens(xs_ref, 0, na), sem.at[1]).wait()


def _dispatch(cnt_t, off_t, gst_t, tot, pst, pen, h2, eid, offcol, tri, n_slots):
    sub = SUBLANES
    t, d = h2.shape
    tm = tri.shape[0]
    grid_spec = pltpu.PrefetchScalarGridSpec(
        num_scalar_prefetch=6,
        grid=(t // tm,),
        in_specs=[pl.BlockSpec((tm, d), lambda i, *_: (i, 0)),
                  pl.BlockSpec((sub, tm), lambda i, *_: (0, i)),
                  pl.BlockSpec((None, N_EXPERTS, LANES), lambda i, *_: (i, 0, 0)),
                  _full(tri)],
        out_specs=[pl.BlockSpec(memory_space=pl.ANY), pl.BlockSpec((sub, tm), lambda i, *_: (0, i))],
        scratch_shapes=[pltpu.VMEM((2 * tm * sub, LANES), F32), pltpu.VMEM((2 * tm * sub, LANES), F32),
                        pltpu.VMEM((SLOT_BLOCK * sub, LANES), F32), pltpu.SemaphoreType.DMA((2,)),
                        pltpu.SemaphoreType.DMA(())],
    )
    return pl.pallas_call(
        _dispatch_kernel,
        grid_spec=grid_spec,
        out_shape=[jax.ShapeDtypeStruct((n_slots * sub, LANES), F32), jax.ShapeDtypeStruct((sub, t), F32)],
        compiler_params=_params(),
        name="dispatch",
    )(cnt_t, off_t, gst_t, tot, pst, pen, h2, eid, offcol, tri)


def _ffn_kernel(be_ref, nu_ref, xs_ref, wg_ref, wu_ref, wd_ref, ys_ref, wgb_ref, wub_ref, wdb_ref, *, blk):
    i = pl.program_id(0)
    sub = SUBLANES

    @pl.when(i < nu_ref[0])
    def _():
        prev = be_ref[jnp.maximum(i - 1, 0)]

        @pl.when((i == 0) | (be_ref[i] != prev))
        def _():
            wgb_ref[...] = wg_ref[...].astype(BF16)
            wub_ref[...] = wu_ref[...].astype(BF16)
            wdb_ref[...] = wd_ref[...].astype(BF16)

        x = jnp.concatenate([xs_ref[pl.ds(j, blk, stride=sub), :] for j in range(sub)], axis=1).astype(BF16)
        g = jnp.dot(x, wgb_ref[...], preferred_element_type=F32)
        u = jnp.dot(x, wub_ref[...], preferred_element_type=F32)
        a = (g * _sigmoid(g)) * u
        y = jnp.dot(a.astype(BF16), wdb_ref[...], preferred_element_type=F32)
        for j in range(sub):
            ys_ref[pl.ds(j, blk, stride=sub), :] = y[:, LANES * j:LANES * (j + 1)]

    @pl.when(i >= nu_ref[0])
    def _():
        ys_ref[...] = jnp.zeros_like(ys_ref)


def _ffn(block_e, n_used, xs, wg, wu, wd):
    sub = SUBLANES
    blk = SLOT_BLOCK
    n_blocks = xs.shape[0] // (sub * blk)
    ne, d, de = wg.shape

    def slot_map(i, be, nu):
        return (i, 0)

    def w_map(i, be, nu):
        return (be[jnp.minimum(i, nu[0] - 1)], 0, 0)

    grid_spec = pltpu.PrefetchScalarGridSpec(
        num_scalar_prefetch=2,
        grid=(n_blocks,),
        in_specs=[pl.BlockSpec((blk * sub, LANES), slot_map),
                  pl.BlockSpec((None, d, de), w_map), pl.BlockSpec((None, d, de), w_map),
                  pl.BlockSpec((None, de, d), w_map)],
        out_specs=pl.BlockSpec((blk * sub, LANES), slot_map),
        scratch_shapes=[pltpu.VMEM((d, de), BF16), pltpu.VMEM((d, de), BF16), pltpu.VMEM((de, d), BF16)],
    )
    return pl.pallas_call(
        functools.partial(_ffn_kernel, blk=blk),
        grid_spec=grid_spec,
        out_shape=jax.ShapeDtypeStruct(xs.shape, F32),
        compiler_params=_params(),
        name="ffn",
    )(block_e, n_used, xs, wg, wu, wd)


def _combine_kernel(cnt_ref, off_ref, gst_ref, x1_ref, pos_ref, gate_ref, ys_ref, g_ref, o_ref,
                    buf0_ref, buf1_ref, sem):
    i = pl.program_id(0)
    n = pl.num_programs(0)
    tm, d = x1_ref.shape
    na = 2 * tm
    sub = SUBLANES

    def fetch(tile, buf, s):
        def per_expert(e, c):
            k = tile * N_EXPERTS + e
            _strip_copies(cnt_ref[k], ys_ref, gst_ref[k], buf, off_ref[k], sem.at[s], na)
            return c

        lax.fori_loop(0, N_EXPERTS, per_expert, 0)

    @pl.when(i == 0)
    def _():
        fetch(0, buf0_ref, 0)

    @pl.when((i + 1 < n) & (i % 2 == 0))
    def _():
        fetch(i + 1, buf1_ref, 1)

    @pl.when((i + 1 < n) & (i % 2 == 1))
    def _():
        fetch(i + 1, buf0_ref, 0)

    rows4 = jnp.concatenate([pos_ref[0:2, :], gate_ref[0:2, :], jnp.zeros((sub - 4, tm), F32)], axis=0)
    cols = jnp.concatenate([rows4] * (LANES // sub), axis=0).T
    lane = lax.broadcasted_iota(jnp.int32, (tm, na), 1)
    pick0 = jnp.where(lane == cols[:, 0:1].astype(jnp.int32), 1.0, 0.0).astype(BF16)
    pick1 = jnp.where(lane == cols[:, 1:2].astype(jnp.int32), 1.0, 0.0).astype(BF16)

    def finish(buf, s):
        pltpu.make_async_copy(_tokens(ys_ref, 0, na), buf, sem.at[s]).wait()
        y = jnp.concatenate([buf[pl.ds(j, na, stride=sub), :] for j in range(d // LANES)], axis=1).astype(BF16)
        y0 = jnp.dot(pick0, y, preferred_element_type=F32)
        y1 = jnp.dot(pick1, y, preferred_element_type=F32)
        x = x1_ref[...] + (cols[:, 2:3] * y0 + cols[:, 3:4] * y1)
        o_ref[...] = _rms(x, g_ref[...]).reshape(o_ref.shape)

    @pl.when(i % 2 == 0)
    def _():
        finish(buf0_ref, 0)

    @pl.when(i % 2 == 1)
    def _():
        finish(buf1_ref, 1)


def _combine(cnt_t, off_t, gst_t, x1, pos, gate, ys, g, nb, seq):
    t, d = x1.shape
    tq = TIME_CHUNK
    tm = nb * tq
    sub = SUBLANES
    grid_spec = pltpu.PrefetchScalarGridSpec(
        num_scalar_prefetch=3,
        grid=(t // tm,),
        in_specs=[pl.BlockSpec((tm, d), lambda i, *_: (i, 0)),
                  pl.BlockSpec((sub, tm), lambda i, *_: (0, i)),
                  pl.BlockSpec((sub, tm), lambda i, *_: (0, i)),
                  pl.BlockSpec(memory_space=pl.ANY),
                  _full(g)],
        out_specs=pl.BlockSpec((nb, tq, d), lambda i, *_: (0, i, 0)),
        scratch_shapes=[pltpu.VMEM((2 * tm * sub, LANES), F32), pltpu.VMEM((2 * tm * sub, LANES), F32),
                        pltpu.SemaphoreType.DMA((2,))],
    )
    return pl.pallas_call(
        _combine_kernel,
        grid_spec=grid_spec,
        out_shape=jax.ShapeDtypeStruct((nb, seq, d), F32),
        compiler_params=_params(),
        name="combine",
    )(cnt_t, off_t, gst_t, x1, pos, gate, ys, g)


def _s5_tables(lam_re, lam_im, log_dt, b_re, b_im, c_re, c_im):
    ng, p = lam_re.shape
    npair = ng // 2
    kper = LANES // (2 * S5_GROUP)
    lam = lax.complex(lam_re, lam_im)
    dt = jnp.exp(log_dt)[:, None]
    lam_bar = jnp.exp(lam * dt)
    b_bar = ((lam_bar - 1.0) / lam)[..., None] * lax.complex(b_re, b_im)
    lr = jnp.real(lam_bar).reshape(npair, 2 * p)
    li = jnp.imag(lam_bar).reshape(npair, 2 * p)
    eye2 = jnp.eye(2, dtype=F32)
    sel = jax.nn.one_hot(jnp.arange(npair) % kper, kper, dtype=F32)

    bv = jnp.stack([jnp.real(b_bar), jnp.imag(b_bar)])
    bv = bv.reshape(2, npair, 2, p, S5_GROUP).transpose(1, 2, 4, 0, 3)
    bblk = bv[:, :, :, :, None, :] * eye2[None, :, None, None, :, None]
    bblk = bblk.reshape(npair, 2 * S5_GROUP, 4 * p)
    bm = (sel[:, :, None, None] * bblk[:, None]).reshape(npair, LANES, 4 * p)

    cv = jnp.stack([c_re, -c_im])
    cv = cv.reshape(2, npair, 2, S5_GROUP, p).transpose(1, 0, 2, 4, 3)
    cblk = cv[:, :, :, :, None, :] * eye2[None, None, :, None, :, None]
    cblk = cblk.reshape(npair, 4 * p, 2 * S5_GROUP)
    cm = (cblk[:, :, None, :] * sel[:, None, :, None]).reshape(npair, 4 * p, LANES)
    return bm.astype(BF16), lr, li, cm.astype(BF16)


def _blockdiag_tiles(w):
    nh, hi, ho = w.shape
    per = MXU_DIM // hi
    eye = jnp.eye(per, dtype=w.dtype)
    t = w.reshape(nh // per, per, hi, ho)[:, :, :, None, :] * eye[None, :, None, :, None]
    return t.reshape(nh // per, per * hi, per * ho)


def kernel(x, norm_mix, w_in, s5_lam_re, s5_lam_im, s5_log_dt, s5_b_re, s5_b_im, s5_c_re, s5_c_im, s5_d,
           w_s5_out, conv_w, conv_b, lru_w_r, lru_b_r, lru_w_i, lru_b_i, lru_lambda, w_lru_out, w_o, norm_ffn,
           router_group_w, router_group_b, router_expert_w, router_expert_b, expert_w_gate, expert_w_up,
           expert_w_down, norm_final):
    nb, seq, d = x.shape
    t = nb * seq
    assert w_in.shape[0] == 1, "one layer: the final RMSNorm is fused into the layer's combine kernel"
    l = 0
    s5w = s5_d.shape[-1]
    lruw = conv_b.shape[-1]
    rows = nb * TIME_CHUNK

    r = jnp.arange(rows)
    perm = jax.nn.one_hot((r % nb) * TIME_CHUNK + r // nb, rows, dtype=BF16)

    sp = jax.nn.softplus(-lru_lambda[l])[None, :]
    lrup = (conv_w[l], conv_b[l][None, :], _blockdiag_tiles(lru_w_r[l]).astype(BF16),
            _blockdiag_tiles(lru_w_i[l]).astype(BF16), lru_b_r[l][None, :], lru_b_i[l][None, :], sp)
    u, a, b, gg, ga, gb = _inproj(x, norm_mix[l][None, :], perm, w_in[l].astype(BF16), lrup, s5w, lruw)

    bm, lr, li, cm = _s5_tables(s5_lam_re[l], s5_lam_im[l], s5_log_dt[l], s5_b_re[l], s5_b_im[l],
                                s5_c_re[l], s5_c_im[l])
    s5p = (bm, lr, li, cm, s5_d[l][None, :], w_s5_out[l].astype(BF16))
    mixed = _mixers(u, a, b, gg, ga, gb, s5p, w_lru_out[l].astype(BF16), nb)

    gap = jnp.zeros((SUBLANES - N_GROUPS, d), F32)
    wrt = jnp.concatenate([router_group_w[l].T, gap, router_expert_w[l].T], axis=0)
    wr_hi = wrt.astype(BF16)
    wr_lo = (wrt - wr_hi.astype(F32)).astype(BF16)
    rb = jnp.concatenate([router_group_b[l], jnp.zeros((SUBLANES - N_GROUPS,), F32), router_expert_b[l]])
    rbias = jnp.broadcast_to(rb[:, None], (ROUTER_ROWS, LANES))

    x1, h2, eid, gate, tcnt = _mixroute(x, mixed, perm.T, w_o[l].astype(BF16), norm_ffn[l][None, :],
                                        wr_hi, wr_lo, rbias)

    n_tiles = seq // TIME_CHUNK
    cnt_t = tcnt.reshape(n_tiles, SUBLANES, LANES)[:, 0, :N_EXPERTS].astype(jnp.int32)
    off_t = jnp.cumsum(cnt_t, axis=1) - cnt_t
    counts = jnp.sum(cnt_t, axis=0)
    padded = ((counts + SLOT_BLOCK - 1) // SLOT_BLOCK) * SLOT_BLOCK
    pad_ends = jnp.cumsum(padded)
    pad_starts = pad_ends - padded
    gst_t = pad_starts[None, :] + jnp.cumsum(cnt_t, axis=0) - cnt_t
    offcol = jnp.broadcast_to(off_t.astype(F32)[:, :, None], (n_tiles, N_EXPERTS, LANES))
    n_blocks = -(-(2 * t) // SLOT_BLOCK) + N_EXPERTS
    n_slots = n_blocks * SLOT_BLOCK
    block_start = jnp.arange(n_blocks, dtype=jnp.int32) * SLOT_BLOCK
    block_e = jnp.minimum(jnp.sum((pad_ends[None, :] <= block_start[:, None]).astype(jnp.int32), axis=1),
                          N_EXPERTS - 1)
    n_used = pad_ends[-1:] // SLOT_BLOCK
    cnt_f, off_f, gst_f = cnt_t.reshape(-1), off_t.reshape(-1), gst_t.reshape(-1)

    tri = jnp.triu(jnp.ones((rows, rows), BF16), k=1)
    xs, pos = _dispatch(cnt_f, off_f, gst_f, counts, pad_starts, pad_ends, h2, eid, offcol, tri, n_slots)
    ys = _ffn(block_e, n_used, xs, expert_w_gate[l], expert_w_up[l], expert_w_down[l])
    return _combine(cnt_f, off_f, gst_f, x1, pos, gate, ys, norm_final[None, :], nb, seq)
```

```python
import functools
import math

import jax
import jax.numpy as jnp
from jax import lax
from jax.experimental import pallas as pl
from jax.experimental.pallas import tpu as pltpu

F32 = jnp.float32
BF16 = jnp.bfloat16

RMS_EPS = 1e-6
S5_GROUP = 16
CONV_WIDTH = 4
LRU_C = 8.0
N_GROUPS = 4
EXPERTS_PER_GROUP = 8
N_EXPERTS = N_GROUPS * EXPERTS_PER_GROUP

LANES = 128
SUBLANES = 8
MXU_DIM = 256
TIME_CHUNK = 64
SCAN_UNROLL = 4
SLOT_BLOCK = 512
ROUTE_WIDTH = 4096
ROUTER_ROWS = 40
VMEM_LIMIT = 56 * 1024 * 1024


def _gelu(x):
    c = math.sqrt(2.0 / math.pi)
    return 0.5 * x * (1.0 + jnp.tanh(c * (x + 0.044715 * (x * x * x))))


def _sigmoid(x):
    return 0.5 * jnp.tanh(0.5 * x) + 0.5


def _rms(x, g):
    return x * lax.rsqrt(jnp.mean(x * x, axis=-1, keepdims=True) + RMS_EPS) * g


def _params():
    return pltpu.CompilerParams(dimension_semantics=("arbitrary",), vmem_limit_bytes=VMEM_LIMIT)


def _full(a):
    return pl.BlockSpec(a.shape, lambda i, *_: (0,) * a.ndim)


def _tile(t):
    return pl.ds(pl.multiple_of(t * SUBLANES, SUBLANES), SUBLANES)


def _inproj_kernel(x_ref, g_ref, p_ref, w_ref, cw_ref, cb_ref, wr_ref, wi_ref, br_ref, bi_ref, sp_ref,
                   u_ref, a_ref, b_ref, gg_ref, ga_ref, gb_ref, xprev_ref, *, s5w, lruw):
    nb, tq, d = x_ref.shape
    rows = nb * tq
    halo = (CONV_WIDTH - 1) * nb

    @pl.when(pl.program_id(0) == 0)
    def _():
        xprev_ref[...] = jnp.zeros_like(xprev_ref)

    h = _rms(x_ref[...].reshape(rows, d), g_ref[...])
    hb = jnp.dot(p_ref[...], h.astype(BF16), preferred_element_type=F32).astype(BF16)

    def proj(lo, hi):
        return jnp.dot(hb, w_ref[:, lo:hi], preferred_element_type=F32)

    o1 = s5w
    o2 = o1 + lruw
    o3 = o2 + lruw
    o4 = o3 + d
    xin = proj(o1, o2)
    xext = jnp.concatenate([xprev_ref[...], xin], axis=0)
    xprev_ref[...] = xin[rows - halo:, :]
    xc_all = cb_ref[...]
    for k in range(CONV_WIDTH):
        xc_all = xc_all + xext[k * nb:k * nb + rows, :] * cw_ref[k:k + 1, :]

    def lru_gates(j):
        cs = slice(MXU_DIM * j, MXU_DIM * (j + 1))
        xc = xc_all[:, cs]
        xcb = xc.astype(BF16)
        r = _sigmoid(jnp.dot(xcb, wr_ref[j], preferred_element_type=F32) + br_ref[:, cs])
        ig = _sigmoid(jnp.dot(xcb, wi_ref[j], preferred_element_type=F32) + bi_ref[:, cs])
        log_a = (-LRU_C) * r * sp_ref[:, cs]
        th = jnp.tanh(log_a)
        mult = jnp.sqrt((-2.0 * th) / (1.0 - th))
        a_ref[:, cs] = jnp.exp(log_a)
        b_ref[:, cs] = mult * (ig * xc)

    others = [lambda: u_ref.__setitem__(Ellipsis, proj(0, o1)),
              lambda: gg_ref.__setitem__(Ellipsis, _gelu(proj(o2, o3)).astype(BF16)),
              lambda: ga_ref.__setitem__(Ellipsis, _sigmoid(proj(o3, o4)).astype(BF16)),
              lambda: gb_ref.__setitem__(Ellipsis, _sigmoid(proj(o4, o4 + d)).astype(BF16))]
    ntile = wr_ref.shape[0]
    for j in range(max(ntile, len(others))):
        if j < ntile:
            lru_gates(j)
        if j < len(others):
            others[j]()


def _inproj(x, g, perm, w_in_b, lrup, s5w, lruw):
    nb, seq, d = x.shape
    tq = TIME_CHUNK
    rows = nb * tq
    t = nb * seq
    row = lambda c: pl.BlockSpec((rows, c), lambda i: (i, 0))
    return pl.pallas_call(
        functools.partial(_inproj_kernel, s5w=s5w, lruw=lruw),
        grid=(seq // tq,),
        in_specs=[pl.BlockSpec((nb, tq, d), lambda i: (0, i, 0)), _full(g), _full(perm), _full(w_in_b)]
        + [_full(p) for p in lrup],
        out_specs=[row(s5w), row(lruw), row(lruw), row(lruw), row(d), row(d)],
        out_shape=[jax.ShapeDtypeStruct((t, s5w), F32), jax.ShapeDtypeStruct((t, lruw), F32),
                   jax.ShapeDtypeStruct((t, lruw), F32), jax.ShapeDtypeStruct((t, lruw), BF16),
                   jax.ShapeDtypeStruct((t, d), BF16), jax.ShapeDtypeStruct((t, d), BF16)],
        scratch_shapes=[pltpu.VMEM(((CONV_WIDTH - 1) * nb, lruw), F32)],
        compiler_params=_params(),
        name="inproj",
    )(x, g, perm, w_in_b, *lrup)


def _mixers_kernel(u_ref, a_ref, b_ref, gg_ref, ga_ref, gb_ref,
                   bm_ref, lr_ref, li_ref, cm_ref, dk_ref, wo_ref, wlo_ref,
                   o_ref, state_ref, sbuf_ref, h_ref, hbuf_ref, *, tq, nb):
    d = o_ref.shape[-1]
    npair = bm_ref.shape[0]
    kper = LANES // (2 * S5_GROUP)

    @pl.when(pl.program_id(0) == 0)
    def _():
        state_ref[...] = jnp.zeros_like(state_ref)
        h_ref[...] = jnp.zeros_like(h_ref)

    u2 = u_ref[...]
    ub = u2.astype(BF16)
    for k in range(npair):
        kb = k // kper
        bu = jnp.dot(ub[:, LANES * kb:LANES * (kb + 1)], bm_ref[k], preferred_element_type=F32)
        sbuf_ref[2 * k] = bu[:, :LANES]
        sbuf_ref[2 * k + 1] = bu[:, LANES:]

    pairs_per_loop = 4
    for kk in range(npair // pairs_per_loop):
        k0 = kk * pairs_per_loop
        lr = [jnp.broadcast_to(lr_ref[k0 + j:k0 + j + 1, :], (nb, LANES)) for j in range(pairs_per_loop)]
        li = [jnp.broadcast_to(li_ref[k0 + j:k0 + j + 1, :], (nb, LANES)) for j in range(pairs_per_loop)]
        init = tuple(state_ref[2 * k0 + j] for j in range(2 * pairs_per_loop))

        def body(it, carry, k0=k0, lr=lr, li=li):
            carry = list(carry)
            for s in range(SCAN_UNROLL):
                rows = _tile(it * SCAN_UNROLL + s)
                for j in range(pairs_per_loop):
                    sre, sim = carry[2 * j], carry[2 * j + 1]
                    nre = lr[j] * sre - li[j] * sim + sbuf_ref[2 * (k0 + j), rows, :]
                    nim = lr[j] * sim + li[j] * sre + sbuf_ref[2 * (k0 + j) + 1, rows, :]
                    sbuf_ref[2 * (k0 + j), rows, :] = nre
                    sbuf_ref[2 * (k0 + j) + 1, rows, :] = nim
                    carry[2 * j], carry[2 * j + 1] = nre, nim
            return tuple(carry)

        fin = lax.fori_loop(0, tq // SCAN_UNROLL, body, init)
        for j in range(2 * pairs_per_loop):
            state_ref[2 * k0 + j] = fin[j]

    def scan_body(it, h):
        for s in range(SCAN_UNROLL):
            tr = _tile(it * SCAN_UNROLL + s)
            h = a_ref[tr, :] * h + b_ref[tr, :]
            hbuf_ref[tr, :] = h
        return h

    h_ref[...] = lax.fori_loop(0, tq // SCAN_UNROLL, scan_body, h_ref[...])

    ys = []
    for kb in range(npair // kper):
        acc = None
        for j in range(kper):
            k = kb * kper + j
            s = jnp.concatenate([sbuf_ref[2 * k], sbuf_ref[2 * k + 1]], axis=1).astype(BF16)
            p = jnp.dot(s, cm_ref[k], preferred_element_type=F32)
            acc = p if acc is None else acc + p
        ys.append(acc)
    y = jnp.concatenate(ys, axis=1) + dk_ref[...] * u2
    z = jnp.dot(_gelu(y).astype(BF16), wo_ref[...], preferred_element_type=F32)
    ya = z[:, :d] * _sigmoid(z[:, d:])

    yl = hbuf_ref[...] * gg_ref[...].astype(F32)
    yb = jnp.dot(yl.astype(BF16), wlo_ref[...], preferred_element_type=F32)
    o_ref[...] = (ya * ga_ref[...].astype(F32) + yb * gb_ref[...].astype(F32)).astype(BF16)


def _mixers(u, a, b, gg, ga, gb, s5p, w_lru_out_b, nb):
    t, s5w = u.shape
    w = a.shape[-1]
    d = ga.shape[-1]
    tq = TIME_CHUNK
    rows = nb * tq
    nstate = s5p[0].shape[0] * MXU_DIM
    row = lambda c: pl.BlockSpec((rows, c), lambda i: (i, 0))
    params = tuple(s5p) + (w_lru_out_b,)
    return pl.pallas_call(
        functools.partial(_mixers_kernel, tq=tq, nb=nb),
        grid=(t // rows,),
        in_specs=[row(s5w), row(w), row(w), row(w), row(d), row(d)] + [_full(p) for p in params],
        out_specs=row(d),
        out_shape=jax.ShapeDtypeStruct((t, d), BF16),
        scratch_shapes=[pltpu.VMEM((nstate // LANES, nb, LANES), F32),
                        pltpu.VMEM((nstate // LANES, rows, LANES), F32),
                        pltpu.VMEM((nb, w), F32),
                        pltpu.VMEM((rows, w), F32)],
        compiler_params=_params(),
        name="mixers",
    )(u, a, b, gg, ga, gb, *params)


def _mixroute_kernel(x_ref, m_ref, pt_ref, wo_ref, g_ref, wh_ref, wl_ref, rb_ref,
                     x1_ref, h2_ref, lt_ref):
    nb, tq, d = x_ref.shape
    tm = nb * tq
    m = jnp.dot(pt_ref[...], m_ref[...], preferred_element_type=F32).astype(BF16)
    x1 = x_ref[...].reshape(tm, d) + jnp.dot(m, wo_ref[...], preferred_element_type=F32)
    x1_ref[...] = x1
    h = _rms(x1, g_ref[...])

    hh = h.astype(BF16)
    h2_ref[...] = hh
    hl = (h - hh.astype(F32)).astype(BF16)
    dn = (((1,), (1,)), ((), ()))
    lt = (lax.dot_general(wh_ref[...], hh, dn, preferred_element_type=F32)
          + lax.dot_general(wh_ref[...], hl, dn, preferred_element_type=F32)
          + lax.dot_general(wl_ref[...], hh, dn, preferred_element_type=F32))
    lt_ref[...] = lt + rb_ref[:, 0:1]


def _mixroute(x, m, perm_t, w_o_b, g, wr_hi, wr_lo, rbias):
    nb, seq, d = x.shape
    tq = TIME_CHUNK
    tm = nb * tq
    t = nb * seq
    row = lambda c: pl.BlockSpec((tm, c), lambda i: (i, 0))
    return pl.pallas_call(
        _mixroute_kernel,
        grid=(seq // tq,),
        in_specs=[pl.BlockSpec((nb, tq, d), lambda i: (0, i, 0)), row(d), _full(perm_t), _full(w_o_b),
                  _full(g), _full(wr_hi), _full(wr_lo), _full(rbias)],
        out_specs=[row(d), row(d), pl.BlockSpec((ROUTER_ROWS, tm), lambda i: (0, i))],
        out_shape=[jax.ShapeDtypeStruct((t, d), F32), jax.ShapeDtypeStruct((t, d), BF16),
                   jax.ShapeDtypeStruct((ROUTER_ROWS, t), F32)],
        compiler_params=_params(),
        name="mixroute",
    )(x, m, perm_t, w_o_b, g, wr_hi, wr_lo, rbias)


def _route_kernel(lt_ref, tri_ref, ltri_ref, gate_ref, pos_ref, cnt_ref, *, tile):
    lt = lt_ref[...]
    tm = lt.shape[1]
    gl = lt[0:N_GROUPS, :]
    gmax = jnp.max(gl, axis=0, keepdims=True)
    gsum = jnp.sum(jnp.exp(gl - gmax), axis=0, keepdims=True)
    p_g = 1.0 / gsum
    iota_g = lax.broadcasted_iota(jnp.int32, gl.shape, 0)
    g_idx = jnp.min(jnp.where(gl == gmax, iota_g, N_GROUPS), axis=0, keepdims=True)

    el = lt[SUBLANES:SUBLANES + EXPERTS_PER_GROUP, :]
    for g in range(1, N_GROUPS):
        lo = SUBLANES + EXPERTS_PER_GROUP * g
        el = jnp.where(g_idx == g, lt[lo:lo + EXPERTS_PER_GROUP, :], el)
    emax = jnp.max(el, axis=0, keepdims=True)
    esum = jnp.sum(jnp.exp(el - emax), axis=0, keepdims=True)
    iota_e = lax.broadcasted_iota(jnp.int32, el.shape, 0)
    i1 = jnp.min(jnp.where(el == emax, iota_e, EXPERTS_PER_GROUP), axis=0, keepdims=True)
    el2 = jnp.where(iota_e == i1, -jnp.inf, el)
    emax2 = jnp.max(el2, axis=0, keepdims=True)
    i2 = jnp.min(jnp.where(el2 == emax2, iota_e, EXPERTS_PER_GROUP), axis=0, keepdims=True)
    p1 = 1.0 / esum
    p2 = jnp.exp(emax2 - emax) / esum
    psum = p1 + p2
    g1 = p_g * (p1 / psum)
    g2 = p_g * (p2 / psum)
    e1 = g_idx * EXPERTS_PER_GROUP + i1
    e2 = g_idx * EXPERTS_PER_GROUP + i2
    gate_ref[...] = jnp.concatenate([g1, g2, jnp.zeros((SUBLANES - 2, tm), F32)], axis=0)

    dn = (((1,), (1,)), ((), ()))
    iota = lax.broadcasted_iota(jnp.int32, (N_EXPERTS, tile), 0)
    iota_l = lax.broadcasted_iota(jnp.int32, (LANES, tile), 0)
    ones = jnp.ones((SUBLANES, tile), BF16)
    for s in range(tm // tile):
        cs = slice(s * tile, (s + 1) * tile)
        a1, a2 = e1[:, cs], e2[:, cs]
        oh0 = iota == a1
        oh1 = iota == a2
        c = jnp.where(oh0 | oh1, 1.0, 0.0)
        ci = jnp.sum(c, axis=1, keepdims=True).astype(jnp.int32)
        hi = jnp.broadcast_to((ci >> 5).astype(F32), (N_EXPERTS, LANES)).astype(BF16)
        lo = jnp.broadcast_to((ci & 31).astype(F32), (N_EXPERTS, LANES)).astype(BF16)
        first = (32.0 * jnp.dot(ltri_ref[...], hi, preferred_element_type=F32)
                 + jnp.dot(ltri_ref[...], lo, preferred_element_type=F32))
        before = jnp.dot(c.astype(BF16), tri_ref[...], preferred_element_type=F32) + first[:, 0:1]
        p0 = jnp.sum(jnp.where(oh0, before, 0.0), axis=0, keepdims=True)
        p1 = jnp.sum(jnp.where(oh1, before, 0.0), axis=0, keepdims=True)
        pos_ref[:, cs] = jnp.concatenate([p0, p1, jnp.zeros((SUBLANES - 2, tile), F32)], axis=0)
        hits = jnp.where((iota_l == a1) | (iota_l == a2), 1.0, 0.0).astype(BF16)
        cnt_ref[SUBLANES * s:SUBLANES * (s + 1), :] = lax.dot_general(ones, hits, dn, preferred_element_type=F32)


def _route(lt, tri, ltri):
    t = lt.shape[1]
    tile = tri.shape[0]
    tm = min(t, ROUTE_WIDTH)
    col = pl.BlockSpec((SUBLANES, tm), lambda i: (0, i))
    return pl.pallas_call(
        functools.partial(_route_kernel, tile=tile),
        grid=(t // tm,),
        in_specs=[pl.BlockSpec((ROUTER_ROWS, tm), lambda i: (0, i)), _full(tri), _full(ltri)],
        out_specs=[col, col, pl.BlockSpec((tm // tile * SUBLANES, LANES), lambda i: (i, 0))],
        out_shape=[jax.ShapeDtypeStruct((SUBLANES, t), F32), jax.ShapeDtypeStruct((SUBLANES, t), F32),
                   jax.ShapeDtypeStruct((t // tile * SUBLANES, LANES), F32)],
        compiler_params=_params(),
        name="route",
    )(lt, tri, ltri)


def _tokens(ref, tok, n):
    return ref.at[pl.ds(pl.multiple_of(tok * SUBLANES, SUBLANES), n * SUBLANES)]


def _strip_copies(n, src, src_tok, dst, dst_tok, sem, max_tokens, wait=False):
    del max_tokens

    @pl.when(n > 0)
    def _():
        cp = pltpu.make_async_copy(_tokens(src, src_tok, n), _tokens(dst, dst_tok, n), sem)
        if wait:
            cp.wait()
        else:
            cp.start()


def _dispatch_kernel(cnt_ref, off_ref, gst_ref, tot_ref, pst_ref, pen_ref,
                     h2_ref, pos_ref, xs_ref,
                     stage0_ref, stage1_ref, zero_ref, sem, zsem):
    i = pl.program_id(0)
    last = pl.num_programs(0) - 1
    tm = pos_ref.shape[1]
    na = 2 * tm
    d = h2_ref.shape[1]

    p0i = pos_ref[0:1, :].astype(jnp.int32)
    p1i = pos_ref[1:2, :].astype(jnp.int32)
    rid = lax.broadcasted_iota(jnp.int32, (na, tm), 0)
    onehot = jnp.where((rid == p0i) | (rid == p1i), 1.0, 0.0).astype(BF16)
    srt = jnp.dot(onehot, h2_ref[...], preferred_element_type=F32)

    def step(stage, s):
        @pl.when(i >= 2)
        def _():
            pltpu.make_async_copy(stage, _tokens(xs_ref, 0, na), sem.at[s]).wait()

        for j in range(d // LANES):
            stage[pl.ds(j, na, stride=d // LANES), :] = srt[:, LANES * j:LANES * (j + 1)]

        def per_expert(e, c):
            k = i * N_EXPERTS + e
            _strip_copies(cnt_ref[k], stage, off_ref[k], xs_ref, gst_ref[k], sem.at[s], na)
            return c

        lax.fori_loop(0, N_EXPERTS, per_expert, 0)

    @pl.when(i % 2 == 0)
    def _():
        step(stage0_ref, 0)

    @pl.when(i % 2 == 1)
    def _():
        step(stage1_ref, 1)

    @pl.when(i == last)
    def _():
        zero_ref[...] = jnp.zeros_like(zero_ref)
        for wait in (False, True):
            for e in range(N_EXPERTS):
                _strip_copies(pen_ref[e] - pst_ref[e] - tot_ref[e], zero_ref, 0, xs_ref,
                              pst_ref[e] + tot_ref[e], zsem, SLOT_BLOCK - 1, wait=wait)

        nblk = xs_ref.shape[0] // zero_ref.shape[0]

        def bcopy(b):
            return pltpu.make_async_copy(zero_ref, _tokens(xs_ref, b * SLOT_BLOCK, SLOT_BLOCK), zsem)

        first_unused = pen_ref[N_EXPERTS - 1] // SLOT_BLOCK
        lax.fori_loop(first_unused, nblk, lambda b, c: (bcopy(b).start(), c)[1], 0)
        lax.fori_loop(first_unused, nblk, lambda b, c: (bcopy(b).wait(), c)[1], 0)

        @pl.when(i >= 1)
        def _():
            @pl.when(i % 2 == 0)
            def _():
                pltpu.make_async_copy(stage1_ref, _tokens(xs_ref, 0, na), sem.at[1]).wait()

            @pl.when(i % 2 == 1)
            def _():
                pltpu.make_async_copy(stage0_ref, _tokens(xs_ref, 0, na), sem.at[0]).wait()

        @pl.when(i % 2 == 0)
        def _():
            pltpu.make_async_copy(stage0_ref, _tokens(xs_ref, 0, na), sem.at[0]).wait()

        @pl.when(i % 2 == 1)
        def _():
            pltpu.make_async_copy(stage1_ref, _tokens(xs_ref, 0, na), sem.at[1]).wait()


def _dispatch(cnt_t, off_t, gst_t, tot, pst, pen, h2, pos, tm, n_slots):
    sub = SUBLANES
    t, d = h2.shape
    grid_spec = pltpu.PrefetchScalarGridSpec(
        num_scalar_prefetch=6,
        grid=(t // tm,),
        in_specs=[pl.BlockSpec((tm, d), lambda i, *_: (i, 0)),
                  pl.BlockSpec((sub, tm), lambda i, *_: (0, i))],
        out_specs=pl.BlockSpec(memory_space=pl.ANY),
        scratch_shapes=[pltpu.VMEM((2 * tm * sub, LANES), F32), pltpu.VMEM((2 * tm * sub, LANES), F32),
                        pltpu.VMEM((SLOT_BLOCK * sub, LANES), F32), pltpu.SemaphoreType.DMA((2,)),
                        pltpu.SemaphoreType.DMA(())],
    )
    return pl.pallas_call(
        _dispatch_kernel,
        grid_spec=grid_spec,
        out_shape=jax.ShapeDtypeStruct((n_slots * sub, LANES), F32),
        compiler_params=_params(),
        name="dispatch",
    )(cnt_t, off_t, gst_t, tot, pst, pen, h2, pos)


def _ffn_kernel(be_ref, nu_ref, xs_ref, wg_ref, wu_ref, wd_ref, ys_ref, wgb_ref, wub_ref, wdb_ref, *, blk):
    i = pl.program_id(0)
    sub = SUBLANES

    @pl.when(i < nu_ref[0])
    def _():
        prev = be_ref[jnp.maximum(i - 1, 0)]

        @pl.when((i == 0) | (be_ref[i] != prev))
        def _():
            wgb_ref[...] = wg_ref[...].astype(BF16)
            wub_ref[...] = wu_ref[...].astype(BF16)
            wdb_ref[...] = wd_ref[...].astype(BF16)

        x = jnp.concatenate([xs_ref[pl.ds(j, blk, stride=sub), :] for j in range(sub)], axis=1).astype(BF16)
        g = jnp.dot(x, wgb_ref[...], preferred_element_type=F32)
        u = jnp.dot(x, wub_ref[...], preferred_element_type=F32)
        a = (g * _sigmoid(g)) * u
        y = jnp.dot(a.astype(BF16), wdb_ref[...], preferred_element_type=F32)
        for j in range(sub):
            ys_ref[pl.ds(j, blk, stride=sub), :] = y[:, LANES * j:LANES * (j + 1)]

    @pl.when(i >= nu_ref[0])
    def _():
        ys_ref[...] = jnp.zeros_like(ys_ref)


def _ffn(block_e, n_used, xs, wg, wu, wd):
    sub = SUBLANES
    blk = SLOT_BLOCK
    n_blocks = xs.shape[0] // (sub * blk)
    ne, d, de = wg.shape

    def slot_map(i, be, nu):
        return (i, 0)

    def w_map(i, be, nu):
        return (be[jnp.minimum(i, nu[0] - 1)], 0, 0)

    grid_spec = pltpu.PrefetchScalarGridSpec(
        num_scalar_prefetch=2,
        grid=(n_blocks,),
        in_specs=[pl.BlockSpec((blk * sub, LANES), slot_map),
                  pl.BlockSpec((None, d, de), w_map), pl.BlockSpec((None, d, de), w_map),
                  pl.BlockSpec((None, de, d), w_map)],
        out_specs=pl.BlockSpec((blk * sub, LANES), slot_map),
        scratch_shapes=[pltpu.VMEM((d, de), BF16), pltpu.VMEM((d, de), BF16), pltpu.VMEM((de, d), BF16)],
    )
    return pl.pallas_call(
        functools.partial(_ffn_kernel, blk=blk),
        grid_spec=grid_spec,
        out_shape=jax.ShapeDtypeStruct(xs.shape, F32),
        compiler_params=_params(),
        name="ffn",
    )(block_e, n_used, xs, wg, wu, wd)


def _combine_kernel(cnt_ref, off_ref, gst_ref, x1_ref, pos_ref, gate_ref, ys_ref, g_ref, o_ref,
                    buf0_ref, buf1_ref, sem):
    i = pl.program_id(0)
    n = pl.num_programs(0)
    tm, d = x1_ref.shape
    na = 2 * tm
    sub = SUBLANES

    def fetch(tile, buf, s):
        def per_expert(e, c):
            k = tile * N_EXPERTS + e
            _strip_copies(cnt_ref[k], ys_ref, gst_ref[k], buf, off_ref[k], sem.at[s], na)
            return c

        lax.fori_loop(0, N_EXPERTS, per_expert, 0)

    @pl.when(i == 0)
    def _():
        fetch(0, buf0_ref, 0)

    @pl.when((i + 1 < n) & (i % 2 == 0))
    def _():
        fetch(i + 1, buf1_ref, 1)

    @pl.when((i + 1 < n) & (i % 2 == 1))
    def _():
        fetch(i + 1, buf0_ref, 0)

    rows4 = jnp.concatenate([pos_ref[0:2, :], gate_ref[0:2, :], jnp.zeros((sub - 4, tm), F32)], axis=0)
    cols = jnp.concatenate([rows4] * (LANES // sub), axis=0).T
    lane = lax.broadcasted_iota(jnp.int32, (tm, na), 1)
    pick0 = jnp.where(lane == cols[:, 0:1].astype(jnp.int32), 1.0, 0.0).astype(BF16)
    pick1 = jnp.where(lane == cols[:, 1:2].astype(jnp.int32), 1.0, 0.0).astype(BF16)

    def finish(buf, s):
        pltpu.make_async_copy(_tokens(ys_ref, 0, na), buf, sem.at[s]).wait()
        y = jnp.concatenate([buf[pl.ds(j, na, stride=sub), :] for j in range(d // LANES)], axis=1).astype(BF16)
        y0 = jnp.dot(pick0, y, preferred_element_type=F32)
        y1 = jnp.dot(pick1, y, preferred_element_type=F32)
        x = x1_ref[...] + (cols[:, 2:3] * y0 + cols[:, 3:4] * y1)
        o_ref[...] = _rms(x, g_ref[...]).reshape(o_ref.shape)

    @pl.when(i % 2 == 0)
    def _():
        finish(buf0_ref, 0)

    @pl.when(i % 2 == 1)
    def _():
        finish(buf1_ref, 1)


def _combine(cnt_t, off_t, gst_t, x1, pos, gate, ys, g, nb, seq):
    t, d = x1.shape
    tq = TIME_CHUNK
    tm = nb * tq
    sub = SUBLANES
    grid_spec = pltpu.PrefetchScalarGridSpec(
        num_scalar_prefetch=3,
        grid=(t // tm,),
        in_specs=[pl.BlockSpec((tm, d), lambda i, *_: (i, 0)),
                  pl.BlockSpec((sub, tm), lambda i, *_: (0, i)),
                  pl.BlockSpec((sub, tm), lambda i, *_: (0, i)),
                  pl.BlockSpec(memory_space=pl.ANY),
                  _full(g)],
        out_specs=pl.BlockSpec((nb, tq, d), lambda i, *_: (0, i, 0)),
        scratch_shapes=[pltpu.VMEM((2 * tm * sub, LANES), F32), pltpu.VMEM((2 * tm * sub, LANES), F32),
                        pltpu.SemaphoreType.DMA((2,))],
    )
    return pl.pallas_call(
        _combine_kernel,
        grid_spec=grid_spec,
        out_shape=jax.ShapeDtypeStruct((nb, seq, d), F32),
        compiler_params=_params(),
        name="combine",
    )(cnt_t, off_t, gst_t, x1, pos, gate, ys, g)


def _s5_tables(lam_re, lam_im, log_dt, b_re, b_im, c_re, c_im):
    ng, p = lam_re.shape
    npair = ng // 2
    kper = LANES // (2 * S5_GROUP)
    lam = lax.complex(lam_re, lam_im)
    dt = jnp.exp(log_dt)[:, None]
    lam_bar = jnp.exp(lam * dt)
    b_bar = ((lam_bar - 1.0) / lam)[..., None] * lax.complex(b_re, b_im)
    lr = jnp.real(lam_bar).reshape(npair, 2 * p)
    li = jnp.imag(lam_bar).reshape(npair, 2 * p)
    eye2 = jnp.eye(2, dtype=F32)
    sel = jax.nn.one_hot(jnp.arange(npair) % kper, kper, dtype=F32)

    bv = jnp.stack([jnp.real(b_bar), jnp.imag(b_bar)])
    bv = bv.reshape(2, npair, 2, p, S5_GROUP).transpose(1, 2, 4, 0, 3)
    bblk = bv[:, :, :, :, None, :] * eye2[None, :, None, None, :, None]
    bblk = bblk.reshape(npair, 2 * S5_GROUP, 4 * p)
    bm = (sel[:, :, None, None] * bblk[:, None]).reshape(npair, LANES, 4 * p)

    cv = jnp.stack([c_re, -c_im])
    cv = cv.reshape(2, npair, 2, S5_GROUP, p).transpose(1, 0, 2, 4, 3)
    cblk = cv[:, :, :, :, None, :] * eye2[None, None, :, None, :, None]
    cblk = cblk.reshape(npair, 4 * p, 2 * S5_GROUP)
    cm = (cblk[:, :, None, :] * sel[:, None, :, None]).reshape(npair, 4 * p, LANES)
    return bm.astype(BF16), lr, li, cm.astype(BF16)


def _blockdiag_tiles(w):
    nh, hi, ho = w.shape
    per = MXU_DIM // hi
    eye = jnp.eye(per, dtype=w.dtype)
    t = w.reshape(nh // per, per, hi, ho)[:, :, :, None, :] * eye[None, :, None, :, None]
    return t.reshape(nh // per, per * hi, per * ho)


def kernel(x, norm_mix, w_in, s5_lam_re, s5_lam_im, s5_log_dt, s5_b_re, s5_b_im, s5_c_re, s5_c_im, s5_d,
           w_s5_out, conv_w, conv_b, lru_w_r, lru_b_r, lru_w_i, lru_b_i, lru_lambda, w_lru_out, w_o, norm_ffn,
           router_group_w, router_group_b, router_expert_w, router_expert_b, expert_w_gate, expert_w_up,
           expert_w_down, norm_final):
    nb, seq, d = x.shape
    t = nb * seq
    assert w_in.shape[0] == 1, "one layer: the final RMSNorm is fused into the layer's combine kernel"
    l = 0
    s5w = s5_d.shape[-1]
    lruw = conv_b.shape[-1]
    rows = nb * TIME_CHUNK

    r = jnp.arange(rows)
    perm = jax.nn.one_hot((r % nb) * TIME_CHUNK + r // nb, rows, dtype=BF16)

    sp = jax.nn.softplus(-lru_lambda[l])[None, :]
    lrup = (conv_w[l], conv_b[l][None, :], _blockdiag_tiles(lru_w_r[l]).astype(BF16),
            _blockdiag_tiles(lru_w_i[l]).astype(BF16), lru_b_r[l][None, :], lru_b_i[l][None, :], sp)
    u, a, b, gg, ga, gb = _inproj(x, norm_mix[l][None, :], perm, w_in[l].astype(BF16), lrup, s5w, lruw)

    bm, lr, li, cm = _s5_tables(s5_lam_re[l], s5_lam_im[l], s5_log_dt[l], s5_b_re[l], s5_b_im[l],
                                s5_c_re[l], s5_c_im[l])
    s5p = (bm, lr, li, cm, s5_d[l][None, :], w_s5_out[l].astype(BF16))
    mixed = _mixers(u, a, b, gg, ga, gb, s5p, w_lru_out[l].astype(BF16), nb)

    gap = jnp.zeros((SUBLANES - N_GROUPS, d), F32)
    wrt = jnp.concatenate([router_group_w[l].T, gap, router_expert_w[l].T], axis=0)
    wr_hi = wrt.astype(BF16)
    wr_lo = (wrt - wr_hi.astype(F32)).astype(BF16)
    rb = jnp.concatenate([router_group_b[l], jnp.zeros((SUBLANES - N_GROUPS,), F32), router_expert_b[l]])
    rbias = jnp.broadcast_to(rb[:, None], (ROUTER_ROWS, LANES))

    x1, h2, logits = _mixroute(x, mixed, perm.T, w_o[l].astype(BF16), norm_ffn[l][None, :], wr_hi, wr_lo, rbias)
    tri = jnp.triu(jnp.ones((rows, rows), BF16), k=1)
    ltri = jnp.tril(jnp.ones((N_EXPERTS, N_EXPERTS), BF16), k=-1)
    gate, pos, tcnt = _route(logits, tri, ltri)

    n_tiles = seq // TIME_CHUNK
    cnt_t = tcnt.reshape(n_tiles, SUBLANES, LANES)[:, 0, :N_EXPERTS].astype(jnp.int32)
    off_t = jnp.cumsum(cnt_t, axis=1) - cnt_t
    counts = jnp.sum(cnt_t, axis=0)
    padded = ((counts + SLOT_BLOCK - 1) // SLOT_BLOCK) * SLOT_BLOCK
    pad_ends = jnp.cumsum(padded)
    pad_starts = pad_ends - padded
    gst_t = pad_starts[None, :] + jnp.cumsum(cnt_t, axis=0) - cnt_t
    n_blocks = -(-(2 * t) // SLOT_BLOCK) + N_EXPERTS
    n_slots = n_blocks * SLOT_BLOCK
    block_start = jnp.arange(n_blocks, dtype=jnp.int32) * SLOT_BLOCK
    block_e = jnp.minimum(jnp.sum((pad_ends[None, :] <= block_start[:, None]).astype(jnp.int32), axis=1),
                          N_EXPERTS - 1)
    n_used = pad_ends[-1:] // SLOT_BLOCK
    cnt_f, off_f, gst_f = cnt_t.reshape(-1), off_t.reshape(-1), gst_t.reshape(-1)

    xs = _dispatch(cnt_f, off_f, gst_f, counts, pad_starts, pad_ends, h2, pos, rows, n_slots)
    ys = _ffn(block_e, n_used, xs, expert_w_gate[l], expert_w_up[l], expert_w_down[l])
    return _combine(cnt_f, off_f, gst_f, x1, pos, gate, ys, norm_final[None, :], nb, seq)
```

```python
import functools
import math

import jax
import jax.numpy as jnp
from jax import lax
from jax.experimental import pallas as pl
from jax.experimental.pallas import tpu as pltpu

F32 = jnp.float32
BF16 = jnp.bfloat16

RMS_EPS = 1e-6
S5_GROUP = 16
CONV_WIDTH = 4
LRU_C = 8.0
N_GROUPS = 4
EXPERTS_PER_GROUP = 8
N_EXPERTS = N_GROUPS * EXPERTS_PER_GROUP

LANES = 128
SUBLANES = 8
MXU_DIM = 256
TIME_CHUNK = 64
SCAN_UNROLL = 4
SLOT_BLOCK = 512
ROUTE_WIDTH = 4096
ROUTER_ROWS = 40
VMEM_LIMIT = 56 * 1024 * 1024


def _gelu(x):
    c = math.sqrt(2.0 / math.pi)
    return 0.5 * x * (1.0 + jnp.tanh(c * (x + 0.044715 * (x * x * x))))


def _sigmoid(x):
    return 0.5 * jnp.tanh(0.5 * x) + 0.5


def _rms(x, g):
    return x * lax.rsqrt(jnp.mean(x * x, axis=-1, keepdims=True) + RMS_EPS) * g


def _params():
    return pltpu.CompilerParams(dimension_semantics=("arbitrary",), vmem_limit_bytes=VMEM_LIMIT)


def _full(a):
    return pl.BlockSpec(a.shape, lambda i, *_: (0,) * a.ndim)


def _tile(t):
    return pl.ds(pl.multiple_of(t * SUBLANES, SUBLANES), SUBLANES)


def _inproj_kernel(x_ref, g_ref, p_ref, w_ref, cw_ref, cb_ref, wr_ref, wi_ref, br_ref, bi_ref, sp_ref,
                   u_ref, a_ref, b_ref, gg_ref, ga_ref, gb_ref, xprev_ref, *, s5w, lruw):
    nb, tq, d = x_ref.shape
    rows = nb * tq
    halo = (CONV_WIDTH - 1) * nb

    @pl.when(pl.program_id(0) == 0)
    def _():
        xprev_ref[...] = jnp.zeros_like(xprev_ref)

    h = _rms(x_ref[...].reshape(rows, d), g_ref[...])
    hb = jnp.dot(p_ref[...], h.astype(BF16), preferred_element_type=F32).astype(BF16)

    def proj(lo, hi):
        return jnp.dot(hb, w_ref[:, lo:hi], preferred_element_type=F32)

    o1 = s5w
    o2 = o1 + lruw
    o3 = o2 + lruw
    o4 = o3 + d
    xin = proj(o1, o2)
    xext = jnp.concatenate([xprev_ref[...], xin], axis=0)
    xprev_ref[...] = xin[rows - halo:, :]
    xc_all = cb_ref[...]
    for k in range(CONV_WIDTH):
        xc_all = xc_all + xext[k * nb:k * nb + rows, :] * cw_ref[k:k + 1, :]

    def lru_gates(j):
        cs = slice(MXU_DIM * j, MXU_DIM * (j + 1))
        xc = xc_all[:, cs]
        xcb = xc.astype(BF16)
        r = _sigmoid(jnp.dot(xcb, wr_ref[j], preferred_element_type=F32) + br_ref[:, cs])
        ig = _sigmoid(jnp.dot(xcb, wi_ref[j], preferred_element_type=F32) + bi_ref[:, cs])
        log_a = (-LRU_C) * r * sp_ref[:, cs]
        th = jnp.tanh(log_a)
        mult = jnp.sqrt((-2.0 * th) / (1.0 - th))
        a_ref[:, cs] = jnp.exp(log_a)
        b_ref[:, cs] = mult * (ig * xc)

    others = [lambda: u_ref.__setitem__(Ellipsis, proj(0, o1)),
              lambda: gg_ref.__setitem__(Ellipsis, _gelu(proj(o2, o3)).astype(BF16)),
              lambda: ga_ref.__setitem__(Ellipsis, _sigmoid(proj(o3, o4)).astype(BF16)),
              lambda: gb_ref.__setitem__(Ellipsis, _sigmoid(proj(o4, o4 + d)).astype(BF16))]
    ntile = wr_ref.shape[0]
    for j in range(max(ntile, len(others))):
        if j < ntile:
            lru_gates(j)
        if j < len(others):
            others[j]()


def _inproj(x, g, perm, w_in_b, lrup, s5w, lruw):
    nb, seq, d = x.shape
    tq = TIME_CHUNK
    rows = nb * tq
    t = nb * seq
    row = lambda c: pl.BlockSpec((rows, c), lambda i: (i, 0))
    return pl.pallas_call(
        functools.partial(_inproj_kernel, s5w=s5w, lruw=lruw),
        grid=(seq // tq,),
        in_specs=[pl.BlockSpec((nb, tq, d), lambda i: (0, i, 0)), _full(g), _full(perm), _full(w_in_b)]
        + [_full(p) for p in lrup],
        out_specs=[row(s5w), row(lruw), row(lruw), row(lruw), row(d), row(d)],
        out_shape=[jax.ShapeDtypeStruct((t, s5w), F32), jax.ShapeDtypeStruct((t, lruw), F32),
                   jax.ShapeDtypeStruct((t, lruw), F32), jax.ShapeDtypeStruct((t, lruw), BF16),
                   jax.ShapeDtypeStruct((t, d), BF16), jax.ShapeDtypeStruct((t, d), BF16)],
        scratch_shapes=[pltpu.VMEM(((CONV_WIDTH - 1) * nb, lruw), F32)],
        compiler_params=_params(),
        name="inproj",
    )(x, g, perm, w_in_b, *lrup)


def _mixers_kernel(u_ref, a_ref, b_ref, gg_ref, ga_ref, gb_ref,
                   bm_ref, lr_ref, li_ref, cm_ref, dk_ref, wo_ref, wlo_ref,
                   o_ref, state_ref, sbuf_ref, h_ref, hbuf_ref, *, tq, nb):
    d = o_ref.shape[-1]
    npair = bm_ref.shape[0]
    kper = LANES // (2 * S5_GROUP)

    @pl.when(pl.program_id(0) == 0)
    def _():
        state_ref[...] = jnp.zeros_like(state_ref)
        h_ref[...] = jnp.zeros_like(h_ref)

    u2 = u_ref[...]
    ub = u2.astype(BF16)
    for k in range(npair):
        kb = k // kper
        bu = jnp.dot(ub[:, LANES * kb:LANES * (kb + 1)], bm_ref[k], preferred_element_type=F32)
        sbuf_ref[2 * k] = bu[:, :LANES]
        sbuf_ref[2 * k + 1] = bu[:, LANES:]

    pairs_per_loop = 4
    for kk in range(npair // pairs_per_loop):
        k0 = kk * pairs_per_loop
        lr = [jnp.broadcast_to(lr_ref[k0 + j:k0 + j + 1, :], (nb, LANES)) for j in range(pairs_per_loop)]
        li = [jnp.broadcast_to(li_ref[k0 + j:k0 + j + 1, :], (nb, LANES)) for j in range(pairs_per_loop)]
        init = tuple(state_ref[2 * k0 + j] for j in range(2 * pairs_per_loop))

        def body(it, carry, k0=k0, lr=lr, li=li):
            carry = list(carry)
            for s in range(SCAN_UNROLL):
                rows = _tile(it * SCAN_UNROLL + s)
                for j in range(pairs_per_loop):
                    sre, sim = carry[2 * j], carry[2 * j + 1]
                    nre = lr[j] * sre - li[j] * sim + sbuf_ref[2 * (k0 + j), rows, :]
                    nim = lr[j] * sim + li[j] * sre + sbuf_ref[2 * (k0 + j) + 1, rows, :]
                    sbuf_ref[2 * (k0 + j), rows, :] = nre
                    sbuf_ref[2 * (k0 + j) + 1, rows, :] = nim
                    carry[2 * j], carry[2 * j + 1] = nre, nim
            return tuple(carry)

        fin = lax.fori_loop(0, tq // SCAN_UNROLL, body, init)
        for j in range(2 * pairs_per_loop):
            state_ref[2 * k0 + j] = fin[j]

    def scan_body(it, h):
        for s in range(SCAN_UNROLL):
            tr = _tile(it * SCAN_UNROLL + s)
            h = a_ref[tr, :] * h + b_ref[tr, :]
            hbuf_ref[tr, :] = h
        return h

    h_ref[...] = lax.fori_loop(0, tq // SCAN_UNROLL, scan_body, h_ref[...])

    ys = []
    for kb in range(npair // kper):
        acc = None
        for j in range(kper):
            k = kb * kper + j
            s = jnp.concatenate([sbuf_ref[2 * k], sbuf_ref[2 * k + 1]], axis=1).astype(BF16)
            p = jnp.dot(s, cm_ref[k], preferred_element_type=F32)
            acc = p if acc is None else acc + p
        ys.append(acc)
    y = jnp.concatenate(ys, axis=1) + dk_ref[...] * u2
    z = jnp.dot(_gelu(y).astype(BF16), wo_ref[...], preferred_element_type=F32)
    ya = z[:, :d] * _sigmoid(z[:, d:])

    yl = hbuf_ref[...] * gg_ref[...].astype(F32)
    yb = jnp.dot(yl.astype(BF16), wlo_ref[...], preferred_element_type=F32)
    o_ref[...] = (ya * ga_ref[...].astype(F32) + yb * gb_ref[...].astype(F32)).astype(BF16)


def _mixers(u, a, b, gg, ga, gb, s5p, w_lru_out_b, nb):
    t, s5w = u.shape
    w = a.shape[-1]
    d = ga.shape[-1]
    tq = TIME_CHUNK
    rows = nb * tq
    nstate = s5p[0].shape[0] * MXU_DIM
    row = lambda c: pl.BlockSpec((rows, c), lambda i: (i, 0))
    params = tuple(s5p) + (w_lru_out_b,)
    return pl.pallas_call(
        functools.partial(_mixers_kernel, tq=tq, nb=nb),
        grid=(t // rows,),
        in_specs=[row(s5w), row(w), row(w), row(w), row(d), row(d)] + [_full(p) for p in params],
        out_specs=row(d),
        out_shape=jax.ShapeDtypeStruct((t, d), BF16),
        scratch_shapes=[pltpu.VMEM((nstate // LANES, nb, LANES), F32),
                        pltpu.VMEM((nstate // LANES, rows, LANES), F32),
                        pltpu.VMEM((nb, w), F32),
                        pltpu.VMEM((rows, w), F32)],
        compiler_params=_params(),
        name="mixers",
    )(u, a, b, gg, ga, gb, *params)


def _mixroute_kernel(x_ref, m_ref, pt_ref, wo_ref, g_ref, wh_ref, wl_ref, rb_ref,
                     x1_ref, h2_ref, lt_ref):
    nb, tq, d = x_ref.shape
    tm = nb * tq
    m = jnp.dot(pt_ref[...], m_ref[...], preferred_element_type=F32).astype(BF16)
    x1 = x_ref[...].reshape(tm, d) + jnp.dot(m, wo_ref[...], preferred_element_type=F32)
    x1_ref[...] = x1
    h = _rms(x1, g_ref[...])

    hh = h.astype(BF16)
    h2_ref[...] = hh
    hl = (h - hh.astype(F32)).astype(BF16)
    dn = (((1,), (1,)), ((), ()))
    lt = (lax.dot_general(wh_ref[...], hh, dn, preferred_element_type=F32)
          + lax.dot_general(wh_ref[...], hl, dn, preferred_element_type=F32)
          + lax.dot_general(wl_ref[...], hh, dn, preferred_element_type=F32))
    lt_ref[...] = lt + rb_ref[:, 0:1]


def _mixroute(x, m, perm_t, w_o_b, g, wr_hi, wr_lo, rbias):
    nb, seq, d = x.shape
    tq = TIME_CHUNK
    tm = nb * tq
    t = nb * seq
    row = lambda c: pl.BlockSpec((tm, c), lambda i: (i, 0))
    return pl.pallas_call(
        _mixroute_kernel,
        grid=(seq // tq,),
        in_specs=[pl.BlockSpec((nb, tq, d), lambda i: (0, i, 0)), row(d), _full(perm_t), _full(w_o_b),
                  _full(g), _full(wr_hi), _full(wr_lo), _full(rbias)],
        out_specs=[row(d), row(d), pl.BlockSpec((ROUTER_ROWS, tm), lambda i: (0, i))],
        out_shape=[jax.ShapeDtypeStruct((t, d), F32), jax.ShapeDtypeStruct((t, d), BF16),
                   jax.ShapeDtypeStruct((ROUTER_ROWS, t), F32)],
        compiler_params=_params(),
        name="mixroute",
    )(x, m, perm_t, w_o_b, g, wr_hi, wr_lo, rbias)


def _route_kernel(lt_ref, tri_ref, ltri_ref, gate_ref, pos_ref, cnt_ref, *, tile):
    lt = lt_ref[...]
    tm = lt.shape[1]
    gl = lt[0:N_GROUPS, :]
    gmax = jnp.max(gl, axis=0, keepdims=True)
    gsum = jnp.sum(jnp.exp(gl - gmax), axis=0, keepdims=True)
    p_g = 1.0 / gsum
    iota_g = lax.broadcasted_iota(jnp.int32, gl.shape, 0)
    g_idx = jnp.min(jnp.where(gl == gmax, iota_g, N_GROUPS), axis=0, keepdims=True)

    el = lt[SUBLANES:SUBLANES + EXPERTS_PER_GROUP, :]
    for g in range(1, N_GROUPS):
        lo = SUBLANES + EXPERTS_PER_GROUP * g
        el = jnp.where(g_idx == g, lt[lo:lo + EXPERTS_PER_GROUP, :], el)
    emax = jnp.max(el, axis=0, keepdims=True)
    esum = jnp.sum(jnp.exp(el - emax), axis=0, keepdims=True)
    iota_e = lax.broadcasted_iota(jnp.int32, el.shape, 0)
    i1 = jnp.min(jnp.where(el == emax, iota_e, EXPERTS_PER_GROUP), axis=0, keepdims=True)
    el2 = jnp.where(iota_e == i1, -jnp.inf, el)
    emax2 = jnp.max(el2, axis=0, keepdims=True)
    i2 = jnp.min(jnp.where(el2 == emax2, iota_e, EXPERTS_PER_GROUP), axis=0, keepdims=True)
    p1 = 1.0 / esum
    p2 = jnp.exp(emax2 - emax) / esum
    psum = p1 + p2
    g1 = p_g * (p1 / psum)
    g2 = p_g * (p2 / psum)
    e1 = g_idx * EXPERTS_PER_GROUP + i1
    e2 = g_idx * EXPERTS_PER_GROUP + i2
    gate_ref[...] = jnp.concatenate([g1, g2, jnp.zeros((SUBLANES - 2, tm), F32)], axis=0)

    dn = (((1,), (1,)), ((), ()))
    iota = lax.broadcasted_iota(jnp.int32, (N_EXPERTS, tile), 0)
    iota_l = lax.broadcasted_iota(jnp.int32, (LANES, tile), 0)
    ones = jnp.ones((SUBLANES, tile), BF16)
    for s in range(tm // tile):
        cs = slice(s * tile, (s + 1) * tile)
        a1, a2 = e1[:, cs], e2[:, cs]
        oh0 = iota == a1
        oh1 = iota == a2
        c = jnp.where(oh0 | oh1, 1.0, 0.0)
        ci = jnp.sum(c, axis=1, keepdims=True).astype(jnp.int32)
        hi = jnp.broadcast_to((ci >> 5).astype(F32), (N_EXPERTS, LANES)).astype(BF16)
        lo = jnp.broadcast_to((ci & 31).astype(F32), (N_EXPERTS, LANES)).astype(BF16)
        first = (32.0 * jnp.dot(ltri_ref[...], hi, preferred_element_type=F32)
                 + jnp.dot(ltri_ref[...], lo, preferred_element_type=F32))
        before = jnp.dot(c.astype(BF16), tri_ref[...], preferred_element_type=F32) + first[:, 0:1]
        p0 = jnp.sum(jnp.where(oh0, before, 0.0), axis=0, keepdims=True)
        p1 = jnp.sum(jnp.where(oh1, before, 0.0), axis=0, keepdims=True)
        pos_ref[:, cs] = jnp.concatenate([p0, p1, jnp.zeros((SUBLANES - 2, tile), F32)], axis=0)
        hits = jnp.where((iota_l == a1) | (iota_l == a2), 1.0, 0.0).astype(BF16)
        cnt_ref[SUBLANES * s:SUBLANES * (s + 1), :] = lax.dot_general(ones, hits, dn, preferred_element_type=F32)


def _route(lt, tri, ltri):
    t = lt.shape[1]
    tile = tri.shape[0]
    tm = min(t, ROUTE_WIDTH)
    col = pl.BlockSpec((SUBLANES, tm), lambda i: (0, i))
    return pl.pallas_call(
        functools.partial(_route_kernel, tile=tile),
        grid=(t // tm,),
        in_specs=[pl.BlockSpec((ROUTER_ROWS, tm), lambda i: (0, i)), _full(tri), _full(ltri)],
        out_specs=[col, col, pl.BlockSpec((tm // tile * SUBLANES, LANES), lambda i: (i, 0))],
        out_shape=[jax.ShapeDtypeStruct((SUBLANES, t), F32), jax.ShapeDtypeStruct((SUBLANES, t), F32),
                   jax.ShapeDtypeStruct((t // tile * SUBLANES, LANES), F32)],
        compiler_params=_params(),
        name="route",
    )(lt, tri, ltri)


def _tokens(ref, tok, n):
    return ref.at[pl.ds(pl.multiple_of(tok * SUBLANES, SUBLANES), n * SUBLANES)]


def _strip_copies(n, src, src_tok, dst, dst_tok, sem, max_tokens, wait=False):
    del max_tokens

    @pl.when(n > 0)
    def _():
        cp = pltpu.make_async_copy(_tokens(src, src_tok, n), _tokens(dst, dst_tok, n), sem)
        if wait:
            cp.wait()
        else:
            cp.start()


def _dispatch_kernel(cnt_ref, off_ref, gst_ref, tot_ref, pst_ref, pen_ref,
                     h2_ref, pos_ref, xs_ref,
                     stage0_ref, stage1_ref, zero_ref, sem, zsem):
    i = pl.program_id(0)
    last = pl.num_programs(0) - 1
    tm = pos_ref.shape[1]
    na = 2 * tm
    d = h2_ref.shape[1]

    p0i = pos_ref[0:1, :].astype(jnp.int32)
    p1i = pos_ref[1:2, :].astype(jnp.int32)
    rid = lax.broadcasted_iota(jnp.int32, (na, tm), 0)
    onehot = jnp.where((rid == p0i) | (rid == p1i), 1.0, 0.0).astype(BF16)
    srt = jnp.dot(onehot, h2_ref[...], preferred_element_type=F32)

    def step(stage, s):
        @pl.when(i >= 2)
        def _():
            pltpu.make_async_copy(stage, _tokens(xs_ref, 0, na), sem.at[s]).wait()

        for j in range(d // LANES):
            stage[pl.ds(j, na, stride=d // LANES), :] = srt[:, LANES * j:LANES * (j + 1)]

        def per_expert(e, c):
            k = i * N_EXPERTS + e
            _strip_copies(cnt_ref[k], stage, off_ref[k], xs_ref, gst_ref[k], sem.at[s], na)
            return c

        lax.fori_loop(0, N_EXPERTS, per_expert, 0)

    @pl.when(i % 2 == 0)
    def _():
        step(stage0_ref, 0)

    @pl.when(i % 2 == 1)
    def _():
        step(stage1_ref, 1)

    @pl.when(i == last)
    def _():
        zero_ref[...] = jnp.zeros_like(zero_ref)
        for wait in (False, True):
            for e in range(N_EXPERTS):
                _strip_copies(pen_ref[e] - pst_ref[e] - tot_ref[e], zero_ref, 0, xs_ref,
                              pst_ref[e] + tot_ref[e], zsem, SLOT_BLOCK - 1, wait=wait)

        nblk = xs_ref.shape[0] // zero_ref.shape[0]

        def bcopy(b):
            return pltpu.make_async_copy(zero_ref, _tokens(xs_ref, b * SLOT_BLOCK, SLOT_BLOCK), zsem)

        first_unused = pen_ref[N_EXPERTS - 1] // SLOT_BLOCK
        lax.fori_loop(first_unused, nblk, lambda b, c: (bcopy(b).start(), c)[1], 0)
        lax.fori_loop(first_unused, nblk, lambda b, c: (bcopy(b).wait(), c)[1], 0)

        @pl.when(i >= 1)
        def _():
            @pl.when(i % 2 == 0)
            def _():
                pltpu.make_async_copy(stage1_ref, _tokens(xs_ref, 0, na), sem.at[1]).wait()

            @pl.when(i % 2 == 1)
            def _():
                pltpu.make_async_copy(stage0_ref, _tokens(xs_ref, 0, na), sem.at[0]).wait()

        @pl.when(i % 2 == 0)
        def _():
            pltpu.make_async_copy(stage0_ref, _tokens(xs_ref, 0, na), sem.at[0]).wait()

        @pl.when(i % 2 == 1)
        def _():
            pltpu.make_async_copy(stage1_ref, _tokens(xs_ref, 0, na), sem.at[1]).wait()


def _dispatch(cnt_t, off_t, gst_t, tot, pst, pen, h2, pos, tm, n_slots):
    sub = SUBLANES
    t, d = h2.shape
    grid_spec = pltpu.PrefetchScalarGridSpec(
        num_scalar_prefetch=6,
        grid=(t // tm,),
        in_specs=[pl.BlockSpec((tm, d), lambda i, *_: (i, 0)),
                  pl.BlockSpec((sub, tm), lambda i, *_: (0, i))],
        out_specs=pl.BlockSpec(memory_space=pl.ANY),
        scratch_shapes=[pltpu.VMEM((2 * tm * sub, LANES), F32), pltpu.VMEM((2 * tm * sub, LANES), F32),
                        pltpu.VMEM((SLOT_BLOCK * sub, LANES), F32), pltpu.SemaphoreType.DMA((2,)),
                        pltpu.SemaphoreType.DMA(())],
    )
    return pl.pallas_call(
        _dispatch_kernel,
        grid_spec=grid_spec,
        out_shape=jax.ShapeDtypeStruct((n_slots * sub, LANES), F32),
        compiler_params=_params(),
        name="dispatch",
    )(cnt_t, off_t, gst_t, tot, pst, pen, h2, pos)


def _ffn_kernel(be_ref, nu_ref, xs_ref, wg_ref, wu_ref, wd_ref, ys_ref, wgb_ref, wub_ref, wdb_ref, *, blk):
    i = pl.program_id(0)
    sub = SUBLANES

    @pl.when(i < nu_ref[0])
    def _():
        prev = be_ref[jnp.maximum(i - 1, 0)]

        @pl.when((i == 0) | (be_ref[i] != prev))
        def _():
            wgb_ref[...] = wg_ref[...].astype(BF16)
            wub_ref[...] = wu_ref[...].astype(BF16)
            wdb_ref[...] = wd_ref[...].astype(BF16)

        x = jnp.concatenate([xs_ref[pl.ds(j, blk, stride=sub), :] for j in range(sub)], axis=1).astype(BF16)
        g = jnp.dot(x, wgb_ref[...], preferred_element_type=F32)
        u = jnp.dot(x, wub_ref[...], preferred_element_type=F32)
        a = (g * _sigmoid(g)) * u
        y = jnp.dot(a.astype(BF16), wdb_ref[...], preferred_element_type=F32)
        for j in range(sub):
            ys_ref[pl.ds(j, blk, stride=sub), :] = y[:, LANES * j:LANES * (j + 1)]

    @pl.when(i >= nu_ref[0])
    def _():
        ys_ref[...] = jnp.zeros_like(ys_ref)


def _ffn(block_e, n_used, xs, wg, wu, wd):
    sub = SUBLANES
    blk = SLOT_BLOCK
    n_blocks = xs.shape[0] // (sub * blk)
    ne, d, de = wg.shape

    def slot_map(i, be, nu):
        return (i, 0)

    def w_map(i, be, nu):
        return (be[jnp.minimum(i, nu[0] - 1)], 0, 0)

    grid_spec = pltpu.PrefetchScalarGridSpec(
        num_scalar_prefetch=2,
        grid=(n_blocks,),
        in_specs=[pl.BlockSpec((blk * sub, LANES), slot_map),
                  pl.BlockSpec((None, d, de), w_map), pl.BlockSpec((None, d, de), w_map),
                  pl.BlockSpec((None, de, d), w_map)],
        out_specs=pl.BlockSpec((blk * sub, LANES), slot_map),
        scratch_shapes=[pltpu.VMEM((d, de), BF16), pltpu.VMEM((d, de), BF16), pltpu.VMEM((de, d), BF16)],
    )
    return pl.pallas_call(
        functools.partial(_ffn_kernel, blk=blk),
        grid_spec=grid_spec,
        out_shape=jax.ShapeDtypeStruct(xs.shape, F32),
        compiler_params=_params(),
        name="ffn",
    )(block_e, n_used, xs, wg, wu, wd)


def _combine_kernel(cnt_ref, off_ref, gst_ref, x1_ref, pos_ref, gate_ref, ys_ref, g_ref, o_ref,
                    buf0_ref, buf1_ref, sem):
    i = pl.program_id(0)
    n = pl.num_programs(0)
    tm, d = x1_ref.shape
    na = 2 * tm
    sub = SUBLANES

    def fetch(tile, buf, s):
        def per_expert(e, c):
            k = tile * N_EXPERTS + e
            _strip_copies(cnt_ref[k], ys_ref, gst_ref[k], buf, off_ref[k], sem.at[s], na)
            return c

        lax.fori_loop(0, N_EXPERTS, per_expert, 0)

    @pl.when(i == 0)
    def _():
        fetch(0, buf0_ref, 0)

    @pl.when((i + 1 < n) & (i % 2 == 0))
    def _():
        fetch(i + 1, buf1_ref, 1)

    @pl.when((i + 1 < n) & (i % 2 == 1))
    def _():
        fetch(i + 1, buf0_ref, 0)

    prow = jnp.concatenate([pos_ref[0:2, :], jnp.zeros((sub - 2, tm), F32)], axis=0)
    pcol = jnp.concatenate([prow] * (LANES // sub), axis=0).T.astype(jnp.int32)
    lane = lax.broadcasted_iota(jnp.int32, (tm, na), 1)
    pick = jnp.where((lane == pcol[:, 0:1]) | (lane == pcol[:, 1:2]), 1.0, 0.0).astype(BF16)
    rid = lax.broadcasted_iota(jnp.int32, (na, tm), 0)
    p0 = pos_ref[0:1, :].astype(jnp.int32)
    p1 = pos_ref[1:2, :].astype(jnp.int32)
    grow = jnp.sum(jnp.where(rid == p0, gate_ref[0:1, :], 0.0) + jnp.where(rid == p1, gate_ref[1:2, :], 0.0),
                   axis=1, keepdims=True)

    def finish(buf, s):
        pltpu.make_async_copy(_tokens(ys_ref, 0, na), buf, sem.at[s]).wait()
        y = jnp.concatenate([buf[pl.ds(j, na, stride=sub), :] for j in range(d // LANES)], axis=1)
        moe = jnp.dot(pick, (y * grow).astype(BF16), preferred_element_type=F32)
        x = x1_ref[...] + moe
        o_ref[...] = _rms(x, g_ref[...]).reshape(o_ref.shape)

    @pl.when(i % 2 == 0)
    def _():
        finish(buf0_ref, 0)

    @pl.when(i % 2 == 1)
    def _():
        finish(buf1_ref, 1)


def _combine(cnt_t, off_t, gst_t, x1, pos, gate, ys, g, nb, seq):
    t, d = x1.shape
    tq = TIME_CHUNK
    tm = nb * tq
    sub = SUBLANES
    grid_spec = pltpu.PrefetchScalarGridSpec(
        num_scalar_prefetch=3,
        grid=(t // tm,),
        in_specs=[pl.BlockSpec((tm, d), lambda i, *_: (i, 0)),
                  pl.BlockSpec((sub, tm), lambda i, *_: (0, i)),
                  pl.BlockSpec((sub, tm), lambda i, *_: (0, i)),
                  pl.BlockSpec(memory_space=pl.ANY),
                  _full(g)],
        out_specs=pl.BlockSpec((nb, tq, d), lambda i, *_: (0, i, 0)),
        scratch_shapes=[pltpu.VMEM((2 * tm * sub, LANES), F32), pltpu.VMEM((2 * tm * sub, LANES), F32),
                        pltpu.SemaphoreType.DMA((2,))],
    )
    return pl.pallas_call(
        _combine_kernel,
        grid_spec=grid_spec,
        out_shape=jax.ShapeDtypeStruct((nb, seq, d), F32),
        compiler_params=_params(),
        name="combine",
    )(cnt_t, off_t, gst_t, x1, pos, gate, ys, g)


def _s5_tables(lam_re, lam_im, log_dt, b_re, b_im, c_re, c_im):
    ng, p = lam_re.shape
    npair = ng // 2
    kper = LANES // (2 * S5_GROUP)
    lam = lax.complex(lam_re, lam_im)
    dt = jnp.exp(log_dt)[:, None]
    lam_bar = jnp.exp(lam * dt)
    b_bar = ((lam_bar - 1.0) / lam)[..., None] * lax.complex(b_re, b_im)
    lr = jnp.real(lam_bar).reshape(npair, 2 * p)
    li = jnp.imag(lam_bar).reshape(npair, 2 * p)
    eye2 = jnp.eye(2, dtype=F32)
    sel = jax.nn.one_hot(jnp.arange(npair) % kper, kper, dtype=F32)

    bv = jnp.stack([jnp.real(b_bar), jnp.imag(b_bar)])
    bv = bv.reshape(2, npair, 2, p, S5_GROUP).transpose(1, 2, 4, 0, 3)
    bblk = bv[:, :, :, :, None, :] * eye2[None, :, None, None, :, None]
    bblk = bblk.reshape(npair, 2 * S5_GROUP, 4 * p)
    bm = (sel[:, :, None, None] * bblk[:, None]).reshape(npair, LANES, 4 * p)

    cv = jnp.stack([c_re, -c_im])
    cv = cv.reshape(2, npair, 2, S5_GROUP, p).transpose(1, 0, 2, 4, 3)
    cblk = cv[:, :, :, :, None, :] * eye2[None, None, :, None, :, None]
    cblk = cblk.reshape(npair, 4 * p, 2 * S5_GROUP)
    cm = (cblk[:, :, None, :] * sel[:, None, :, None]).reshape(npair, 4 * p, LANES)
    return bm.astype(BF16), lr, li, cm.astype(BF16)


def _blockdiag_tiles(w):
    nh, hi, ho = w.shape
    per = MXU_DIM // hi
    eye = jnp.eye(per, dtype=w.dtype)
    t = w.reshape(nh // per, per, hi, ho)[:, :, :, None, :] * eye[None, :, None, :, None]
    return t.reshape(nh // per, per * hi, per * ho)


def kernel(x, norm_mix, w_in, s5_lam_re, s5_lam_im, s5_log_dt, s5_b_re, s5_b_im, s5_c_re, s5_c_im, s5_d,
           w_s5_out, conv_w, conv_b, lru_w_r, lru_b_r, lru_w_i, lru_b_i, lru_lambda, w_lru_out, w_o, norm_ffn,
           router_group_w, router_group_b, router_expert_w, router_expert_b, expert_w_gate, expert_w_up,
           expert_w_down, norm_final):
    nb, seq, d = x.shape
    t = nb * seq
    assert w_in.shape[0] == 1, "one layer: the final RMSNorm is fused into the layer's combine kernel"
    l = 0
    s5w = s5_d.shape[-1]
    lruw = conv_b.shape[-1]
    rows = nb * TIME_CHUNK

    r = jnp.arange(rows)
    perm = jax.nn.one_hot((r % nb) * TIME_CHUNK + r // nb, rows, dtype=BF16)

    sp = jax.nn.softplus(-lru_lambda[l])[None, :]
    lrup = (conv_w[l], conv_b[l][None, :], _blockdiag_tiles(lru_w_r[l]).astype(BF16),
            _blockdiag_tiles(lru_w_i[l]).astype(BF16), lru_b_r[l][None, :], lru_b_i[l][None, :], sp)
    u, a, b, gg, ga, gb = _inproj(x, norm_mix[l][None, :], perm, w_in[l].astype(BF16), lrup, s5w, lruw)

    bm, lr, li, cm = _s5_tables(s5_lam_re[l], s5_lam_im[l], s5_log_dt[l], s5_b_re[l], s5_b_im[l],
                                s5_c_re[l], s5_c_im[l])
    s5p = (bm, lr, li, cm, s5_d[l][None, :], w_s5_out[l].astype(BF16))
    mixed = _mixers(u, a, b, gg, ga, gb, s5p, w_lru_out[l].astype(BF16), nb)

    gap = jnp.zeros((SUBLANES - N_GROUPS, d), F32)
    wrt = jnp.concatenate([router_group_w[l].T, gap, router_expert_w[l].T], axis=0)
    wr_hi = wrt.astype(BF16)
    wr_lo = (wrt - wr_hi.astype(F32)).astype(BF16)
    rb = jnp.concatenate([router_group_b[l], jnp.zeros((SUBLANES - N_GROUPS,), F32), router_expert_b[l]])
    rbias = jnp.broadcast_to(rb[:, None], (ROUTER_ROWS, LANES))

    x1, h2, logits = _mixroute(x, mixed, perm.T, w_o[l].astype(BF16), norm_ffn[l][None, :], wr_hi, wr_lo, rbias)
    tri = jnp.triu(jnp.ones((rows, rows), BF16), k=1)
    ltri = jnp.tril(jnp.ones((N_EXPERTS, N_EXPERTS), BF16), k=-1)
    gate, pos, tcnt = _route(logits, tri, ltri)

    n_tiles = seq // TIME_CHUNK
    cnt_t = tcnt.reshape(n_tiles, SUBLANES, LANES)[:, 0, :N_EXPERTS].astype(jnp.int32)
    off_t = jnp.cumsum(cnt_t, axis=1) - cnt_t
    counts = jnp.sum(cnt_t, axis=0)
    padded = ((counts + SLOT_BLOCK - 1) // SLOT_BLOCK) * SLOT_BLOCK
    pad_ends = jnp.cumsum(padded)
    pad_starts = pad_ends - padded
    gst_t = pad_starts[None, :] + jnp.cumsum(cnt_t, axis=0) - cnt_t
    n_blocks = -(-(2 * t) // SLOT_BLOCK) + N_EXPERTS
    n_slots = n_blocks * SLOT_BLOCK
    block_start = jnp.arange(n_blocks, dtype=jnp.int32) * SLOT_BLOCK
    block_e = jnp.minimum(jnp.sum((pad_ends[None, :] <= block_start[:, None]).astype(jnp.int32), axis=1),
                          N_EXPERTS - 1)
    n_used = pad_ends[-1:] // SLOT_BLOCK
    cnt_f, off_f, gst_f = cnt_t.reshape(-1), off_t.reshape(-1), gst_t.reshape(-1)

    xs = _dispatch(cnt_f, off_f, gst_f, counts, pad_starts, pad_ends, h2, pos, rows, n_slots)
    ys = _ffn(block_e, n_used, xs, expert_w_gate[l], expert_w_up[l], expert_w_down[l])
    return _combine(cnt_f, off_f, gst_f, x1, pos, gate, ys, norm_final[None, :], nb, seq)
```

```python
import functools
import math

import jax
import jax.numpy as jnp
from jax import lax
from jax.experimental import pallas as pl
from jax.experimental.pallas import tpu as pltpu

F32 = jnp.float32
BF16 = jnp.bfloat16
U32 = jnp.uint32

RMS_EPS = 1e-6
S5_GROUP = 16
CONV_WIDTH = 4
LRU_C = 8.0
N_GROUPS = 4
EXPERTS_PER_GROUP = 8
N_EXPERTS = N_GROUPS * EXPERTS_PER_GROUP

LANES = 128
SUBLANES = 8
TOKEN_ROWS = 4
MXU_DIM = 256
TIME_CHUNK = 64
SCAN_UNROLL = 4
SLOT_BLOCK = 512
ROUTE_WIDTH = 4096
ROUTER_ROWS = 40
VMEM_LIMIT = 56 * 1024 * 1024


def _gelu(x):
    c = math.sqrt(2.0 / math.pi)
    return 0.5 * x * (1.0 + jnp.tanh(c * (x + 0.044715 * (x * x * x))))


def _sigmoid(x):
    return 0.5 * jnp.tanh(0.5 * x) + 0.5


def _rms(x, g):
    return x * lax.rsqrt(jnp.mean(x * x, axis=-1, keepdims=True) + RMS_EPS) * g


def _params():
    return pltpu.CompilerParams(dimension_semantics=("arbitrary",), vmem_limit_bytes=VMEM_LIMIT)


def _full(a):
    return pl.BlockSpec(a.shape, lambda i, *_: (0,) * a.ndim)


def _tile(t):
    return pl.ds(pl.multiple_of(t * SUBLANES, SUBLANES), SUBLANES)


def _inproj_kernel(x_ref, g_ref, p_ref, w_ref, cw_ref, cb_ref, wr_ref, wi_ref, br_ref, bi_ref, sp_ref,
                   u_ref, a_ref, b_ref, gg_ref, ga_ref, gb_ref, xprev_ref, *, s5w, lruw):
    nb, tq, d = x_ref.shape
    rows = nb * tq
    halo = (CONV_WIDTH - 1) * nb

    @pl.when(pl.program_id(0) == 0)
    def _():
        xprev_ref[...] = jnp.zeros_like(xprev_ref)

    h = _rms(x_ref[...].reshape(rows, d), g_ref[...])
    hb = jnp.dot(p_ref[...], h.astype(BF16), preferred_element_type=F32).astype(BF16)

    def proj(lo, hi):
        return jnp.dot(hb, w_ref[:, lo:hi], preferred_element_type=F32)

    o1 = s5w
    o2 = o1 + lruw
    o3 = o2 + lruw
    o4 = o3 + d
    xin = proj(o1, o2)
    xext = jnp.concatenate([xprev_ref[...], xin], axis=0)
    xprev_ref[...] = xin[rows - halo:, :]
    xc_all = cb_ref[...]
    for k in range(CONV_WIDTH):
        xc_all = xc_all + xext[k * nb:k * nb + rows, :] * cw_ref[k:k + 1, :]

    def lru_gates(j):
        cs = slice(MXU_DIM * j, MXU_DIM * (j + 1))
        xc = xc_all[:, cs]
        xcb = xc.astype(BF16)
        r = _sigmoid(jnp.dot(xcb, wr_ref[j], preferred_element_type=F32) + br_ref[:, cs])
        ig = _sigmoid(jnp.dot(xcb, wi_ref[j], preferred_element_type=F32) + bi_ref[:, cs])
        log_a = (-LRU_C) * r * sp_ref[:, cs]
        th = jnp.tanh(log_a)
        mult = jnp.sqrt((-2.0 * th) / (1.0 - th))
        a_ref[:, cs] = jnp.exp(log_a)
        b_ref[:, cs] = mult * (ig * xc)

    others = [lambda: u_ref.__setitem__(Ellipsis, proj(0, o1)),
              lambda: gg_ref.__setitem__(Ellipsis, _gelu(proj(o2, o3)).astype(BF16)),
              lambda: ga_ref.__setitem__(Ellipsis, _sigmoid(proj(o3, o4)).astype(BF16)),
              lambda: gb_ref.__setitem__(Ellipsis, _sigmoid(proj(o4, o4 + d)).astype(BF16))]
    ntile = wr_ref.shape[0]
    for j in range(max(ntile, len(others))):
        if j < ntile:
            lru_gates(j)
        if j < len(others):
            others[j]()


def _inproj(x, g, perm, w_in_b, lrup, s5w, lruw):
    nb, seq, d = x.shape
    tq = TIME_CHUNK
    rows = nb * tq
    t = nb * seq
    row = lambda c: pl.BlockSpec((rows, c), lambda i: (i, 0))
    return pl.pallas_call(
        functools.partial(_inproj_kernel, s5w=s5w, lruw=lruw),
        grid=(seq // tq,),
        in_specs=[pl.BlockSpec((nb, tq, d), lambda i: (0, i, 0)), _full(g), _full(perm), _full(w_in_b)]
        + [_full(p) for p in lrup],
        out_specs=[row(s5w), row(lruw), row(lruw), row(lruw), row(d), row(d)],
        out_shape=[jax.ShapeDtypeStruct((t, s5w), F32), jax.ShapeDtypeStruct((t, lruw), F32),
                   jax.ShapeDtypeStruct((t, lruw), F32), jax.ShapeDtypeStruct((t, lruw), BF16),
                   jax.ShapeDtypeStruct((t, d), BF16), jax.ShapeDtypeStruct((t, d), BF16)],
        scratch_shapes=[pltpu.VMEM(((CONV_WIDTH - 1) * nb, lruw), F32)],
        compiler_params=_params(),
        name="inproj",
    )(x, g, perm, w_in_b, *lrup)


def _mixers_kernel(u_ref, a_ref, b_ref, gg_ref, ga_ref, gb_ref,
                   bm_ref, lr_ref, li_ref, cm_ref, dk_ref, wo_ref, wlo_ref,
                   o_ref, state_ref, sbuf_ref, h_ref, hbuf_ref, *, tq, nb):
    d = o_ref.shape[-1]
    npair = bm_ref.shape[0]
    kper = LANES // (2 * S5_GROUP)

    @pl.when(pl.program_id(0) == 0)
    def _():
        state_ref[...] = jnp.zeros_like(state_ref)
        h_ref[...] = jnp.zeros_like(h_ref)

    u2 = u_ref[...]
    ub = u2.astype(BF16)
    for k in range(npair):
        kb = k // kper
        bu = jnp.dot(ub[:, LANES * kb:LANES * (kb + 1)], bm_ref[k], preferred_element_type=F32)
        sbuf_ref[2 * k] = bu[:, :LANES]
        sbuf_ref[2 * k + 1] = bu[:, LANES:]

    pairs_per_loop = 4
    for kk in range(npair // pairs_per_loop):
        k0 = kk * pairs_per_loop
        lr = [jnp.broadcast_to(lr_ref[k0 + j:k0 + j + 1, :], (nb, LANES)) for j in range(pairs_per_loop)]
        li = [jnp.broadcast_to(li_ref[k0 + j:k0 + j + 1, :], (nb, LANES)) for j in range(pairs_per_loop)]
        init = tuple(state_ref[2 * k0 + j] for j in range(2 * pairs_per_loop))

        def body(it, carry, k0=k0, lr=lr, li=li):
            carry = list(carry)
            for s in range(SCAN_UNROLL):
                rows = _tile(it * SCAN_UNROLL + s)
                for j in range(pairs_per_loop):
                    sre, sim = carry[2 * j], carry[2 * j + 1]
                    nre = lr[j] * sre - li[j] * sim + sbuf_ref[2 * (k0 + j), rows, :]
                    nim = lr[j] * sim + li[j] * sre + sbuf_ref[2 * (k0 + j) + 1, rows, :]
                    sbuf_ref[2 * (k0 + j), rows, :] = nre
                    sbuf_ref[2 * (k0 + j) + 1, rows, :] = nim
                    carry[2 * j], carry[2 * j + 1] = nre, nim
            return tuple(carry)

        fin = lax.fori_loop(0, tq // SCAN_UNROLL, body, init)
        for j in range(2 * pairs_per_loop):
            state_ref[2 * k0 + j] = fin[j]

    def scan_body(it, h):
        for s in range(SCAN_UNROLL):
            tr = _tile(it * SCAN_UNROLL + s)
            h = a_ref[tr, :] * h + b_ref[tr, :]
            hbuf_ref[tr, :] = h
        return h

    h_ref[...] = lax.fori_loop(0, tq // SCAN_UNROLL, scan_body, h_ref[...])

    ys = []
    for kb in range(npair // kper):
        acc = None
        for j in range(kper):
            k = kb * kper + j
            s = jnp.concatenate([sbuf_ref[2 * k], sbuf_ref[2 * k + 1]], axis=1).astype(BF16)
            p = jnp.dot(s, cm_ref[k], preferred_element_type=F32)
            acc = p if acc is None else acc + p
        ys.append(acc)
    y = jnp.concatenate(ys, axis=1) + dk_ref[...] * u2
    z = jnp.dot(_gelu(y).astype(BF16), wo_ref[...], preferred_element_type=F32)
    ya = z[:, :d] * _sigmoid(z[:, d:])

    yl = hbuf_ref[...] * gg_ref[...].astype(F32)
    yb = jnp.dot(yl.astype(BF16), wlo_ref[...], preferred_element_type=F32)
    o_ref[...] = (ya * ga_ref[...].astype(F32) + yb * gb_ref[...].astype(F32)).astype(BF16)


def _mixers(u, a, b, gg, ga, gb, s5p, w_lru_out_b, nb):
    t, s5w = u.shape
    w = a.shape[-1]
    d = ga.shape[-1]
    tq = TIME_CHUNK
    rows = nb * tq
    nstate = s5p[0].shape[0] * MXU_DIM
    row = lambda c: pl.BlockSpec((rows, c), lambda i: (i, 0))
    params = tuple(s5p) + (w_lru_out_b,)
    return pl.pallas_call(
        functools.partial(_mixers_kernel, tq=tq, nb=nb),
        grid=(t // rows,),
        in_specs=[row(s5w), row(w), row(w), row(w), row(d), row(d)] + [_full(p) for p in params],
        out_specs=row(d),
        out_shape=jax.ShapeDtypeStruct((t, d), BF16),
        scratch_shapes=[pltpu.VMEM((nstate // LANES, nb, LANES), F32),
                        pltpu.VMEM((nstate // LANES, rows, LANES), F32),
                        pltpu.VMEM((nb, w), F32),
                        pltpu.VMEM((rows, w), F32)],
        compiler_params=_params(),
        name="mixers",
    )(u, a, b, gg, ga, gb, *params)


def _mixroute_kernel(x_ref, m_ref, pt_ref, wo_ref, g_ref, wh_ref, wl_ref, rb_ref,
                     x1_ref, h2_ref, lt_ref):
    nb, tq, d = x_ref.shape
    tm = nb * tq
    m = jnp.dot(pt_ref[...], m_ref[...], preferred_element_type=F32).astype(BF16)
    x1 = x_ref[...].reshape(tm, d) + jnp.dot(m, wo_ref[...], preferred_element_type=F32)
    x1_ref[...] = x1
    h = _rms(x1, g_ref[...])

    hh = h.astype(BF16)
    h2_ref[...] = hh
    hl = (h - hh.astype(F32)).astype(BF16)
    dn = (((1,), (1,)), ((), ()))
    lt = (lax.dot_general(wh_ref[...], hh, dn, preferred_element_type=F32)
          + lax.dot_general(wh_ref[...], hl, dn, preferred_element_type=F32)
          + lax.dot_general(wl_ref[...], hh, dn, preferred_element_type=F32))
    lt_ref[...] = lt + rb_ref[:, 0:1]


def _mixroute(x, m, perm_t, w_o_b, g, wr_hi, wr_lo, rbias):
    nb, seq, d = x.shape
    tq = TIME_CHUNK
    tm = nb * tq
    t = nb * seq
    row = lambda c: pl.BlockSpec((tm, c), lambda i: (i, 0))
    return pl.pallas_call(
        _mixroute_kernel,
        grid=(seq // tq,),
        in_specs=[pl.BlockSpec((nb, tq, d), lambda i: (0, i, 0)), row(d), _full(perm_t), _full(w_o_b),
                  _full(g), _full(wr_hi), _full(wr_lo), _full(rbias)],
        out_specs=[row(d), row(d), pl.BlockSpec((ROUTER_ROWS, tm), lambda i: (0, i))],
        out_shape=[jax.ShapeDtypeStruct((t, d), F32), jax.ShapeDtypeStruct((t, d), BF16),
                   jax.ShapeDtypeStruct((ROUTER_ROWS, t), F32)],
        compiler_params=_params(),
        name="mixroute",
    )(x, m, perm_t, w_o_b, g, wr_hi, wr_lo, rbias)


def _route_kernel(lt_ref, tri_ref, ltri_ref, gate_ref, pos_ref, cnt_ref, *, tile):
    lt = lt_ref[...]
    tm = lt.shape[1]
    gl = lt[0:N_GROUPS, :]
    gmax = jnp.max(gl, axis=0, keepdims=True)
    gsum = jnp.sum(jnp.exp(gl - gmax), axis=0, keepdims=True)
    p_g = 1.0 / gsum
    iota_g = lax.broadcasted_iota(jnp.int32, gl.shape, 0)
    g_idx = jnp.min(jnp.where(gl == gmax, iota_g, N_GROUPS), axis=0, keepdims=True)

    el = lt[SUBLANES:SUBLANES + EXPERTS_PER_GROUP, :]
    for g in range(1, N_GROUPS):
        lo = SUBLANES + EXPERTS_PER_GROUP * g
        el = jnp.where(g_idx == g, lt[lo:lo + EXPERTS_PER_GROUP, :], el)
    emax = jnp.max(el, axis=0, keepdims=True)
    esum = jnp.sum(jnp.exp(el - emax), axis=0, keepdims=True)
    iota_e = lax.broadcasted_iota(jnp.int32, el.shape, 0)
    i1 = jnp.min(jnp.where(el == emax, iota_e, EXPERTS_PER_GROUP), axis=0, keepdims=True)
    el2 = jnp.where(iota_e == i1, -jnp.inf, el)
    emax2 = jnp.max(el2, axis=0, keepdims=True)
    i2 = jnp.min(jnp.where(el2 == emax2, iota_e, EXPERTS_PER_GROUP), axis=0, keepdims=True)
    p1 = 1.0 / esum
    p2 = jnp.exp(emax2 - emax) / esum
    psum = p1 + p2
    g1 = p_g * (p1 / psum)
    g2 = p_g * (p2 / psum)
    e1 = g_idx * EXPERTS_PER_GROUP + i1
    e2 = g_idx * EXPERTS_PER_GROUP + i2
    gate_ref[...] = jnp.concatenate([g1, g2, jnp.zeros((SUBLANES - 2, tm), F32)], axis=0)

    dn = (((1,), (1,)), ((), ()))
    iota = lax.broadcasted_iota(jnp.int32, (N_EXPERTS, tile), 0)
    iota_l = lax.broadcasted_iota(jnp.int32, (LANES, tile), 0)
    ones = jnp.ones((SUBLANES, tile), BF16)
    for s in range(tm // tile):
        cs = slice(s * tile, (s + 1) * tile)
        a1, a2 = e1[:, cs], e2[:, cs]
        oh0 = iota == a1
        oh1 = iota == a2
        c = jnp.where(oh0 | oh1, 1.0, 0.0)
        ci = jnp.sum(c, axis=1, keepdims=True).astype(jnp.int32)
        hi = jnp.broadcast_to((ci >> 5).astype(F32), (N_EXPERTS, LANES)).astype(BF16)
        lo = jnp.broadcast_to((ci & 31).astype(F32), (N_EXPERTS, LANES)).astype(BF16)
        first = (32.0 * jnp.dot(ltri_ref[...], hi, preferred_element_type=F32)
                 + jnp.dot(ltri_ref[...], lo, preferred_element_type=F32))
        before = jnp.dot(c.astype(BF16), tri_ref[...], preferred_element_type=F32) + first[:, 0:1]
        p0 = jnp.sum(jnp.where(oh0, before, 0.0), axis=0, keepdims=True)
        p1 = jnp.sum(jnp.where(oh1, before, 0.0), axis=0, keepdims=True)
        pos_ref[:, cs] = jnp.concatenate([p0, p1, jnp.zeros((SUBLANES - 2, tile), F32)], axis=0)
        hits = jnp.where((iota_l == a1) | (iota_l == a2), 1.0, 0.0).astype(BF16)
        cnt_ref[SUBLANES * s:SUBLANES * (s + 1), :] = lax.dot_general(ones, hits, dn, preferred_element_type=F32)


def _route(lt, tri, ltri):
    t = lt.shape[1]
    tile = tri.shape[0]
    tm = min(t, ROUTE_WIDTH)
    col = pl.BlockSpec((SUBLANES, tm), lambda i: (0, i))
    return pl.pallas_call(
        functools.partial(_route_kernel, tile=tile),
        grid=(t // tm,),
        in_specs=[pl.BlockSpec((ROUTER_ROWS, tm), lambda i: (0, i)), _full(tri), _full(ltri)],
        out_specs=[col, col, pl.BlockSpec((tm // tile * SUBLANES, LANES), lambda i: (i, 0))],
        out_shape=[jax.ShapeDtypeStruct((SUBLANES, t), F32), jax.ShapeDtypeStruct((SUBLANES, t), F32),
                   jax.ShapeDtypeStruct((t // tile * SUBLANES, LANES), F32)],
        compiler_params=_params(),
        name="route",
    )(lt, tri, ltri)


def _tokens(ref, tok, n):
    return ref.at[pl.ds(pl.multiple_of(tok * TOKEN_ROWS, TOKEN_ROWS), n * TOKEN_ROWS)]


def _pack_tokens(ref, v):
    n, d = v.shape
    half = d // 2
    bits = lax.bitcast_convert_type(v, U32)
    w = (bits[:, :half] >> 16) | (bits[:, half:] & jnp.uint32(0xFFFF0000))
    for j in range(TOKEN_ROWS):
        ref[pl.ds(j, n, stride=TOKEN_ROWS), :] = w[:, LANES * j:LANES * (j + 1)]


def _unpack_tokens(ref, n):
    ws = [ref[pl.ds(j, n, stride=TOKEN_ROWS), :] for j in range(TOKEN_ROWS)]
    lo = [lax.bitcast_convert_type(w << 16, F32) for w in ws]
    hi = [lax.bitcast_convert_type(w & jnp.uint32(0xFFFF0000), F32) for w in ws]
    return jnp.concatenate(lo + hi, axis=1)


def _strip_copies(n, src, src_tok, dst, dst_tok, sem, max_tokens, wait=False):
    del max_tokens

    @pl.when(n > 0)
    def _():
        cp = pltpu.make_async_copy(_tokens(src, src_tok, n), _tokens(dst, dst_tok, n), sem)
        if wait:
            cp.wait()
        else:
            cp.start()


def _dispatch_kernel(cnt_ref, off_ref, gst_ref, tot_ref, pst_ref, pen_ref,
                     h2_ref, pos_ref, xs_ref,
                     stage0_ref, stage1_ref, zero_ref, sem, zsem):
    i = pl.program_id(0)
    last = pl.num_programs(0) - 1
    tm = pos_ref.shape[1]
    na = 2 * tm
    d = h2_ref.shape[1]

    p0i = pos_ref[0:1, :].astype(jnp.int32)
    p1i = pos_ref[1:2, :].astype(jnp.int32)
    rid = lax.broadcasted_iota(jnp.int32, (na, tm), 0)
    onehot = jnp.where((rid == p0i) | (rid == p1i), 1.0, 0.0).astype(BF16)
    srt = jnp.dot(onehot, h2_ref[...], preferred_element_type=F32)

    def step(stage, s):
        @pl.when(i >= 2)
        def _():
            pltpu.make_async_copy(stage, _tokens(xs_ref, 0, na), sem.at[s]).wait()

        _pack_tokens(stage, srt)

        def per_expert(e, c):
            k = i * N_EXPERTS + e
            _strip_copies(cnt_ref[k], stage, off_ref[k], xs_ref, gst_ref[k], sem.at[s], na)
            return c

        lax.fori_loop(0, N_EXPERTS, per_expert, 0)

    @pl.when(i % 2 == 0)
    def _():
        step(stage0_ref, 0)

    @pl.when(i % 2 == 1)
    def _():
        step(stage1_ref, 1)

    @pl.when(i == last)
    def _():
        zero_ref[...] = jnp.zeros_like(zero_ref)
        for wait in (False, True):
            for e in range(N_EXPERTS):
                _strip_copies(pen_ref[e] - pst_ref[e] - tot_ref[e], zero_ref, 0, xs_ref,
                              pst_ref[e] + tot_ref[e], zsem, SLOT_BLOCK - 1, wait=wait)

        nblk = xs_ref.shape[0] // zero_ref.shape[0]

        def bcopy(b):
            return pltpu.make_async_copy(zero_ref, _tokens(xs_ref, b * SLOT_BLOCK, SLOT_BLOCK), zsem)

        first_unused = pen_ref[N_EXPERTS - 1] // SLOT_BLOCK
        lax.fori_loop(first_unused, nblk, lambda b, c: (bcopy(b).start(), c)[1], 0)
        lax.fori_loop(first_unused, nblk, lambda b, c: (bcopy(b).wait(), c)[1], 0)

        @pl.when(i >= 1)
        def _():
            @pl.when(i % 2 == 0)
            def _():
                pltpu.make_async_copy(stage1_ref, _tokens(xs_ref, 0, na), sem.at[1]).wait()

            @pl.when(i % 2 == 1)
            def _():
                pltpu.make_async_copy(stage0_ref, _tokens(xs_ref, 0, na), sem.at[0]).wait()

        @pl.when(i % 2 == 0)
        def _():
            pltpu.make_async_copy(stage0_ref, _tokens(xs_ref, 0, na), sem.at[0]).wait()

        @pl.when(i % 2 == 1)
        def _():
            pltpu.make_async_copy(stage1_ref, _tokens(xs_ref, 0, na), sem.at[1]).wait()


def _dispatch(cnt_t, off_t, gst_t, tot, pst, pen, h2, pos, tm, n_slots):
    sub = SUBLANES
    tr = TOKEN_ROWS
    t, d = h2.shape
    assert d == 2 * tr * LANES
    grid_spec = pltpu.PrefetchScalarGridSpec(
        num_scalar_prefetch=6,
        grid=(t // tm,),
        in_specs=[pl.BlockSpec((tm, d), lambda i, *_: (i, 0)),
                  pl.BlockSpec((sub, tm), lambda i, *_: (0, i))],
        out_specs=pl.BlockSpec(memory_space=pl.ANY),
        scratch_shapes=[pltpu.VMEM((2 * tm * tr, LANES), U32), pltpu.VMEM((2 * tm * tr, LANES), U32),
                        pltpu.VMEM((SLOT_BLOCK * tr, LANES), U32), pltpu.SemaphoreType.DMA((2,)),
                        pltpu.SemaphoreType.DMA(())],
    )
    return pl.pallas_call(
        _dispatch_kernel,
        grid_spec=grid_spec,
        out_shape=jax.ShapeDtypeStruct((n_slots * tr, LANES), U32),
        compiler_params=_params(),
        name="dispatch",
    )(cnt_t, off_t, gst_t, tot, pst, pen, h2, pos)


def _ffn_kernel(be_ref, nu_ref, xs_ref, wg_ref, wu_ref, wd_ref, ys_ref, wgb_ref, wub_ref, wdb_ref, *, blk):
    i = pl.program_id(0)

    @pl.when(i < nu_ref[0])
    def _():
        prev = be_ref[jnp.maximum(i - 1, 0)]

        @pl.when((i == 0) | (be_ref[i] != prev))
        def _():
            wgb_ref[...] = wg_ref[...].astype(BF16)
            wub_ref[...] = wu_ref[...].astype(BF16)
            wdb_ref[...] = wd_ref[...].astype(BF16)

        x = _unpack_tokens(xs_ref, blk).astype(BF16)
        g = jnp.dot(x, wgb_ref[...], preferred_element_type=F32)
        u = jnp.dot(x, wub_ref[...], preferred_element_type=F32)
        a = (g * _sigmoid(g)) * u
        y = jnp.dot(a.astype(BF16), wdb_ref[...], preferred_element_type=F32)
        _pack_tokens(ys_ref, y.astype(BF16).astype(F32))

    @pl.when(i >= nu_ref[0])
    def _():
        ys_ref[...] = jnp.zeros_like(ys_ref)


def _ffn(block_e, n_used, xs, wg, wu, wd):
    sub = TOKEN_ROWS
    blk = SLOT_BLOCK
    n_blocks = xs.shape[0] // (sub * blk)
    ne, d, de = wg.shape

    def slot_map(i, be, nu):
        return (i, 0)

    def w_map(i, be, nu):
        return (be[jnp.minimum(i, nu[0] - 1)], 0, 0)

    grid_spec = pltpu.PrefetchScalarGridSpec(
        num_scalar_prefetch=2,
        grid=(n_blocks,),
        in_specs=[pl.BlockSpec((blk * sub, LANES), slot_map),
                  pl.BlockSpec((None, d, de), w_map), pl.BlockSpec((None, d, de), w_map),
                  pl.BlockSpec((None, de, d), w_map)],
        out_specs=pl.BlockSpec((blk * sub, LANES), slot_map),
        scratch_shapes=[pltpu.VMEM((d, de), BF16), pltpu.VMEM((d, de), BF16), pltpu.VMEM((de, d), BF16)],
    )
    return pl.pallas_call(
        functools.partial(_ffn_kernel, blk=blk),
        grid_spec=grid_spec,
        out_shape=jax.ShapeDtypeStruct(xs.shape, U32),
        compiler_params=_params(),
        name="ffn",
    )(block_e, n_used, xs, wg, wu, wd)


def _combine_kernel(cnt_ref, off_ref, gst_ref, x1_ref, pos_ref, gate_ref, ys_ref, g_ref, o_ref,
                    buf0_ref, buf1_ref, sem):
    i = pl.program_id(0)
    n = pl.num_programs(0)
    tm, d = x1_ref.shape
    na = 2 * tm
    sub = SUBLANES

    def fetch(tile, buf, s):
        def per_expert(e, c):
            k = tile * N_EXPERTS + e
            _strip_copies(cnt_ref[k], ys_ref, gst_ref[k], buf, off_ref[k], sem.at[s], na)
            return c

        lax.fori_loop(0, N_EXPERTS, per_expert, 0)

    @pl.when(i == 0)
    def _():
        fetch(0, buf0_ref, 0)

    @pl.when((i + 1 < n) & (i % 2 == 0))
    def _():
        fetch(i + 1, buf1_ref, 1)

    @pl.when((i + 1 < n) & (i % 2 == 1))
    def _():
        fetch(i + 1, buf0_ref, 0)

    prow = jnp.concatenate([pos_ref[0:2, :], jnp.zeros((sub - 2, tm), F32)], axis=0)
    pcol = jnp.concatenate([prow] * (LANES // sub), axis=0).T.astype(jnp.int32)
    lane = lax.broadcasted_iota(jnp.int32, (tm, na), 1)
    pick = jnp.where((lane == pcol[:, 0:1]) | (lane == pcol[:, 1:2]), 1.0, 0.0).astype(BF16)
    rid = lax.broadcasted_iota(jnp.int32, (na, tm), 0)
    p0 = pos_ref[0:1, :].astype(jnp.int32)
    p1 = pos_ref[1:2, :].astype(jnp.int32)
    grow = jnp.sum(jnp.where(rid == p0, gate_ref[0:1, :], 0.0) + jnp.where(rid == p1, gate_ref[1:2, :], 0.0),
                   axis=1, keepdims=True)

    def finish(buf, s):
        pltpu.make_async_copy(_tokens(ys_ref, 0, na), buf, sem.at[s]).wait()
        y = _unpack_tokens(buf, na)
        moe = jnp.dot(pick, (y * grow).astype(BF16), preferred_element_type=F32)
        x = x1_ref[...] + moe
        o_ref[...] = _rms(x, g_ref[...]).reshape(o_ref.shape)

    @pl.when(i % 2 == 0)
    def _():
        finish(buf0_ref, 0)

    @pl.when(i % 2 == 1)
    def _():
        finish(buf1_ref, 1)


def _combine(cnt_t, off_t, gst_t, x1, pos, gate, ys, g, nb, seq):
    t, d = x1.shape
    tq = TIME_CHUNK
    tm = nb * tq
    sub = SUBLANES
    grid_spec = pltpu.PrefetchScalarGridSpec(
        num_scalar_prefetch=3,
        grid=(t // tm,),
        in_specs=[pl.BlockSpec((tm, d), lambda i, *_: (i, 0)),
                  pl.BlockSpec((sub, tm), lambda i, *_: (0, i)),
                  pl.BlockSpec((sub, tm), lambda i, *_: (0, i)),
                  pl.BlockSpec(memory_space=pl.ANY),
                  _full(g)],
        out_specs=pl.BlockSpec((nb, tq, d), lambda i, *_: (0, i, 0)),
        scratch_shapes=[pltpu.VMEM((2 * tm * TOKEN_ROWS, LANES), U32),
                        pltpu.VMEM((2 * tm * TOKEN_ROWS, LANES), U32), pltpu.SemaphoreType.DMA((2,))],
    )
    return pl.pallas_call(
        _combine_kernel,
        grid_spec=grid_spec,
        out_shape=jax.ShapeDtypeStruct((nb, seq, d), F32),
        compiler_params=_params(),
        name="combine",
    )(cnt_t, off_t, gst_t, x1, pos, gate, ys, g)


def _s5_tables(lam_re, lam_im, log_dt, b_re, b_im, c_re, c_im):
    ng, p = lam_re.shape
    npair = ng // 2
    kper = LANES // (2 * S5_GROUP)
    lam = lax.complex(lam_re, lam_im)
    dt = jnp.exp(log_dt)[:, None]
    lam_bar = jnp.exp(lam * dt)
    b_bar = ((lam_bar - 1.0) / lam)[..., None] * lax.complex(b_re, b_im)
    lr = jnp.real(lam_bar).reshape(npair, 2 * p)
    li = jnp.imag(lam_bar).reshape(npair, 2 * p)
    eye2 = jnp.eye(2, dtype=F32)
    sel = jax.nn.one_hot(jnp.arange(npair) % kper, kper, dtype=F32)

    bv = jnp.stack([jnp.real(b_bar), jnp.imag(b_bar)])
    bv = bv.reshape(2, npair, 2, p, S5_GROUP).transpose(1, 2, 4, 0, 3)
    bblk = bv[:, :, :, :, None, :] * eye2[None, :, None, None, :, None]
    bblk = bblk.reshape(npair, 2 * S5_GROUP, 4 * p)
    bm = (sel[:, :, None, None] * bblk[:, None]).reshape(npair, LANES, 4 * p)

    cv = jnp.stack([c_re, -c_im])
    cv = cv.reshape(2, npair, 2, S5_GROUP, p).transpose(1, 0, 2, 4, 3)
    cblk = cv[:, :, :, :, None, :] * eye2[None, None, :, None, :, None]
    cblk = cblk.reshape(npair, 4 * p, 2 * S5_GROUP)
    cm = (cblk[:, :, None, :] * sel[:, None, :, None]).reshape(npair, 4 * p, LANES)
    return bm.astype(BF16), lr, li, cm.astype(BF16)


def _blockdiag_tiles(w):
    nh, hi, ho = w.shape
    per = MXU_DIM // hi
    eye = jnp.eye(per, dtype=w.dtype)
    t = w.reshape(nh // per, per, hi, ho)[:, :, :, None, :] * eye[None, :, None, :, None]
    return t.reshape(nh // per, per * hi, per * ho)


def kernel(x, norm_mix, w_in, s5_lam_re, s5_lam_im, s5_log_dt, s5_b_re, s5_b_im, s5_c_re, s5_c_im, s5_d,
           w_s5_out, conv_w, conv_b, lru_w_r, lru_b_r, lru_w_i, lru_b_i, lru_lambda, w_lru_out, w_o, norm_ffn,
           router_group_w, router_group_b, router_expert_w, router_expert_b, expert_w_gate, expert_w_up,
           expert_w_down, norm_final):
    nb, seq, d = x.shape
    t = nb * seq
    assert w_in.shape[0] == 1, "one layer: the final RMSNorm is fused into the layer's combine kernel"
    l = 0
    s5w = s5_d.shape[-1]
    lruw = conv_b.shape[-1]
    rows = nb * TIME_CHUNK

    r = jnp.arange(rows)
    perm = jax.nn.one_hot((r % nb) * TIME_CHUNK + r // nb, rows, dtype=BF16)

    sp = jax.nn.softplus(-lru_lambda[l])[None, :]
    lrup = (conv_w[l], conv_b[l][None, :], _blockdiag_tiles(lru_w_r[l]).astype(BF16),
            _blockdiag_tiles(lru_w_i[l]).astype(BF16), lru_b_r[l][None, :], lru_b_i[l][None, :], sp)
    u, a, b, gg, ga, gb = _inproj(x, norm_mix[l][None, :], perm, w_in[l].astype(BF16), lrup, s5w, lruw)

    bm, lr, li, cm = _s5_tables(s5_lam_re[l], s5_lam_im[l], s5_log_dt[l], s5_b_re[l], s5_b_im[l],
                                s5_c_re[l], s5_c_im[l])
    s5p = (bm, lr, li, cm, s5_d[l][None, :], w_s5_out[l].astype(BF16))
    mixed = _mixers(u, a, b, gg, ga, gb, s5p, w_lru_out[l].astype(BF16), nb)

    gap = jnp.zeros((SUBLANES - N_GROUPS, d), F32)
    wrt = jnp.concatenate([router_group_w[l].T, gap, router_expert_w[l].T], axis=0)
    wr_hi = wrt.astype(BF16)
    wr_lo = (wrt - wr_hi.astype(F32)).astype(BF16)
    rb = jnp.concatenate([router_group_b[l], jnp.zeros((SUBLANES - N_GROUPS,), F32), router_expert_b[l]])
    rbias = jnp.broadcast_to(rb[:, None], (ROUTER_ROWS, LANES))

    x1, h2, logits = _mixroute(x, mixed, perm.T, w_o[l].astype(BF16), norm_ffn[l][None, :], wr_hi, wr_lo, rbias)
    tri = jnp.triu(jnp.ones((rows, rows), BF16), k=1)
    ltri = jnp.tril(jnp.ones((N_EXPERTS, N_EXPERTS), BF16), k=-1)
    gate, pos, tcnt = _route(logits, tri, ltri)

    n_tiles = seq // TIME_CHUNK
    cnt_t = tcnt.reshape(n_tiles, SUBLANES, LANES)[:, 0, :N_EXPERTS].astype(jnp.int32)
    off_t = jnp.cumsum(cnt_t, axis=1) - cnt_t
    counts = jnp.sum(cnt_t, axis=0)
    padded = ((counts + SLOT_BLOCK - 1) // SLOT_BLOCK) * SLOT_BLOCK
    pad_ends = jnp.cumsum(padded)
    pad_starts = pad_ends - padded
    gst_t = pad_starts[None, :] + jnp.cumsum(cnt_t, axis=0) - cnt_t
    n_blocks = -(-(2 * t) // SLOT_BLOCK) + N_EXPERTS
    n_slots = n_blocks * SLOT_BLOCK
    block_start = jnp.arange(n_blocks, dtype=jnp.int32) * SLOT_BLOCK
    block_e = jnp.minimum(jnp.sum((pad_ends[None, :] <= block_start[:, None]).astype(jnp.int32), axis=1),
                          N_EXPERTS - 1)
    n_used = pad_ends[-1:] // SLOT_BLOCK
    cnt_f, off_f, gst_f = cnt_t.reshape(-1), off_t.reshape(-1), gst_t.reshape(-1)

    xs = _dispatch(cnt_f, off_f, gst_f, counts, pad_starts, pad_ends, h2, pos, rows, n_slots)
    ys = _ffn(block_e, n_used, xs, expert_w_gate[l], expert_w_up[l], expert_w_down[l])
    return _combine(cnt_f, off_f, gst_f, x1, pos, gate, ys, norm_final[None, :], nb, seq)
```

```python
import functools
import math

import jax
import jax.numpy as jnp
from jax import lax
from jax.experimental import pallas as pl
from jax.experimental.pallas import tpu as pltpu

F32 = jnp.float32
BF16 = jnp.bfloat16
U32 = jnp.uint32

RMS_EPS = 1e-6
S5_GROUP = 16
CONV_WIDTH = 4
LRU_C = 8.0
N_GROUPS = 4
EXPERTS_PER_GROUP = 8
N_EXPERTS = N_GROUPS * EXPERTS_PER_GROUP

LANES = 128
SUBLANES = 8
TOKEN_ROWS = 4
MXU_DIM = 256
TIME_CHUNK = 64
SCAN_UNROLL = 4
SLOT_BLOCK = 512
ROUTE_WIDTH = 4096
ROUTER_ROWS = 40
VMEM_LIMIT = 56 * 1024 * 1024


def _gelu(x):
    c = math.sqrt(2.0 / math.pi)
    hx = 0.5 * x
    return hx * jnp.tanh(x * (c + (c * 0.044715) * (x * x))) + hx


def _sigmoid_of_half(xh):
    return 0.5 * jnp.tanh(xh) + 0.5


def _rms(x, g):
    return x * lax.rsqrt(jnp.mean(x * x, axis=-1, keepdims=True) + RMS_EPS) * g


def _params():
    return pltpu.CompilerParams(dimension_semantics=("arbitrary",), vmem_limit_bytes=VMEM_LIMIT)


def _full(a):
    return pl.BlockSpec(a.shape, lambda i, *_: (0,) * a.ndim)


def _tile(t):
    return pl.ds(pl.multiple_of(t * SUBLANES, SUBLANES), SUBLANES)


def _inproj_kernel(x_ref, g_ref, p_ref, w_ref, cw_ref, cb_ref, wr_ref, wi_ref, br_ref, bi_ref, sp_ref,
                   u_ref, a_ref, b_ref, gg_ref, ga_ref, gb_ref, xprev_ref, *, s5w, lruw):
    nb, tq, d = x_ref.shape
    rows = nb * tq
    halo = (CONV_WIDTH - 1) * nb

    @pl.when(pl.program_id(0) == 0)
    def _():
        xprev_ref[...] = jnp.zeros_like(xprev_ref)

    h = _rms(x_ref[...].reshape(rows, d), g_ref[...])
    hb = jnp.dot(p_ref[...], h.astype(BF16), preferred_element_type=F32).astype(BF16)

    def proj(lo, hi):
        return jnp.dot(hb, w_ref[:, lo:hi], preferred_element_type=F32)

    o1 = s5w
    o2 = o1 + lruw
    o3 = o2 + lruw
    o4 = o3 + d
    xin = proj(o1, o2)
    xext = jnp.concatenate([xprev_ref[...], xin], axis=0)
    xprev_ref[...] = xin[rows - halo:, :]
    xc_all = cb_ref[...]
    for k in range(CONV_WIDTH):
        xc_all = xc_all + xext[k * nb:k * nb + rows, :] * cw_ref[k:k + 1, :]

    def lru_gates(j):
        cs = slice(MXU_DIM * j, MXU_DIM * (j + 1))
        xc = xc_all[:, cs]
        xcb = xc.astype(BF16)
        tr = jnp.tanh(jnp.dot(xcb, wr_ref[j], preferred_element_type=F32) + br_ref[:, cs])
        ti = jnp.tanh(jnp.dot(xcb, wi_ref[j], preferred_element_type=F32) + bi_ref[:, cs])
        log_a = sp_ref[:, cs] * tr + sp_ref[:, cs]
        th = jnp.tanh(log_a)
        half_mult = jnp.sqrt((-0.5 * th) / (1.0 - th))
        a_ref[:, cs] = jnp.exp(log_a)
        b_ref[:, cs] = (half_mult * xc) * (ti + 1.0)

    others = [lambda: u_ref.__setitem__(Ellipsis, proj(0, o1)),
              lambda: gg_ref.__setitem__(Ellipsis, _gelu(proj(o2, o3)).astype(BF16)),
              lambda: ga_ref.__setitem__(Ellipsis, _sigmoid_of_half(proj(o3, o4)).astype(BF16)),
              lambda: gb_ref.__setitem__(Ellipsis, _sigmoid_of_half(proj(o4, o4 + d)).astype(BF16))]
    ntile = wr_ref.shape[0]
    for j in range(max(ntile, len(others))):
        if j < ntile:
            lru_gates(j)
        if j < len(others):
            others[j]()


def _inproj(x, g, perm, w_in_b, lrup, s5w, lruw):
    nb, seq, d = x.shape
    tq = TIME_CHUNK
    rows = nb * tq
    t = nb * seq
    row = lambda c: pl.BlockSpec((rows, c), lambda i: (i, 0))
    return pl.pallas_call(
        functools.partial(_inproj_kernel, s5w=s5w, lruw=lruw),
        grid=(seq // tq,),
        in_specs=[pl.BlockSpec((nb, tq, d), lambda i: (0, i, 0)), _full(g), _full(perm), _full(w_in_b)]
        + [_full(p) for p in lrup],
        out_specs=[row(s5w), row(lruw), row(lruw), row(lruw), row(d), row(d)],
        out_shape=[jax.ShapeDtypeStruct((t, s5w), F32), jax.ShapeDtypeStruct((t, lruw), F32),
                   jax.ShapeDtypeStruct((t, lruw), F32), jax.ShapeDtypeStruct((t, lruw), BF16),
                   jax.ShapeDtypeStruct((t, d), BF16), jax.ShapeDtypeStruct((t, d), BF16)],
        scratch_shapes=[pltpu.VMEM(((CONV_WIDTH - 1) * nb, lruw), F32)],
        compiler_params=_params(),
        name="inproj",
    )(x, g, perm, w_in_b, *lrup)


def _mixers_kernel(u_ref, a_ref, b_ref, gg_ref, ga_ref, gb_ref,
                   bm_ref, lr_ref, li_ref, cm_ref, dk_ref, wo_ref, wlo_ref,
                   o_ref, state_ref, sbuf_ref, h_ref, hbuf_ref, *, tq, nb):
    d = o_ref.shape[-1]
    npair = bm_ref.shape[0]
    kper = LANES // (2 * S5_GROUP)

    @pl.when(pl.program_id(0) == 0)
    def _():
        state_ref[...] = jnp.zeros_like(state_ref)
        h_ref[...] = jnp.zeros_like(h_ref)

    u2 = u_ref[...]
    ub = u2.astype(BF16)
    for k in range(npair):
        kb = k // kper
        bu = jnp.dot(ub[:, LANES * kb:LANES * (kb + 1)], bm_ref[k], preferred_element_type=F32)
        sbuf_ref[2 * k] = bu[:, :LANES]
        sbuf_ref[2 * k + 1] = bu[:, LANES:]

    pairs_per_loop = 4
    for kk in range(npair // pairs_per_loop):
        k0 = kk * pairs_per_loop
        lr = [jnp.broadcast_to(lr_ref[k0 + j:k0 + j + 1, :], (nb, LANES)) for j in range(pairs_per_loop)]
        li = [jnp.broadcast_to(li_ref[k0 + j:k0 + j + 1, :], (nb, LANES)) for j in range(pairs_per_loop)]
        init = tuple(state_ref[2 * k0 + j] for j in range(2 * pairs_per_loop))

        def body(it, carry, k0=k0, lr=lr, li=li):
            carry = list(carry)
            for s in range(SCAN_UNROLL):
                rows = _tile(it * SCAN_UNROLL + s)
                for j in range(pairs_per_loop):
                    sre, sim = carry[2 * j], carry[2 * j + 1]
                    nre = lr[j] * sre - li[j] * sim + sbuf_ref[2 * (k0 + j), rows, :]
                    nim = lr[j] * sim + li[j] * sre + sbuf_ref[2 * (k0 + j) + 1, rows, :]
                    sbuf_ref[2 * (k0 + j), rows, :] = nre
                    sbuf_ref[2 * (k0 + j) + 1, rows, :] = nim
                    carry[2 * j], carry[2 * j + 1] = nre, nim
            return tuple(carry)

        fin = lax.fori_loop(0, tq // SCAN_UNROLL, body, init)
        for j in range(2 * pairs_per_loop):
            state_ref[2 * k0 + j] = fin[j]

    def scan_body(it, h):
        for s in range(SCAN_UNROLL):
            tr = _tile(it * SCAN_UNROLL + s)
            h = a_ref[tr, :] * h + b_ref[tr, :]
            hbuf_ref[tr, :] = h
        return h

    h_ref[...] = lax.fori_loop(0, tq // SCAN_UNROLL, scan_body, h_ref[...])

    ys = []
    for kb in range(npair // kper):
        acc = None
        for j in range(kper):
            k = kb * kper + j
            s = jnp.concatenate([sbuf_ref[2 * k], sbuf_ref[2 * k + 1]], axis=1).astype(BF16)
            p = jnp.dot(s, cm_ref[k], preferred_element_type=F32)
            acc = p if acc is None else acc + p
        ys.append(acc)
    y = jnp.concatenate(ys, axis=1) + dk_ref[...] * u2
    z = jnp.dot(_gelu(y).astype(BF16), wo_ref[...], preferred_element_type=F32)
    ya = z[:, :d] * _sigmoid_of_half(z[:, d:])

    yl = hbuf_ref[...] * gg_ref[...].astype(F32)
    yb = jnp.dot(yl.astype(BF16), wlo_ref[...], preferred_element_type=F32)
    o_ref[...] = (ya * ga_ref[...].astype(F32) + yb * gb_ref[...].astype(F32)).astype(BF16)


def _mixers(u, a, b, gg, ga, gb, s5p, w_lru_out_b, nb):
    t, s5w = u.shape
    w = a.shape[-1]
    d = ga.shape[-1]
    tq = TIME_CHUNK
    rows = nb * tq
    nstate = s5p[0].shape[0] * MXU_DIM
    row = lambda c: pl.BlockSpec((rows, c), lambda i: (i, 0))
    params = tuple(s5p) + (w_lru_out_b,)
    return pl.pallas_call(
        functools.partial(_mixers_kernel, tq=tq, nb=nb),
        grid=(t // rows,),
        in_specs=[row(s5w), row(w), row(w), row(w), row(d), row(d)] + [_full(p) for p in params],
        out_specs=row(d),
        out_shape=jax.ShapeDtypeStruct((t, d), BF16),
        scratch_shapes=[pltpu.VMEM((nstate // LANES, nb, LANES), F32),
                        pltpu.VMEM((nstate // LANES, rows, LANES), F32),
                        pltpu.VMEM((nb, w), F32),
                        pltpu.VMEM((rows, w), F32)],
        compiler_params=_params(),
        name="mixers",
    )(u, a, b, gg, ga, gb, *params)


def _mixroute_kernel(x_ref, m_ref, pt_ref, wo_ref, g_ref, wh_ref, wl_ref, rb_ref,
                     x1_ref, h2_ref, lt_ref):
    nb, tq, d = x_ref.shape
    tm = nb * tq
    m = jnp.dot(pt_ref[...], m_ref[...], preferred_element_type=F32).astype(BF16)
    x1 = x_ref[...].reshape(tm, d) + jnp.dot(m, wo_ref[...], preferred_element_type=F32)
    x1_ref[...] = x1
    h = _rms(x1, g_ref[...])

    hh = h.astype(BF16)
    h2_ref[...] = hh
    hl = (h - hh.astype(F32)).astype(BF16)
    dn = (((1,), (1,)), ((), ()))
    lt = (lax.dot_general(wh_ref[...], hh, dn, preferred_element_type=F32)
          + lax.dot_general(wh_ref[...], hl, dn, preferred_element_type=F32)
          + lax.dot_general(wl_ref[...], hh, dn, preferred_element_type=F32))
    lt_ref[...] = lt + rb_ref[:, 0:1]


def _mixroute(x, m, perm_t, w_o_b, g, wr_hi, wr_lo, rbias):
    nb, seq, d = x.shape
    tq = TIME_CHUNK
    tm = nb * tq
    t = nb * seq
    row = lambda c: pl.BlockSpec((tm, c), lambda i: (i, 0))
    return pl.pallas_call(
        _mixroute_kernel,
        grid=(seq // tq,),
        in_specs=[pl.BlockSpec((nb, tq, d), lambda i: (0, i, 0)), row(d), _full(perm_t), _full(w_o_b),
                  _full(g), _full(wr_hi), _full(wr_lo), _full(rbias)],
        out_specs=[row(d), row(d), pl.BlockSpec((ROUTER_ROWS, tm), lambda i: (0, i))],
        out_shape=[jax.ShapeDtypeStruct((t, d), F32), jax.ShapeDtypeStruct((t, d), BF16),
                   jax.ShapeDtypeStruct((ROUTER_ROWS, t), F32)],
        compiler_params=_params(),
        name="mixroute",
    )(x, m, perm_t, w_o_b, g, wr_hi, wr_lo, rbias)


def _route_kernel(lt_ref, tri_ref, ltri_ref, gate_ref, pos_ref, cnt_ref, *, tile):
    lt = lt_ref[...]
    tm = lt.shape[1]
    gl = lt[0:N_GROUPS, :]
    gmax = jnp.max(gl, axis=0, keepdims=True)
    gsum = jnp.sum(jnp.exp(gl - gmax), axis=0, keepdims=True)
    p_g = 1.0 / gsum
    iota_g = lax.broadcasted_iota(jnp.int32, gl.shape, 0)
    g_idx = jnp.min(jnp.where(gl == gmax, iota_g, N_GROUPS), axis=0, keepdims=True)

    el = lt[SUBLANES:SUBLANES + EXPERTS_PER_GROUP, :]
    for g in range(1, N_GROUPS):
        lo = SUBLANES + EXPERTS_PER_GROUP * g
        el = jnp.where(g_idx == g, lt[lo:lo + EXPERTS_PER_GROUP, :], el)
    emax = jnp.max(el, axis=0, keepdims=True)
    esum = jnp.sum(jnp.exp(el - emax), axis=0, keepdims=True)
    iota_e = lax.broadcasted_iota(jnp.int32, el.shape, 0)
    i1 = jnp.min(jnp.where(el == emax, iota_e, EXPERTS_PER_GROUP), axis=0, keepdims=True)
    el2 = jnp.where(iota_e == i1, -jnp.inf, el)
    emax2 = jnp.max(el2, axis=0, keepdims=True)
    i2 = jnp.min(jnp.where(el2 == emax2, iota_e, EXPERTS_PER_GROUP), axis=0, keepdims=True)
    p1 = 1.0 / esum
    p2 = jnp.exp(emax2 - emax) / esum
    psum = p1 + p2
    g1 = p_g * (p1 / psum)
    g2 = p_g * (p2 / psum)
    e1 = g_idx * EXPERTS_PER_GROUP + i1
    e2 = g_idx * EXPERTS_PER_GROUP + i2
    gate_ref[...] = jnp.concatenate([g1, g2, jnp.zeros((SUBLANES - 2, tm), F32)], axis=0)

    dn = (((1,), (1,)), ((), ()))
    iota = lax.broadcasted_iota(jnp.int32, (N_EXPERTS, tile), 0)
    iota_l = lax.broadcasted_iota(jnp.int32, (LANES, tile), 0)
    ones = jnp.ones((SUBLANES, tile), BF16)
    for s in range(tm // tile):
        cs = slice(s * tile, (s + 1) * tile)
        a1, a2 = e1[:, cs], e2[:, cs]
        oh0 = iota == a1
        oh1 = iota == a2
        c = jnp.where(oh0 | oh1, 1.0, 0.0)
        ci = jnp.sum(c, axis=1, keepdims=True).astype(jnp.int32)
        hi = jnp.broadcast_to((ci >> 5).astype(F32), (N_EXPERTS, LANES)).astype(BF16)
        lo = jnp.broadcast_to((ci & 31).astype(F32), (N_EXPERTS, LANES)).astype(BF16)
        first = (32.0 * jnp.dot(ltri_ref[...], hi, preferred_element_type=F32)
                 + jnp.dot(ltri_ref[...], lo, preferred_element_type=F32))
        before = jnp.dot(c.astype(BF16), tri_ref[...], preferred_element_type=F32) + first[:, 0:1]
        p0 = jnp.sum(jnp.where(oh0, before, 0.0), axis=0, keepdims=True)
        p1 = jnp.sum(jnp.where(oh1, before, 0.0), axis=0, keepdims=True)
        pos_ref[:, cs] = jnp.concatenate([p0, p1, jnp.zeros((SUBLANES - 2, tile), F32)], axis=0)
        hits = jnp.where((iota_l == a1) | (iota_l == a2), 1.0, 0.0).astype(BF16)
        cnt_ref[SUBLANES * s:SUBLANES * (s + 1), :] = lax.dot_general(ones, hits, dn, preferred_element_type=F32)


def _route(lt, tri, ltri):
    t = lt.shape[1]
    tile = tri.shape[0]
    tm = min(t, ROUTE_WIDTH)
    col = pl.BlockSpec((SUBLANES, tm), lambda i: (0, i))
    return pl.pallas_call(
        functools.partial(_route_kernel, tile=tile),
        grid=(t // tm,),
        in_specs=[pl.BlockSpec((ROUTER_ROWS, tm), lambda i: (0, i)), _full(tri), _full(ltri)],
        out_specs=[col, col, pl.BlockSpec((tm // tile * SUBLANES, LANES), lambda i: (i, 0))],
        out_shape=[jax.ShapeDtypeStruct((SUBLANES, t), F32), jax.ShapeDtypeStruct((SUBLANES, t), F32),
                   jax.ShapeDtypeStruct((t // tile * SUBLANES, LANES), F32)],
        compiler_params=_params(),
        name="route",
    )(lt, tri, ltri)


def _tokens(ref, tok, n):
    return ref.at[pl.ds(pl.multiple_of(tok * TOKEN_ROWS, TOKEN_ROWS), n * TOKEN_ROWS)]


def _pack_tokens(ref, v):
    n, d = v.shape
    half = d // 2
    bits = lax.bitcast_convert_type(v, U32)
    w = (bits[:, :half] >> 16) | (bits[:, half:] & jnp.uint32(0xFFFF0000))
    for j in range(TOKEN_ROWS):
        ref[pl.ds(j, n, stride=TOKEN_ROWS), :] = w[:, LANES * j:LANES * (j + 1)]


def _unpack_tokens(ref, n):
    ws = [ref[pl.ds(j, n, stride=TOKEN_ROWS), :] for j in range(TOKEN_ROWS)]
    lo = [lax.bitcast_convert_type(w << 16, F32) for w in ws]
    hi = [lax.bitcast_convert_type(w & jnp.uint32(0xFFFF0000), F32) for w in ws]
    return jnp.concatenate(lo + hi, axis=1)


def _strip_copies(n, src, src_tok, dst, dst_tok, sem, max_tokens, wait=False):
    del max_tokens

    @pl.when(n > 0)
    def _():
        cp = pltpu.make_async_copy(_tokens(src, src_tok, n), _tokens(dst, dst_tok, n), sem)
        if wait:
            cp.wait()
        else:
            cp.start()


def _dispatch_kernel(cnt_ref, off_ref, gst_ref, tot_ref, pst_ref, pen_ref,
                     h2_ref, pos_ref, xs_ref,
                     stage0_ref, stage1_ref, zero_ref, sem, zsem):
    i = pl.program_id(0)
    last = pl.num_programs(0) - 1
    tm = pos_ref.shape[1]
    na = 2 * tm
    d = h2_ref.shape[1]

    p0i = pos_ref[0:1, :].astype(jnp.int32)
    p1i = pos_ref[1:2, :].astype(jnp.int32)
    rid = lax.broadcasted_iota(jnp.int32, (na, tm), 0)
    onehot = jnp.where((rid == p0i) | (rid == p1i), 1.0, 0.0).astype(BF16)
    srt = jnp.dot(onehot, h2_ref[...], preferred_element_type=F32)

    def step(stage, s):
        @pl.when(i >= 2)
        def _():
            pltpu.make_async_copy(stage, _tokens(xs_ref, 0, na), sem.at[s]).wait()

        _pack_tokens(stage, srt)

        def per_expert(e, c):
            k = i * N_EXPERTS + e
            _strip_copies(cnt_ref[k], stage, off_ref[k], xs_ref, gst_ref[k], sem.at[s], na)
            return c

        lax.fori_loop(0, N_EXPERTS, per_expert, 0)

    @pl.when(i % 2 == 0)
    def _():
        step(stage0_ref, 0)

    @pl.when(i % 2 == 1)
    def _():
        step(stage1_ref, 1)

    @pl.when(i == last)
    def _():
        zero_ref[...] = jnp.zeros_like(zero_ref)
        for wait in (False, True):
            for e in range(N_EXPERTS):
                _strip_copies(pen_ref[e] - pst_ref[e] - tot_ref[e], zero_ref, 0, xs_ref,
                              pst_ref[e] + tot_ref[e], zsem, SLOT_BLOCK - 1, wait=wait)

        nblk = xs_ref.shape[0] // zero_ref.shape[0]

        def bcopy(b):
            return pltpu.make_async_copy(zero_ref, _tokens(xs_ref, b * SLOT_BLOCK, SLOT_BLOCK), zsem)

        first_unused = pen_ref[N_EXPERTS - 1] // SLOT_BLOCK
        lax.fori_loop(first_unused, nblk, lambda b, c: (bcopy(b).start(), c)[1], 0)
        lax.fori_loop(first_unused, nblk, lambda b, c: (bcopy(b).wait(), c)[1], 0)

        @pl.when(i >= 1)
        def _():
            @pl.when(i % 2 == 0)
            def _():
                pltpu.make_async_copy(stage1_ref, _tokens(xs_ref, 0, na), sem.at[1]).wait()

            @pl.when(i % 2 == 1)
            def _():
                pltpu.make_async_copy(stage0_ref, _tokens(xs_ref, 0, na), sem.at[0]).wait()

        @pl.when(i % 2 == 0)
        def _():
            pltpu.make_async_copy(stage0_ref, _tokens(xs_ref, 0, na), sem.at[0]).wait()

        @pl.when(i % 2 == 1)
        def _():
            pltpu.make_async_copy(stage1_ref, _tokens(xs_ref, 0, na), sem.at[1]).wait()


def _dispatch(cnt_t, off_t, gst_t, tot, pst, pen, h2, pos, tm, n_slots):
    sub = SUBLANES
    tr = TOKEN_ROWS
    t, d = h2.shape
    assert d == 2 * tr * LANES
    grid_spec = pltpu.PrefetchScalarGridSpec(
        num_scalar_prefetch=6,
        grid=(t // tm,),
        in_specs=[pl.BlockSpec((tm, d), lambda i, *_: (i, 0)),
                  pl.BlockSpec((sub, tm), lambda i, *_: (0, i))],
        out_specs=pl.BlockSpec(memory_space=pl.ANY),
        scratch_shapes=[pltpu.VMEM((2 * tm * tr, LANES), U32), pltpu.VMEM((2 * tm * tr, LANES), U32),
                        pltpu.VMEM((SLOT_BLOCK * tr, LANES), U32), pltpu.SemaphoreType.DMA((2,)),
                        pltpu.SemaphoreType.DMA(())],
    )
    return pl.pallas_call(
        _dispatch_kernel,
        grid_spec=grid_spec,
        out_shape=jax.ShapeDtypeStruct((n_slots * tr, LANES), U32),
        compiler_params=_params(),
        name="dispatch",
    )(cnt_t, off_t, gst_t, tot, pst, pen, h2, pos)


def _ffn_kernel(be_ref, nu_ref, xs_ref, wg_ref, wu_ref, wd_ref, ys_ref, wgb_ref, wub_ref, wdb_ref, *, blk):
    i = pl.program_id(0)

    @pl.when(i < nu_ref[0])
    def _():
        prev = be_ref[jnp.maximum(i - 1, 0)]

        @pl.when((i == 0) | (be_ref[i] != prev))
        def _():
            wgb_ref[...] = wg_ref[...].astype(BF16)
            wub_ref[...] = wu_ref[...].astype(BF16)
            wdb_ref[...] = wd_ref[...].astype(BF16)

        x = _unpack_tokens(xs_ref, blk).astype(BF16)
        g = jnp.dot(x, wgb_ref[...], preferred_element_type=F32)
        u = jnp.dot(x, wub_ref[...], preferred_element_type=F32)
        hg = 0.5 * g
        a = (hg * jnp.tanh(hg) + hg) * u
        y = jnp.dot(a.astype(BF16), wdb_ref[...], preferred_element_type=F32)
        _pack_tokens(ys_ref, y.astype(BF16).astype(F32))

    @pl.when(i >= nu_ref[0])
    def _():
        ys_ref[...] = jnp.zeros_like(ys_ref)


def _ffn(block_e, n_used, xs, wg, wu, wd):
    sub = TOKEN_ROWS
    blk = SLOT_BLOCK
    n_blocks = xs.shape[0] // (sub * blk)
    ne, d, de = wg.shape

    def slot_map(i, be, nu):
        return (i, 0)

    def w_map(i, be, nu):
        return (be[jnp.minimum(i, nu[0] - 1)], 0, 0)

    grid_spec = pltpu.PrefetchScalarGridSpec(
        num_scalar_prefetch=2,
        grid=(n_blocks,),
        in_specs=[pl.BlockSpec((blk * sub, LANES), slot_map),
                  pl.BlockSpec((None, d, de), w_map), pl.BlockSpec((None, d, de), w_map),
                  pl.BlockSpec((None, de, d), w_map)],
        out_specs=pl.BlockSpec((blk * sub, LANES), slot_map),
        scratch_shapes=[pltpu.VMEM((d, de), BF16), pltpu.VMEM((d, de), BF16), pltpu.VMEM((de, d), BF16)],
    )
    return pl.pallas_call(
        functools.partial(_ffn_kernel, blk=blk),
        grid_spec=grid_spec,
        out_shape=jax.ShapeDtypeStruct(xs.shape, U32),
        compiler_params=_params(),
        name="ffn",
    )(block_e, n_used, xs, wg, wu, wd)


def _combine_kernel(cnt_ref, off_ref, gst_ref, x1_ref, pos_ref, gate_ref, ys_ref, g_ref, o_ref,
                    buf0_ref, buf1_ref, sem):
    i = pl.program_id(0)
    n = pl.num_programs(0)
    tm, d = x1_ref.shape
    na = 2 * tm
    sub = SUBLANES

    def fetch(tile, buf, s):
        def per_expert(e, c):
            k = tile * N_EXPERTS + e
            _strip_copies(cnt_ref[k], ys_ref, gst_ref[k], buf, off_ref[k], sem.at[s], na)
            return c

        lax.fori_loop(0, N_EXPERTS, per_expert, 0)

    @pl.when(i == 0)
    def _():
        fetch(0, buf0_ref, 0)

    @pl.when((i + 1 < n) & (i % 2 == 0))
    def _():
        fetch(i + 1, buf1_ref, 1)

    @pl.when((i + 1 < n) & (i % 2 == 1))
    def _():
        fetch(i + 1, buf0_ref, 0)

    prow = jnp.concatenate([pos_ref[0:2, :], jnp.zeros((sub - 2, tm), F32)], axis=0)
    pcol = jnp.concatenate([prow] * (LANES // sub), axis=0).T.astype(jnp.int32)
    lane = lax.broadcasted_iota(jnp.int32, (tm, na), 1)
    pick = jnp.where((lane == pcol[:, 0:1]) | (lane == pcol[:, 1:2]), 1.0, 0.0).astype(BF16)
    rid = lax.broadcasted_iota(jnp.int32, (na, tm), 0)
    p0 = pos_ref[0:1, :].astype(jnp.int32)
    p1 = pos_ref[1:2, :].astype(jnp.int32)
    grow = jnp.sum(jnp.where(rid == p0, gate_ref[0:1, :], 0.0) + jnp.where(rid == p1, gate_ref[1:2, :], 0.0),
                   axis=1, keepdims=True)

    def finish(buf, s):
        pltpu.make_async_copy(_tokens(ys_ref, 0, na), buf, sem.at[s]).wait()
        y = _unpack_tokens(buf, na)
        moe = jnp.dot(pick, (y * grow).astype(BF16), preferred_element_type=F32)
        x = x1_ref[...] + moe
        o_ref[...] = _rms(x, g_ref[...]).reshape(o_ref.shape)

    @pl.when(i % 2 == 0)
    def _():
        finish(buf0_ref, 0)

    @pl.when(i % 2 == 1)
    def _():
        finish(buf1_ref, 1)


def _combine(cnt_t, off_t, gst_t, x1, pos, gate, ys, g, nb, seq):
    t, d = x1.shape
    tq = TIME_CHUNK
    tm = nb * tq
    sub = SUBLANES
    grid_spec = pltpu.PrefetchScalarGridSpec(
        num_scalar_prefetch=3,
        grid=(t // tm,),
        in_specs=[pl.BlockSpec((tm, d), lambda i, *_: (i, 0)),
                  pl.BlockSpec((sub, tm), lambda i, *_: (0, i)),
                  pl.BlockSpec((sub, tm), lambda i, *_: (0, i)),
                  pl.BlockSpec(memory_space=pl.ANY),
                  _full(g)],
        out_specs=pl.BlockSpec((nb, tq, d), lambda i, *_: (0, i, 0)),
        scratch_shapes=[pltpu.VMEM((2 * tm * TOKEN_ROWS, LANES), U32),
                        pltpu.VMEM((2 * tm * TOKEN_ROWS, LANES), U32), pltpu.SemaphoreType.DMA((2,))],
    )
    return pl.pallas_call(
        _combine_kernel,
        grid_spec=grid_spec,
        out_shape=jax.ShapeDtypeStruct((nb, seq, d), F32),
        compiler_params=_params(),
        name="combine",
    )(cnt_t, off_t, gst_t, x1, pos, gate, ys, g)


def _s5_tables(lam_re, lam_im, log_dt, b_re, b_im, c_re, c_im):
    ng, p = lam_re.shape
    npair = ng // 2
    kper = LANES // (2 * S5_GROUP)
    lam = lax.complex(lam_re, lam_im)
    dt = jnp.exp(log_dt)[:, None]
    lam_bar = jnp.exp(lam * dt)
    b_bar = ((lam_bar - 1.0) / lam)[..., None] * lax.complex(b_re, b_im)
    lr = jnp.real(lam_bar).reshape(npair, 2 * p)
    li = jnp.imag(lam_bar).reshape(npair, 2 * p)
    eye2 = jnp.eye(2, dtype=F32)
    sel = jax.nn.one_hot(jnp.arange(npair) % kper, kper, dtype=F32)

    bv = jnp.stack([jnp.real(b_bar), jnp.imag(b_bar)])
    bv = bv.reshape(2, npair, 2, p, S5_GROUP).transpose(1, 2, 4, 0, 3)
    bblk = bv[:, :, :, :, None, :] * eye2[None, :, None, None, :, None]
    bblk = bblk.reshape(npair, 2 * S5_GROUP, 4 * p)
    bm = (sel[:, :, None, None] * bblk[:, None]).reshape(npair, LANES, 4 * p)

    cv = jnp.stack([c_re, -c_im])
    cv = cv.reshape(2, npair, 2, S5_GROUP, p).transpose(1, 0, 2, 4, 3)
    cblk = cv[:, :, :, :, None, :] * eye2[None, None, :, None, :, None]
    cblk = cblk.reshape(npair, 4 * p, 2 * S5_GROUP)
    cm = (cblk[:, :, None, :] * sel[:, None, :, None]).reshape(npair, 4 * p, LANES)
    return bm.astype(BF16), lr, li, cm.astype(BF16)


def _blockdiag_tiles(w):
    nh, hi, ho = w.shape
    per = MXU_DIM // hi
    eye = jnp.eye(per, dtype=w.dtype)
    t = w.reshape(nh // per, per, hi, ho)[:, :, :, None, :] * eye[None, :, None, :, None]
    return t.reshape(nh // per, per * hi, per * ho)


def kernel(x, norm_mix, w_in, s5_lam_re, s5_lam_im, s5_log_dt, s5_b_re, s5_b_im, s5_c_re, s5_c_im, s5_d,
           w_s5_out, conv_w, conv_b, lru_w_r, lru_b_r, lru_w_i, lru_b_i, lru_lambda, w_lru_out, w_o, norm_ffn,
           router_group_w, router_group_b, router_expert_w, router_expert_b, expert_w_gate, expert_w_up,
           expert_w_down, norm_final):
    nb, seq, d = x.shape
    t = nb * seq
    assert w_in.shape[0] == 1, "one layer: the final RMSNorm is fused into the layer's combine kernel"
    l = 0
    s5w = s5_d.shape[-1]
    lruw = conv_b.shape[-1]
    rows = nb * TIME_CHUNK

    r = jnp.arange(rows)
    perm = jax.nn.one_hot((r % nb) * TIME_CHUNK + r // nb, rows, dtype=BF16)

    sp = ((-0.5 * LRU_C) * jax.nn.softplus(-lru_lambda[l]))[None, :]
    lrup = (conv_w[l], conv_b[l][None, :], _blockdiag_tiles(0.5 * lru_w_r[l]).astype(BF16),
            _blockdiag_tiles(0.5 * lru_w_i[l]).astype(BF16), 0.5 * lru_b_r[l][None, :],
            0.5 * lru_b_i[l][None, :], sp)
    gate_cols = jnp.arange(w_in.shape[-1]) >= s5w + 2 * lruw
    w_in_b = jnp.where(gate_cols[None, :], 0.5 * w_in[l], w_in[l]).astype(BF16)
    u, a, b, gg, ga, gb = _inproj(x, norm_mix[l][None, :], perm, w_in_b, lrup, s5w, lruw)

    bm, lr, li, cm = _s5_tables(s5_lam_re[l], s5_lam_im[l], s5_log_dt[l], s5_b_re[l], s5_b_im[l],
                                s5_c_re[l], s5_c_im[l])
    glu_cols = jnp.arange(w_s5_out.shape[-1]) >= d
    w_s5_out_b = jnp.where(glu_cols[None, :], 0.5 * w_s5_out[l], w_s5_out[l]).astype(BF16)
    s5p = (bm, lr, li, cm, s5_d[l][None, :], w_s5_out_b)
    mixed = _mixers(u, a, b, gg, ga, gb, s5p, w_lru_out[l].astype(BF16), nb)

    gap = jnp.zeros((SUBLANES - N_GROUPS, d), F32)
    wrt = jnp.concatenate([router_group_w[l].T, gap, router_expert_w[l].T], axis=0)
    wr_hi = wrt.astype(BF16)
    wr_lo = (wrt - wr_hi.astype(F32)).astype(BF16)
    rb = jnp.concatenate([router_group_b[l], jnp.zeros((SUBLANES - N_GROUPS,), F32), router_expert_b[l]])
    rbias = jnp.broadcast_to(rb[:, None], (ROUTER_ROWS, LANES))

    x1, h2, logits = _mixroute(x, mixed, perm.T, w_o[l].astype(BF16), norm_ffn[l][None, :], wr_hi, wr_lo, rbias)
    tri = jnp.triu(jnp.ones((rows, rows), BF16), k=1)
    ltri = jnp.tril(jnp.ones((N_EXPERTS, N_EXPERTS), BF16), k=-1)
    gate, pos, tcnt = _route(logits, tri, ltri)

    n_tiles = seq // TIME_CHUNK
    cnt_t = tcnt.reshape(n_tiles, SUBLANES, LANES)[:, 0, :N_EXPERTS].astype(jnp.int32)
    off_t = jnp.cumsum(cnt_t, axis=1) - cnt_t
    counts = jnp.sum(cnt_t, axis=0)
    padded = ((counts + SLOT_BLOCK - 1) // SLOT_BLOCK) * SLOT_BLOCK
    pad_ends = jnp.cumsum(padded)
    pad_starts = pad_ends - padded
    gst_t = pad_starts[None, :] + jnp.cumsum(cnt_t, axis=0) - cnt_t
    n_blocks = -(-(2 * t) // SLOT_BLOCK) + N_EXPERTS
    n_slots = n_blocks * SLOT_BLOCK
    block_start = jnp.arange(n_blocks, dtype=jnp.int32) * SLOT_BLOCK
    block_e = jnp.minimum(jnp.sum((pad_ends[None, :] <= block_start[:, None]).astype(jnp.int32), axis=1),
                          N_EXPERTS - 1)
    n_used = pad_ends[-1:] // SLOT_BLOCK
    cnt_f, off_f, gst_f = cnt_t.reshape(-1), off_t.reshape(-1), gst_t.reshape(-1)

    xs = _dispatch(cnt_f, off_f, gst_f, counts, pad_starts, pad_ends, h2, pos, rows, n_slots)
    ys = _ffn(block_e, n_used, xs, expert_w_gate[l], expert_w_up[l], expert_w_down[l])
    return _combine(cnt_f, off_f, gst_f, x1, pos, gate, ys, norm_final[None, :], nb, seq)
```

```python
import functools
import math

import jax
import jax.numpy as jnp
from jax import lax
from jax.experimental import pallas as pl
from jax.experimental.pallas import tpu as pltpu

F32 = jnp.float32
BF16 = jnp.bfloat16
U32 = jnp.uint32

RMS_EPS = 1e-6
S5_GROUP = 16
CONV_WIDTH = 4
LRU_C = 8.0
N_GROUPS = 4
EXPERTS_PER_GROUP = 8
N_EXPERTS = N_GROUPS * EXPERTS_PER_GROUP

LANES = 128
SUBLANES = 8
TOKEN_ROWS = 4
MXU_DIM = 256
TIME_CHUNK = 64
SCAN_UNROLL = 4
SLOT_BLOCK = 512
ROUTE_WIDTH = 4096
ROUTER_ROWS = 40
VMEM_LIMIT = 56 * 1024 * 1024


def _gelu(x):
    c = math.sqrt(2.0 / math.pi)
    hx = 0.5 * x
    return hx * jnp.tanh(x * (c + (c * 0.044715) * (x * x))) + hx


def _sigmoid_of_half(xh):
    return 0.5 * jnp.tanh(xh) + 0.5


def _rms(x, g):
    return x * lax.rsqrt(jnp.mean(x * x, axis=-1, keepdims=True) + RMS_EPS) * g


def _params():
    return pltpu.CompilerParams(dimension_semantics=("arbitrary",), vmem_limit_bytes=VMEM_LIMIT)


def _full(a):
    return pl.BlockSpec(a.shape, lambda i, *_: (0,) * a.ndim)


def _tile(t):
    return pl.ds(pl.multiple_of(t * SUBLANES, SUBLANES), SUBLANES)


def _inproj_kernel(x_ref, g_ref, p_ref, w_ref, cw_ref, cb_ref, wr_ref, wi_ref, br_ref, bi_ref, sp_ref,
                   u_ref, a_ref, b_ref, gg_ref, ga_ref, gb_ref, xprev_ref, *, s5w, lruw):
    nb, tq, d = x_ref.shape
    rows = nb * tq
    halo = (CONV_WIDTH - 1) * nb

    @pl.when(pl.program_id(0) == 0)
    def _():
        xprev_ref[...] = jnp.zeros_like(xprev_ref)

    h = _rms(x_ref[...].reshape(rows, d), g_ref[...])
    hb = jnp.dot(p_ref[...], h.astype(BF16), preferred_element_type=F32).astype(BF16)

    def proj(lo, hi):
        return jnp.dot(hb, w_ref[:, lo:hi], preferred_element_type=F32)

    o1 = s5w
    o2 = o1 + lruw
    o3 = o2 + lruw
    o4 = o3 + d
    xin = proj(o1, o2)
    xext = jnp.concatenate([xprev_ref[...], xin], axis=0)
    xprev_ref[...] = xin[rows - halo:, :]
    xc_all = cb_ref[...]
    for k in range(CONV_WIDTH):
        xc_all = xc_all + xext[k * nb:k * nb + rows, :] * cw_ref[k:k + 1, :]

    def lru_gates(j):
        cs = slice(MXU_DIM * j, MXU_DIM * (j + 1))
        xc = xc_all[:, cs]
        xcb = xc.astype(BF16)
        tr = jnp.tanh(jnp.dot(xcb, wr_ref[j], preferred_element_type=F32) + br_ref[:, cs])
        ti = jnp.tanh(jnp.dot(xcb, wi_ref[j], preferred_element_type=F32) + bi_ref[:, cs])
        log_a = sp_ref[:, cs] * tr + sp_ref[:, cs]
        th = jnp.tanh(log_a)
        half_mult = jnp.sqrt((-0.5 * th) / (1.0 - th))
        a_ref[:, cs] = jnp.exp(log_a)
        b_ref[:, cs] = (half_mult * xc) * (ti + 1.0)

    others = [lambda: u_ref.__setitem__(Ellipsis, proj(0, o1)),
              lambda: gg_ref.__setitem__(Ellipsis, _gelu(proj(o2, o3)).astype(BF16)),
              lambda: ga_ref.__setitem__(Ellipsis, _sigmoid_of_half(proj(o3, o4)).astype(BF16)),
              lambda: gb_ref.__setitem__(Ellipsis, _sigmoid_of_half(proj(o4, o4 + d)).astype(BF16))]
    ntile = wr_ref.shape[0]
    for j in range(max(ntile, len(others))):
        if j < ntile:
            lru_gates(j)
        if j < len(others):
            others[j]()


def _inproj(x, g, perm, w_in_b, lrup, s5w, lruw):
    nb, seq, d = x.shape
    tq = TIME_CHUNK
    rows = nb * tq
    t = nb * seq
    row = lambda c: pl.BlockSpec((rows, c), lambda i: (i, 0))
    return pl.pallas_call(
        functools.partial(_inproj_kernel, s5w=s5w, lruw=lruw),
        grid=(seq // tq,),
        in_specs=[pl.BlockSpec((nb, tq, d), lambda i: (0, i, 0)), _full(g), _full(perm), _full(w_in_b)]
        + [_full(p) for p in lrup],
        out_specs=[row(s5w), row(lruw), row(lruw), row(lruw), row(d), row(d)],
        out_shape=[jax.ShapeDtypeStruct((t, s5w), F32), jax.ShapeDtypeStruct((t, lruw), F32),
                   jax.ShapeDtypeStruct((t, lruw), F32), jax.ShapeDtypeStruct((t, lruw), BF16),
                   jax.ShapeDtypeStruct((t, d), BF16), jax.ShapeDtypeStruct((t, d), BF16)],
        scratch_shapes=[pltpu.VMEM(((CONV_WIDTH - 1) * nb, lruw), F32)],
        compiler_params=_params(),
        name="inproj",
    )(x, g, perm, w_in_b, *lrup)


def _mixers_kernel(u_ref, a_ref, b_ref, gg_ref, ga_ref, gb_ref,
                   bm_ref, lr_ref, li_ref, cm_ref, cml_ref, gm_ref, dk_ref, wo_ref, wlo_ref,
                   o_ref, state_ref, sbuf_ref, uprev_ref, h_ref, hbuf_ref, *, tq, nb):
    d = o_ref.shape[-1]
    s5w = u_ref.shape[-1]
    npair = bm_ref.shape[0]
    kper = LANES // (2 * S5_GROUP)
    nk = tq // 2
    half = nk * nb

    @pl.when(pl.program_id(0) == 0)
    def _():
        state_ref[...] = jnp.zeros_like(state_ref)
        uprev_ref[...] = jnp.zeros_like(uprev_ref)
        h_ref[...] = jnp.zeros_like(h_ref)

    u2 = u_ref[...]
    u4 = u2.reshape(nk, 2, nb, s5w)
    u_e = u4[:, 0].reshape(half, s5w)
    u_o = u4[:, 1].reshape(half, s5w)
    u_po = jnp.concatenate([uprev_ref[...], u_o[:half - nb]], axis=0)
    uprev_ref[...] = u_o[half - nb:]
    ue_b = u_e.astype(BF16)
    upo_b = u_po.astype(BF16)
    for k in range(npair):
        cs = slice(LANES * (k // kper), LANES * (k // kper + 1))
        bu = jnp.dot(jnp.concatenate([ue_b[:, cs], upo_b[:, cs]], axis=1), bm_ref[k],
                     preferred_element_type=F32)
        sbuf_ref[2 * k] = bu[:, :LANES]
        sbuf_ref[2 * k + 1] = bu[:, LANES:]

    pairs_per_loop = 4
    for kk in range(npair // pairs_per_loop):
        k0 = kk * pairs_per_loop
        lr = [jnp.broadcast_to(lr_ref[k0 + j:k0 + j + 1, :], (nb, LANES)) for j in range(pairs_per_loop)]
        li = [jnp.broadcast_to(li_ref[k0 + j:k0 + j + 1, :], (nb, LANES)) for j in range(pairs_per_loop)]
        init = tuple(state_ref[2 * k0 + j] for j in range(2 * pairs_per_loop))

        def body(it, carry, k0=k0, lr=lr, li=li):
            carry = list(carry)
            for s in range(SCAN_UNROLL):
                rows = _tile(it * SCAN_UNROLL + s)
                for j in range(pairs_per_loop):
                    sre, sim = carry[2 * j], carry[2 * j + 1]
                    nre = lr[j] * sre - li[j] * sim + sbuf_ref[2 * (k0 + j), rows, :]
                    nim = lr[j] * sim + li[j] * sre + sbuf_ref[2 * (k0 + j) + 1, rows, :]
                    sbuf_ref[2 * (k0 + j), rows, :] = nre
                    sbuf_ref[2 * (k0 + j) + 1, rows, :] = nim
                    carry[2 * j], carry[2 * j + 1] = nre, nim
            return tuple(carry)

        fin = lax.fori_loop(0, nk // SCAN_UNROLL, body, init)
        for j in range(2 * pairs_per_loop):
            state_ref[2 * k0 + j] = fin[j]

    def scan_body(it, h):
        for s in range(SCAN_UNROLL):
            tr = _tile(it * SCAN_UNROLL + s)
            h = a_ref[tr, :] * h + b_ref[tr, :]
            hbuf_ref[tr, :] = h
        return h

    h_ref[...] = lax.fori_loop(0, tq // SCAN_UNROLL, scan_body, h_ref[...])

    ys_e, ys_o = [], []
    for kb in range(npair // kper):
        acc_e = acc_o = None
        for j in range(kper):
            k = kb * kper + j
            s = jnp.concatenate([sbuf_ref[2 * k], sbuf_ref[2 * k + 1]], axis=1).astype(BF16)
            pe = jnp.dot(s, cm_ref[k], preferred_element_type=F32)
            po = jnp.dot(s, cml_ref[k], preferred_element_type=F32)
            acc_e = pe if acc_e is None else acc_e + pe
            acc_o = po if acc_o is None else acc_o + po
        ys_e.append(acc_e)
        ys_o.append(acc_o)
    y_e = jnp.concatenate(ys_e, axis=1) + dk_ref[...] * u_e
    y_o = (jnp.concatenate(ys_o, axis=1) + jnp.dot(u_o.astype(BF16), gm_ref[...], preferred_element_type=F32)
           + dk_ref[...] * u_o)
    y = jnp.stack([y_e.reshape(nk, nb, s5w), y_o.reshape(nk, nb, s5w)], axis=1).reshape(2 * half, s5w)
    z = jnp.dot(_gelu(y).astype(BF16), wo_ref[...], preferred_element_type=F32)
    ya = z[:, :d] * _sigmoid_of_half(z[:, d:])

    yl = hbuf_ref[...] * gg_ref[...].astype(F32)
    yb = jnp.dot(yl.astype(BF16), wlo_ref[...], preferred_element_type=F32)
    o_ref[...] = (ya * ga_ref[...].astype(F32) + yb * gb_ref[...].astype(F32)).astype(BF16)


def _mixers(u, a, b, gg, ga, gb, s5p, w_lru_out_b, nb):
    t, s5w = u.shape
    w = a.shape[-1]
    d = ga.shape[-1]
    tq = TIME_CHUNK
    rows = nb * tq
    nstate = s5p[0].shape[0] * MXU_DIM
    row = lambda c: pl.BlockSpec((rows, c), lambda i: (i, 0))
    params = tuple(s5p) + (w_lru_out_b,)
    return pl.pallas_call(
        functools.partial(_mixers_kernel, tq=tq, nb=nb),
        grid=(t // rows,),
        in_specs=[row(s5w), row(w), row(w), row(w), row(d), row(d)] + [_full(p) for p in params],
        out_specs=row(d),
        out_shape=jax.ShapeDtypeStruct((t, d), BF16),
        scratch_shapes=[pltpu.VMEM((nstate // LANES, nb, LANES), F32),
                        pltpu.VMEM((nstate // LANES, rows // 2, LANES), F32),
                        pltpu.VMEM((nb, s5w), F32),
                        pltpu.VMEM((nb, w), F32),
                        pltpu.VMEM((rows, w), F32)],
        compiler_params=_params(),
        name="mixers",
    )(u, a, b, gg, ga, gb, *params)


def _mixroute_kernel(x_ref, m_ref, pt_ref, wo_ref, g_ref, wh_ref, wl_ref, rb_ref,
                     x1_ref, h2_ref, lt_ref):
    nb, tq, d = x_ref.shape
    tm = nb * tq
    m = jnp.dot(pt_ref[...], m_ref[...], preferred_element_type=F32).astype(BF16)
    x1 = x_ref[...].reshape(tm, d) + jnp.dot(m, wo_ref[...], preferred_element_type=F32)
    x1_ref[...] = x1
    h = _rms(x1, g_ref[...])

    hh = h.astype(BF16)
    h2_ref[...] = hh
    hl = (h - hh.astype(F32)).astype(BF16)
    dn = (((1,), (1,)), ((), ()))
    lt = (lax.dot_general(wh_ref[...], hh, dn, preferred_element_type=F32)
          + lax.dot_general(wh_ref[...], hl, dn, preferred_element_type=F32)
          + lax.dot_general(wl_ref[...], hh, dn, preferred_element_type=F32))
    lt_ref[...] = lt + rb_ref[:, 0:1]


def _mixroute(x, m, perm_t, w_o_b, g, wr_hi, wr_lo, rbias):
    nb, seq, d = x.shape
    tq = TIME_CHUNK
    tm = nb * tq
    t = nb * seq
    row = lambda c: pl.BlockSpec((tm, c), lambda i: (i, 0))
    return pl.pallas_call(
        _mixroute_kernel,
        grid=(seq // tq,),
        in_specs=[pl.BlockSpec((nb, tq, d), lambda i: (0, i, 0)), row(d), _full(perm_t), _full(w_o_b),
                  _full(g), _full(wr_hi), _full(wr_lo), _full(rbias)],
        out_specs=[row(d), row(d), pl.BlockSpec((ROUTER_ROWS, tm), lambda i: (0, i))],
        out_shape=[jax.ShapeDtypeStruct((t, d), F32), jax.ShapeDtypeStruct((t, d), BF16),
                   jax.ShapeDtypeStruct((ROUTER_ROWS, t), F32)],
        compiler_params=_params(),
        name="mixroute",
    )(x, m, perm_t, w_o_b, g, wr_hi, wr_lo, rbias)


def _route_kernel(lt_ref, tri_ref, ltri_ref, gate_ref, pos_ref, cnt_ref, *, tile):
    lt = lt_ref[...]
    tm = lt.shape[1]
    gl = lt[0:N_GROUPS, :]
    gmax = jnp.max(gl, axis=0, keepdims=True)
    gsum = jnp.sum(jnp.exp(gl - gmax), axis=0, keepdims=True)
    p_g = 1.0 / gsum
    iota_g = lax.broadcasted_iota(jnp.int32, gl.shape, 0)
    g_idx = jnp.min(jnp.where(gl == gmax, iota_g, N_GROUPS), axis=0, keepdims=True)

    el = lt[SUBLANES:SUBLANES + EXPERTS_PER_GROUP, :]
    for g in range(1, N_GROUPS):
        lo = SUBLANES + EXPERTS_PER_GROUP * g
        el = jnp.where(g_idx == g, lt[lo:lo + EXPERTS_PER_GROUP, :], el)
    emax = jnp.max(el, axis=0, keepdims=True)
    esum = jnp.sum(jnp.exp(el - emax), axis=0, keepdims=True)
    iota_e = lax.broadcasted_iota(jnp.int32, el.shape, 0)
    i1 = jnp.min(jnp.where(el == emax, iota_e, EXPERTS_PER_GROUP), axis=0, keepdims=True)
    el2 = jnp.where(iota_e == i1, -jnp.inf, el)
    emax2 = jnp.max(el2, axis=0, keepdims=True)
    i2 = jnp.min(jnp.where(el2 == emax2, iota_e, EXPERTS_PER_GROUP), axis=0, keepdims=True)
    p1 = 1.0 / esum
    p2 = jnp.exp(emax2 - emax) / esum
    psum = p1 + p2
    g1 = p_g * (p1 / psum)
    g2 = p_g * (p2 / psum)
    e1 = g_idx * EXPERTS_PER_GROUP + i1
    e2 = g_idx * EXPERTS_PER_GROUP + i2
    gate_ref[...] = jnp.concatenate([g1, g2, jnp.zeros((SUBLANES - 2, tm), F32)], axis=0)

    dn = (((1,), (1,)), ((), ()))
    iota = lax.broadcasted_iota(jnp.int32, (N_EXPERTS, tile), 0)
    iota_l = lax.broadcasted_iota(jnp.int32, (LANES, tile), 0)
    ones = jnp.ones((SUBLANES, tile), BF16)
    for s in range(tm // tile):
        cs = slice(s * tile, (s + 1) * tile)
        a1, a2 = e1[:, cs], e2[:, cs]
        oh0 = iota == a1
        oh1 = iota == a2
        c = jnp.where(oh0 | oh1, 1.0, 0.0)
        ci = jnp.sum(c, axis=1, keepdims=True).astype(jnp.int32)
        hi = jnp.broadcast_to((ci >> 5).astype(F32), (N_EXPERTS, LANES)).astype(BF16)
        lo = jnp.broadcast_to((ci & 31).astype(F32), (N_EXPERTS, LANES)).astype(BF16)
        first = (32.0 * jnp.dot(ltri_ref[...], hi, preferred_element_type=F32)
                 + jnp.dot(ltri_ref[...], lo, preferred_element_type=F32))
        before = jnp.dot(c.astype(BF16), tri_ref[...], preferred_element_type=F32) + first[:, 0:1]
        p0 = jnp.sum(jnp.where(oh0, before, 0.0), axis=0, keepdims=True)
        p1 = jnp.sum(jnp.where(oh1, before, 0.0), axis=0, keepdims=True)
        pos_ref[:, cs] = jnp.concatenate([p0, p1, jnp.zeros((SUBLANES - 2, tile), F32)], axis=0)
        hits = jnp.where((iota_l == a1) | (iota_l == a2), 1.0, 0.0).astype(BF16)
        cnt_ref[SUBLANES * s:SUBLANES * (s + 1), :] = lax.dot_general(ones, hits, dn, preferred_element_type=F32)


def _route(lt, tri, ltri):
    t = lt.shape[1]
    tile = tri.shape[0]
    tm = min(t, ROUTE_WIDTH)
    col = pl.BlockSpec((SUBLANES, tm), lambda i: (0, i))
    return pl.pallas_call(
        functools.partial(_route_kernel, tile=tile),
        grid=(t // tm,),
        in_specs=[pl.BlockSpec((ROUTER_ROWS, tm), lambda i: (0, i)), _full(tri), _full(ltri)],
        out_specs=[col, col, pl.BlockSpec((tm // tile * SUBLANES, LANES), lambda i: (i, 0))],
        out_shape=[jax.ShapeDtypeStruct((SUBLANES, t), F32), jax.ShapeDtypeStruct((SUBLANES, t), F32),
                   jax.ShapeDtypeStruct((t // tile * SUBLANES, LANES), F32)],
        compiler_params=_params(),
        name="route",
    )(lt, tri, ltri)


def _tokens(ref, tok, n):
    return ref.at[pl.ds(pl.multiple_of(tok * TOKEN_ROWS, TOKEN_ROWS), n * TOKEN_ROWS)]


def _pack_tokens(ref, v):
    n, d = v.shape
    half = d // 2
    bits = lax.bitcast_convert_type(v, U32)
    w = (bits[:, :half] >> 16) | (bits[:, half:] & jnp.uint32(0xFFFF0000))
    for j in range(TOKEN_ROWS):
        ref[pl.ds(j, n, stride=TOKEN_ROWS), :] = w[:, LANES * j:LANES * (j + 1)]


def _unpack_tokens(ref, n):
    ws = [ref[pl.ds(j, n, stride=TOKEN_ROWS), :] for j in range(TOKEN_ROWS)]
    lo = [lax.bitcast_convert_type(w << 16, F32) for w in ws]
    hi = [lax.bitcast_convert_type(w & jnp.uint32(0xFFFF0000), F32) for w in ws]
    return jnp.concatenate(lo + hi, axis=1)


def _strip_copies(n, src, src_tok, dst, dst_tok, sem, max_tokens, wait=False):
    del max_tokens

    @pl.when(n > 0)
    def _():
        cp = pltpu.make_async_copy(_tokens(src, src_tok, n), _tokens(dst, dst_tok, n), sem)
        if wait:
            cp.wait()
        else:
            cp.start()


def _dispatch_kernel(cnt_ref, off_ref, gst_ref, tot_ref, pst_ref, pen_ref,
                     h2_ref, pos_ref, xs_ref,
                     stage0_ref, stage1_ref, zero_ref, sem, zsem):
    i = pl.program_id(0)
    last = pl.num_programs(0) - 1
    tm = pos_ref.shape[1]
    na = 2 * tm
    d = h2_ref.shape[1]

    p0i = pos_ref[0:1, :].astype(jnp.int32)
    p1i = pos_ref[1:2, :].astype(jnp.int32)
    rid = lax.broadcasted_iota(jnp.int32, (na, tm), 0)
    onehot = jnp.where((rid == p0i) | (rid == p1i), 1.0, 0.0).astype(BF16)
    srt = jnp.dot(onehot, h2_ref[...], preferred_element_type=F32)

    def step(stage, s):
        @pl.when(i >= 2)
        def _():
            pltpu.make_async_copy(stage, _tokens(xs_ref, 0, na), sem.at[s]).wait()

        _pack_tokens(stage, srt)

        def per_expert(e, c):
            k = i * N_EXPERTS + e
            _strip_copies(cnt_ref[k], stage, off_ref[k], xs_ref, gst_ref[k], sem.at[s], na)
            return c

        lax.fori_loop(0, N_EXPERTS, per_expert, 0)

    @pl.when(i % 2 == 0)
    def _():
        step(stage0_ref, 0)

    @pl.when(i % 2 == 1)
    def _():
        step(stage1_ref, 1)

    @pl.when(i == last)
    def _():
        zero_ref[...] = jnp.zeros_like(zero_ref)
        for wait in (False, True):
            for e in range(N_EXPERTS):
                _strip_copies(pen_ref[e] - pst_ref[e] - tot_ref[e], zero_ref, 0, xs_ref,
                              pst_ref[e] + tot_ref[e], zsem, SLOT_BLOCK - 1, wait=wait)

        nblk = xs_ref.shape[0] // zero_ref.shape[0]

        def bcopy(b):
            return pltpu.make_async_copy(zero_ref, _tokens(xs_ref, b * SLOT_BLOCK, SLOT_BLOCK), zsem)

        first_unused = pen_ref[N_EXPERTS - 1] // SLOT_BLOCK
        lax.fori_loop(first_unused, nblk, lambda b, c: (bcopy(b).start(), c)[1], 0)
        lax.fori_loop(first_unused, nblk, lambda b, c: (bcopy(b).wait(), c)[1], 0)

        @pl.when(i >= 1)
        def _():
            @pl.when(i % 2 == 0)
            def _():
                pltpu.make_async_copy(stage1_ref, _tokens(xs_ref, 0, na), sem.at[1]).wait()

            @pl.when(i % 2 == 1)
            def _():
                pltpu.make_async_copy(stage0_ref, _tokens(xs_ref, 0, na), sem.at[0]).wait()

        @pl.when(i % 2 == 0)
        def _():
            pltpu.make_async_copy(stage0_ref, _tokens(xs_ref, 0, na), sem.at[0]).wait()

        @pl.when(i % 2 == 1)
        def _():
            pltpu.make_async_copy(stage1_ref, _tokens(xs_ref, 0, na), sem.at[1]).wait()


def _dispatch(cnt_t, off_t, gst_t, tot, pst, pen, h2, pos, tm, n_slots):
    sub = SUBLANES
    tr = TOKEN_ROWS
    t, d = h2.shape
    assert d == 2 * tr * LANES
    grid_spec = pltpu.PrefetchScalarGridSpec(
        num_scalar_prefetch=6,
        grid=(t // tm,),
        in_specs=[pl.BlockSpec((tm, d), lambda i, *_: (i, 0)),
                  pl.BlockSpec((sub, tm), lambda i, *_: (0, i))],
        out_specs=pl.BlockSpec(memory_space=pl.ANY),
        scratch_shapes=[pltpu.VMEM((2 * tm * tr, LANES), U32), pltpu.VMEM((2 * tm * tr, LANES), U32),
                        pltpu.VMEM((SLOT_BLOCK * tr, LANES), U32), pltpu.SemaphoreType.DMA((2,)),
                        pltpu.SemaphoreType.DMA(())],
    )
    return pl.pallas_call(
        _dispatch_kernel,
        grid_spec=grid_spec,
        out_shape=jax.ShapeDtypeStruct((n_slots * tr, LANES), U32),
        compiler_params=_params(),
        name="dispatch",
    )(cnt_t, off_t, gst_t, tot, pst, pen, h2, pos)


def _ffn_kernel(be_ref, nu_ref, xs_ref, wg_ref, wu_ref, wd_ref, ys_ref, wgb_ref, wub_ref, wdb_ref, *, blk):
    i = pl.program_id(0)

    @pl.when(i < nu_ref[0])
    def _():
        prev = be_ref[jnp.maximum(i - 1, 0)]

        @pl.when((i == 0) | (be_ref[i] != prev))
        def _():
            wgb_ref[...] = wg_ref[...].astype(BF16)
            wub_ref[...] = wu_ref[...].astype(BF16)
            wdb_ref[...] = wd_ref[...].astype(BF16)

        x = _unpack_tokens(xs_ref, blk).astype(BF16)
        g = jnp.dot(x, wgb_ref[...], preferred_element_type=F32)
        u = jnp.dot(x, wub_ref[...], preferred_element_type=F32)
        hg = 0.5 * g
        a = (hg * jnp.tanh(hg) + hg) * u
        y = jnp.dot(a.astype(BF16), wdb_ref[...], preferred_element_type=F32)
        _pack_tokens(ys_ref, y.astype(BF16).astype(F32))

    @pl.when(i >= nu_ref[0])
    def _():
        ys_ref[...] = jnp.zeros_like(ys_ref)


def _ffn(block_e, n_used, xs, wg, wu, wd):
    sub = TOKEN_ROWS
    blk = SLOT_BLOCK
    n_blocks = xs.shape[0] // (sub * blk)
    ne, d, de = wg.shape

    def slot_map(i, be, nu):
        return (i, 0)

    def w_map(i, be, nu):
        return (be[jnp.minimum(i, nu[0] - 1)], 0, 0)

    grid_spec = pltpu.PrefetchScalarGridSpec(
        num_scalar_prefetch=2,
        grid=(n_blocks,),
        in_specs=[pl.BlockSpec((blk * sub, LANES), slot_map),
                  pl.BlockSpec((None, d, de), w_map), pl.BlockSpec((None, d, de), w_map),
                  pl.BlockSpec((None, de, d), w_map)],
        out_specs=pl.BlockSpec((blk * sub, LANES), slot_map),
        scratch_shapes=[pltpu.VMEM((d, de), BF16), pltpu.VMEM((d, de), BF16), pltpu.VMEM((de, d), BF16)],
    )
    return pl.pallas_call(
        functools.partial(_ffn_kernel, blk=blk),
        grid_spec=grid_spec,
        out_shape=jax.ShapeDtypeStruct(xs.shape, U32),
        compiler_params=_params(),
        name="ffn",
    )(block_e, n_used, xs, wg, wu, wd)


def _combine_kernel(cnt_ref, off_ref, gst_ref, x1_ref, pos_ref, gate_ref, ys_ref, g_ref, o_ref,
                    buf0_ref, buf1_ref, sem):
    i = pl.program_id(0)
    n = pl.num_programs(0)
    tm, d = x1_ref.shape
    na = 2 * tm
    sub = SUBLANES

    def fetch(tile, buf, s):
        def per_expert(e, c):
            k = tile * N_EXPERTS + e
            _strip_copies(cnt_ref[k], ys_ref, gst_ref[k], buf, off_ref[k], sem.at[s], na)
            return c

        lax.fori_loop(0, N_EXPERTS, per_expert, 0)

    @pl.when(i == 0)
    def _():
        fetch(0, buf0_ref, 0)

    @pl.when((i + 1 < n) & (i % 2 == 0))
    def _():
        fetch(i + 1, buf1_ref, 1)

    @pl.when((i + 1 < n) & (i % 2 == 1))
    def _():
        fetch(i + 1, buf0_ref, 0)

    prow = jnp.concatenate([pos_ref[0:2, :], jnp.zeros((sub - 2, tm), F32)], axis=0)
    pcol = jnp.concatenate([prow] * (LANES // sub), axis=0).T.astype(jnp.int32)
    lane = lax.broadcasted_iota(jnp.int32, (tm, na), 1)
    pick = jnp.where((lane == pcol[:, 0:1]) | (lane == pcol[:, 1:2]), 1.0, 0.0).astype(BF16)
    rid = lax.broadcasted_iota(jnp.int32, (na, tm), 0)
    p0 = pos_ref[0:1, :].astype(jnp.int32)
    p1 = pos_ref[1:2, :].astype(jnp.int32)
    grow = jnp.sum(jnp.where(rid == p0, gate_ref[0:1, :], 0.0) + jnp.where(rid == p1, gate_ref[1:2, :], 0.0),
                   axis=1, keepdims=True)

    def finish(buf, s):
        pltpu.make_async_copy(_tokens(ys_ref, 0, na), buf, sem.at[s]).wait()
        y = _unpack_tokens(buf, na)
        moe = jnp.dot(pick, (y * grow).astype(BF16), preferred_element_type=F32)
        x = x1_ref[...] + moe
        o_ref[...] = _rms(x, g_ref[...]).reshape(o_ref.shape)

    @pl.when(i % 2 == 0)
    def _():
        finish(buf0_ref, 0)

    @pl.when(i % 2 == 1)
    def _():
        finish(buf1_ref, 1)


def _combine(cnt_t, off_t, gst_t, x1, pos, gate, ys, g, nb, seq):
    t, d = x1.shape
    tq = TIME_CHUNK
    tm = nb * tq
    sub = SUBLANES
    grid_spec = pltpu.PrefetchScalarGridSpec(
        num_scalar_prefetch=3,
        grid=(t // tm,),
        in_specs=[pl.BlockSpec((tm, d), lambda i, *_: (i, 0)),
                  pl.BlockSpec((sub, tm), lambda i, *_: (0, i)),
                  pl.BlockSpec((sub, tm), lambda i, *_: (0, i)),
                  pl.BlockSpec(memory_space=pl.ANY),
                  _full(g)],
        out_specs=pl.BlockSpec((nb, tq, d), lambda i, *_: (0, i, 0)),
        scratch_shapes=[pltpu.VMEM((2 * tm * TOKEN_ROWS, LANES), U32),
                        pltpu.VMEM((2 * tm * TOKEN_ROWS, LANES), U32), pltpu.SemaphoreType.DMA((2,))],
    )
    return pl.pallas_call(
        _combine_kernel,
        grid_spec=grid_spec,
        out_shape=jax.ShapeDtypeStruct((nb, seq, d), F32),
        compiler_params=_params(),
        name="combine",
    )(cnt_t, off_t, gst_t, x1, pos, gate, ys, g)


def _s5_tables(lam_re, lam_im, log_dt, b_re, b_im, c_re, c_im):
    ng, p = lam_re.shape
    npair = ng // 2
    kper = LANES // (2 * S5_GROUP)
    lam = lax.complex(lam_re, lam_im)
    dt = jnp.exp(log_dt)[:, None]
    lam_bar = jnp.exp(lam * dt)
    b_bar = ((lam_bar - 1.0) / lam)[..., None] * lax.complex(b_re, b_im)
    cmat = lax.complex(c_re, c_im)
    lam2 = lam_bar * lam_bar
    lr = jnp.real(lam2).reshape(npair, 2 * p)
    li = jnp.imag(lam2).reshape(npair, 2 * p)
    eye2 = jnp.eye(2, dtype=F32)
    sel = jax.nn.one_hot(jnp.arange(npair) % kper, kper, dtype=F32)

    def in_tiles(bc):
        bv = jnp.stack([jnp.real(bc), jnp.imag(bc)])
        bv = bv.reshape(2, npair, 2, p, S5_GROUP).transpose(1, 2, 4, 0, 3)
        bblk = bv[:, :, :, :, None, :] * eye2[None, :, None, None, :, None]
        bblk = bblk.reshape(npair, 2 * S5_GROUP, 4 * p)
        return (sel[:, :, None, None] * bblk[:, None]).reshape(npair, LANES, 4 * p)

    def out_tiles(cc):
        cv = jnp.stack([jnp.real(cc), -jnp.imag(cc)])
        cv = cv.reshape(2, npair, 2, S5_GROUP, p).transpose(1, 0, 2, 4, 3)
        cblk = cv[:, :, :, :, None, :] * eye2[None, None, :, None, :, None]
        cblk = cblk.reshape(npair, 4 * p, 2 * S5_GROUP)
        return (cblk[:, :, None, :] * sel[:, None, :, None]).reshape(npair, 4 * p, LANES)

    bm2 = jnp.concatenate([in_tiles(b_bar), in_tiles(lam_bar[..., None] * b_bar)], axis=1)
    cm = out_tiles(cmat)
    cml = out_tiles(cmat * lam_bar[:, None, :])
    direct = jnp.real(jnp.einsum('gop,gpi->gio', cmat, b_bar))
    gmat = (direct[:, :, None, :] * jnp.eye(ng, dtype=F32)[:, None, :, None]).reshape(
        ng * S5_GROUP, ng * S5_GROUP)
    return bm2.astype(BF16), lr, li, cm.astype(BF16), cml.astype(BF16), gmat.astype(BF16)


def _blockdiag_tiles(w):
    nh, hi, ho = w.shape
    per = MXU_DIM // hi
    eye = jnp.eye(per, dtype=w.dtype)
    t = w.reshape(nh // per, per, hi, ho)[:, :, :, None, :] * eye[None, :, None, :, None]
    return t.reshape(nh // per, per * hi, per * ho)


def kernel(x, norm_mix, w_in, s5_lam_re, s5_lam_im, s5_log_dt, s5_b_re, s5_b_im, s5_c_re, s5_c_im, s5_d,
           w_s5_out, conv_w, conv_b, lru_w_r, lru_b_r, lru_w_i, lru_b_i, lru_lambda, w_lru_out, w_o, norm_ffn,
           router_group_w, router_group_b, router_expert_w, router_expert_b, expert_w_gate, expert_w_up,
           expert_w_down, norm_final):
    nb, seq, d = x.shape
    t = nb * seq
    assert w_in.shape[0] == 1, "one layer: the final RMSNorm is fused into the layer's combine kernel"
    l = 0
    s5w = s5_d.shape[-1]
    lruw = conv_b.shape[-1]
    rows = nb * TIME_CHUNK

    r = jnp.arange(rows)
    perm = jax.nn.one_hot((r % nb) * TIME_CHUNK + r // nb, rows, dtype=BF16)

    sp = ((-0.5 * LRU_C) * jax.nn.softplus(-lru_lambda[l]))[None, :]
    lrup = (conv_w[l], conv_b[l][None, :], _blockdiag_tiles(0.5 * lru_w_r[l]).astype(BF16),
            _blockdiag_tiles(0.5 * lru_w_i[l]).astype(BF16), 0.5 * lru_b_r[l][None, :],
            0.5 * lru_b_i[l][None, :], sp)
    gate_cols = jnp.arange(w_in.shape[-1]) >= s5w + 2 * lruw
    w_in_b = jnp.where(gate_cols[None, :], 0.5 * w_in[l], w_in[l]).astype(BF16)
    u, a, b, gg, ga, gb = _inproj(x, norm_mix[l][None, :], perm, w_in_b, lrup, s5w, lruw)

    bm, lr, li, cm, cml, gmat = _s5_tables(s5_lam_re[l], s5_lam_im[l], s5_log_dt[l], s5_b_re[l], s5_b_im[l],
                                           s5_c_re[l], s5_c_im[l])
    glu_cols = jnp.arange(w_s5_out.shape[-1]) >= d
    w_s5_out_b = jnp.where(glu_cols[None, :], 0.5 * w_s5_out[l], w_s5_out[l]).astype(BF16)
    s5p = (bm, lr, li, cm, cml, gmat, s5_d[l][None, :], w_s5_out_b)
    mixed = _mixers(u, a, b, gg, ga, gb, s5p, w_lru_out[l].astype(BF16), nb)

    gap = jnp.zeros((SUBLANES - N_GROUPS, d), F32)
    wrt = jnp.concatenate([router_group_w[l].T, gap, router_expert_w[l].T], axis=0)
    wr_hi = wrt.astype(BF16)
    wr_lo = (wrt - wr_hi.astype(F32)).astype(BF16)
    rb = jnp.concatenate([router_group_b[l], jnp.zeros((SUBLANES - N_GROUPS,), F32), router_expert_b[l]])
    rbias = jnp.broadcast_to(rb[:, None], (ROUTER_ROWS, LANES))

    x1, h2, logits = _mixroute(x, mixed, perm.T, w_o[l].astype(BF16), norm_ffn[l][None, :], wr_hi, wr_lo, rbias)
    tri = jnp.triu(jnp.ones((rows, rows), BF16), k=1)
    ltri = jnp.tril(jnp.ones((N_EXPERTS, N_EXPERTS), BF16), k=-1)
    gate, pos, tcnt = _route(logits, tri, ltri)

    n_tiles = seq // TIME_CHUNK
    cnt_t = tcnt.reshape(n_tiles, SUBLANES, LANES)[:, 0, :N_EXPERTS].astype(jnp.int32)
    off_t = jnp.cumsum(cnt_t, axis=1) - cnt_t
    counts = jnp.sum(cnt_t, axis=0)
    padded = ((counts + SLOT_BLOCK - 1) // SLOT_BLOCK) * SLOT_BLOCK
    pad_ends = jnp.cumsum(padded)
    pad_starts = pad_ends - padded
    gst_t = pad_starts[None, :] + jnp.cumsum(cnt_t, axis=0) - cnt_t
    n_blocks = -(-(2 * t) // SLOT_BLOCK) + N_EXPERTS
    n_slots = n_blocks * SLOT_BLOCK
    block_start = jnp.arange(n_blocks, dtype=jnp.int32) * SLOT_BLOCK
    block_e = jnp.minimum(jnp.sum((pad_ends[None, :] <= block_start[:, None]).astype(jnp.int32), axis=1),
                          N_EXPERTS - 1)
    n_used = pad_ends[-1:] // SLOT_BLOCK
    cnt_f, off_f, gst_f = cnt_t.reshape(-1), off_t.reshape(-1), gst_t.reshape(-1)

    xs = _dispatch(cnt_f, off_f, gst_f, counts, pad_starts, pad_ends, h2, pos, rows, n_slots)
    ys = _ffn(block_e, n_used, xs, expert_w_gate[l], expert_w_up[l], expert_w_down[l])
    return _combine(cnt_f, off_f, gst_f, x1, pos, gate, ys, norm_final[None, :], nb, seq)
```

```python
import functools
import math

import jax
import jax.numpy as jnp
from jax import lax
from jax.experimental import pallas as pl
from jax.experimental.pallas import tpu as pltpu

F32 = jnp.float32
BF16 = jnp.bfloat16
U32 = jnp.uint32

RMS_EPS = 1e-6
S5_GROUP = 16
CONV_WIDTH = 4
LRU_C = 8.0
N_GROUPS = 4
EXPERTS_PER_GROUP = 8
N_EXPERTS = N_GROUPS * EXPERTS_PER_GROUP

LANES = 128
SUBLANES = 8
TOKEN_ROWS = 4
MXU_DIM = 256
TIME_CHUNK = 64
SCAN_UNROLL = 4
SLOT_BLOCK = 512
ROUTE_WIDTH = 4096
ROUTER_ROWS = 40
VMEM_LIMIT = 56 * 1024 * 1024


def _gelu(x):
    c = math.sqrt(2.0 / math.pi)
    hx = 0.5 * x
    return hx * jnp.tanh(x * (c + (c * 0.044715) * (x * x))) + hx


def _sigmoid_of_half(xh):
    return 0.5 * jnp.tanh(xh) + 0.5


def _rms(x, g):
    return x * lax.rsqrt(jnp.mean(x * x, axis=-1, keepdims=True) + RMS_EPS) * g


def _params():
    return pltpu.CompilerParams(dimension_semantics=("arbitrary",), vmem_limit_bytes=VMEM_LIMIT)


def _full(a):
    return pl.BlockSpec(a.shape, lambda i, *_: (0,) * a.ndim)


def _tile(t):
    return pl.ds(pl.multiple_of(t * SUBLANES, SUBLANES), SUBLANES)


def _inproj_kernel(x_ref, g_ref, p_ref, w_ref, cw_ref, cb_ref, wr_ref, wi_ref, br_ref, bi_ref, sp_ref,
                   u_ref, a_ref, b_ref, gg_ref, ga_ref, gb_ref, xprev_ref, *, s5w, lruw):
    nb, tq, d = x_ref.shape
    rows = nb * tq
    halo = (CONV_WIDTH - 1) * nb

    @pl.when(pl.program_id(0) == 0)
    def _():
        xprev_ref[...] = jnp.zeros_like(xprev_ref)

    h = _rms(x_ref[...].reshape(rows, d), g_ref[...])
    hb = jnp.dot(p_ref[...], h.astype(BF16), preferred_element_type=F32).astype(BF16)

    def proj(lo, hi):
        return jnp.dot(hb, w_ref[:, lo:hi], preferred_element_type=F32)

    o1 = s5w
    o2 = o1 + lruw
    o3 = o2 + lruw
    o4 = o3 + d
    xin = proj(o1, o2)
    xext = jnp.concatenate([xprev_ref[...], xin], axis=0)
    xprev_ref[...] = xin[rows - halo:, :]
    xc_all = cb_ref[...]
    for k in range(CONV_WIDTH):
        xc_all = xc_all + xext[k * nb:k * nb + rows, :] * cw_ref[k:k + 1, :]

    def lru_gates(j):
        cs = slice(MXU_DIM * j, MXU_DIM * (j + 1))
        xc = xc_all[:, cs]
        xcb = xc.astype(BF16)
        tr = jnp.tanh(jnp.dot(xcb, wr_ref[j], preferred_element_type=F32) + br_ref[:, cs])
        ti = jnp.tanh(jnp.dot(xcb, wi_ref[j], preferred_element_type=F32) + bi_ref[:, cs])
        log_a = sp_ref[:, cs] * tr + sp_ref[:, cs]
        th = jnp.tanh(log_a)
        half_mult = jnp.sqrt((-0.5 * th) / (1.0 - th))
        a_ref[:, cs] = jnp.exp(log_a)
        b_ref[:, cs] = (half_mult * xc) * (ti + 1.0)

    others = [lambda: u_ref.__setitem__(Ellipsis, proj(0, o1)),
              lambda: gg_ref.__setitem__(Ellipsis, _gelu(proj(o2, o3)).astype(BF16)),
              lambda: ga_ref.__setitem__(Ellipsis, _sigmoid_of_half(proj(o3, o4)).astype(BF16)),
              lambda: gb_ref.__setitem__(Ellipsis, _sigmoid_of_half(proj(o4, o4 + d)).astype(BF16))]
    ntile = wr_ref.shape[0]
    for j in range(max(ntile, len(others))):
        if j < ntile:
            lru_gates(j)
        if j < len(others):
            others[j]()


def _inproj(x, g, perm, w_in_b, lrup, s5w, lruw):
    nb, seq, d = x.shape
    tq = TIME_CHUNK
    rows = nb * tq
    t = nb * seq
    row = lambda c: pl.BlockSpec((rows, c), lambda i: (i, 0))
    return pl.pallas_call(
        functools.partial(_inproj_kernel, s5w=s5w, lruw=lruw),
        grid=(seq // tq,),
        in_specs=[pl.BlockSpec((nb, tq, d), lambda i: (0, i, 0)), _full(g), _full(perm), _full(w_in_b)]
        + [_full(p) for p in lrup],
        out_specs=[row(s5w), row(lruw), row(lruw), row(lruw), row(d), row(d)],
        out_shape=[jax.ShapeDtypeStruct((t, s5w), F32), jax.ShapeDtypeStruct((t, lruw), F32),
                   jax.ShapeDtypeStruct((t, lruw), F32), jax.ShapeDtypeStruct((t, lruw), BF16),
                   jax.ShapeDtypeStruct((t, d), BF16), jax.ShapeDtypeStruct((t, d), BF16)],
        scratch_shapes=[pltpu.VMEM(((CONV_WIDTH - 1) * nb, lruw), F32)],
        compiler_params=_params(),
        name="inproj",
    )(x, g, perm, w_in_b, *lrup)


def _mixers_kernel(u_ref, a_ref, b_ref, gg_ref, ga_ref, gb_ref,
                   bm_ref, lr_ref, li_ref, cm_ref, gm_ref, dk_ref, wo_ref, wlo_ref,
                   o_ref, state_ref, sbuf_ref, uprev_ref, h_ref, hbuf_ref, *, tq, nb):
    d = o_ref.shape[-1]
    s5w = u_ref.shape[-1]
    npair = bm_ref.shape[0]
    kper = LANES // (2 * S5_GROUP)
    nk = tq // 2
    half = nk * nb

    @pl.when(pl.program_id(0) == 0)
    def _():
        state_ref[...] = jnp.zeros_like(state_ref)
        uprev_ref[...] = jnp.zeros_like(uprev_ref)
        h_ref[...] = jnp.zeros_like(h_ref)

    u2 = u_ref[...]
    u4 = u2.reshape(nk, 2, nb, s5w)
    u_e = u4[:, 0].reshape(half, s5w)
    u_o = u4[:, 1].reshape(half, s5w)
    u_po = jnp.concatenate([uprev_ref[...], u_o[:half - nb]], axis=0)
    uprev_ref[...] = u_o[half - nb:]
    ue_b = u_e.astype(BF16)
    upo_b = u_po.astype(BF16)
    for k in range(npair):
        cs = slice(LANES * (k // kper), LANES * (k // kper + 1))
        bu = jnp.dot(jnp.concatenate([ue_b[:, cs], upo_b[:, cs]], axis=1), bm_ref[k],
                     preferred_element_type=F32)
        sbuf_ref[2 * k] = bu[:, :LANES]
        sbuf_ref[2 * k + 1] = bu[:, LANES:]

    pairs_per_loop = 4
    for kk in range(npair // pairs_per_loop):
        k0 = kk * pairs_per_loop
        lr = [jnp.broadcast_to(lr_ref[k0 + j:k0 + j + 1, :], (nb, LANES)) for j in range(pairs_per_loop)]
        li = [jnp.broadcast_to(li_ref[k0 + j:k0 + j + 1, :], (nb, LANES)) for j in range(pairs_per_loop)]
        init = tuple(state_ref[2 * k0 + j] for j in range(2 * pairs_per_loop))

        def body(it, carry, k0=k0, lr=lr, li=li):
            carry = list(carry)
            for s in range(SCAN_UNROLL):
                rows = _tile(it * SCAN_UNROLL + s)
                for j in range(pairs_per_loop):
                    sre, sim = carry[2 * j], carry[2 * j + 1]
                    nre = lr[j] * sre - li[j] * sim + sbuf_ref[2 * (k0 + j), rows, :]
                    nim = lr[j] * sim + li[j] * sre + sbuf_ref[2 * (k0 + j) + 1, rows, :]
                    sbuf_ref[2 * (k0 + j), rows, :] = nre
                    sbuf_ref[2 * (k0 + j) + 1, rows, :] = nim
                    carry[2 * j], carry[2 * j + 1] = nre, nim
            return tuple(carry)

        fin = lax.fori_loop(0, nk // SCAN_UNROLL, body, init)
        for j in range(2 * pairs_per_loop):
            state_ref[2 * k0 + j] = fin[j]

    def scan_body(it, h):
        for s in range(SCAN_UNROLL):
            tr = _tile(it * SCAN_UNROLL + s)
            h = a_ref[tr, :] * h + b_ref[tr, :]
            hbuf_ref[tr, :] = h
        return h

    h_ref[...] = lax.fori_loop(0, tq // SCAN_UNROLL, scan_body, h_ref[...])

    ys_e, ys_o = [], []
    for kb in range(npair // kper):
        acc_e = acc_o = None
        for j in range(kper):
            k = kb * kper + j
            s = jnp.concatenate([sbuf_ref[2 * k], sbuf_ref[2 * k + 1]], axis=1).astype(BF16)
            p = jnp.dot(s, cm_ref[k], preferred_element_type=F32)
            acc_e = p[:, :LANES] if acc_e is None else acc_e + p[:, :LANES]
            acc_o = p[:, LANES:] if acc_o is None else acc_o + p[:, LANES:]
        ys_e.append(acc_e)
        ys_o.append(acc_o)
    y_e = jnp.concatenate(ys_e, axis=1) + dk_ref[...] * u_e
    y_o = (jnp.concatenate(ys_o, axis=1) + jnp.dot(u_o.astype(BF16), gm_ref[...], preferred_element_type=F32)
           + dk_ref[...] * u_o)
    y = jnp.stack([y_e.reshape(nk, nb, s5w), y_o.reshape(nk, nb, s5w)], axis=1).reshape(2 * half, s5w)
    z = jnp.dot(_gelu(y).astype(BF16), wo_ref[...], preferred_element_type=F32)
    ya = z[:, :d] * _sigmoid_of_half(z[:, d:])

    yl = hbuf_ref[...] * gg_ref[...].astype(F32)
    yb = jnp.dot(yl.astype(BF16), wlo_ref[...], preferred_element_type=F32)
    o_ref[...] = (ya * ga_ref[...].astype(F32) + yb * gb_ref[...].astype(F32)).astype(BF16)


def _mixers(u, a, b, gg, ga, gb, s5p, w_lru_out_b, nb):
    t, s5w = u.shape
    w = a.shape[-1]
    d = ga.shape[-1]
    tq = TIME_CHUNK
    rows = nb * tq
    nstate = s5p[0].shape[0] * MXU_DIM
    row = lambda c: pl.BlockSpec((rows, c), lambda i: (i, 0))
    params = tuple(s5p) + (w_lru_out_b,)
    return pl.pallas_call(
        functools.partial(_mixers_kernel, tq=tq, nb=nb),
        grid=(t // rows,),
        in_specs=[row(s5w), row(w), row(w), row(w), row(d), row(d)] + [_full(p) for p in params],
        out_specs=row(d),
        out_shape=jax.ShapeDtypeStruct((t, d), BF16),
        scratch_shapes=[pltpu.VMEM((nstate // LANES, nb, LANES), F32),
                        pltpu.VMEM((nstate // LANES, rows // 2, LANES), F32),
                        pltpu.VMEM((nb, s5w), F32),
                        pltpu.VMEM((nb, w), F32),
                        pltpu.VMEM((rows, w), F32)],
        compiler_params=_params(),
        name="mixers",
    )(u, a, b, gg, ga, gb, *params)


def _mixroute_kernel(x_ref, m_ref, pt_ref, wo_ref, g_ref, wh_ref, wl_ref, rb_ref,
                     x1_ref, h2_ref, lt_ref):
    nb, tq, d = x_ref.shape
    tm = nb * tq
    m = jnp.dot(pt_ref[...], m_ref[...], preferred_element_type=F32).astype(BF16)
    x1 = x_ref[...].reshape(tm, d) + jnp.dot(m, wo_ref[...], preferred_element_type=F32)
    x1_ref[...] = x1
    h = _rms(x1, g_ref[...])

    hh = h.astype(BF16)
    h2_ref[...] = hh
    hl = (h - hh.astype(F32)).astype(BF16)
    dn = (((1,), (1,)), ((), ()))
    lt = (lax.dot_general(wh_ref[...], hh, dn, preferred_element_type=F32)
          + lax.dot_general(wh_ref[...], hl, dn, preferred_element_type=F32)
          + lax.dot_general(wl_ref[...], hh, dn, preferred_element_type=F32))
    lt_ref[...] = lt + rb_ref[:, 0:1]


def _mixroute(x, m, perm_t, w_o_b, g, wr_hi, wr_lo, rbias):
    nb, seq, d = x.shape
    tq = TIME_CHUNK
    tm = nb * tq
    t = nb * seq
    row = lambda c: pl.BlockSpec((tm, c), lambda i: (i, 0))
    return pl.pallas_call(
        _mixroute_kernel,
        grid=(seq // tq,),
        in_specs=[pl.BlockSpec((nb, tq, d), lambda i: (0, i, 0)), row(d), _full(perm_t), _full(w_o_b),
                  _full(g), _full(wr_hi), _full(wr_lo), _full(rbias)],
        out_specs=[row(d), row(d), pl.BlockSpec((ROUTER_ROWS, tm), lambda i: (0, i))],
        out_shape=[jax.ShapeDtypeStruct((t, d), F32), jax.ShapeDtypeStruct((t, d), BF16),
                   jax.ShapeDtypeStruct((ROUTER_ROWS, t), F32)],
        compiler_params=_params(),
        name="mixroute",
    )(x, m, perm_t, w_o_b, g, wr_hi, wr_lo, rbias)


def _route_kernel(lt_ref, tri_ref, ltri_ref, gate_ref, pos_ref, cnt_ref, *, tile):
    lt = lt_ref[...]
    tm = lt.shape[1]
    gl = lt[0:N_GROUPS, :]
    gmax = jnp.max(gl, axis=0, keepdims=True)
    gsum = jnp.sum(jnp.exp(gl - gmax), axis=0, keepdims=True)
    p_g = 1.0 / gsum
    iota_g = lax.broadcasted_iota(jnp.int32, gl.shape, 0)
    g_idx = jnp.min(jnp.where(gl == gmax, iota_g, N_GROUPS), axis=0, keepdims=True)

    el = lt[SUBLANES:SUBLANES + EXPERTS_PER_GROUP, :]
    for g in range(1, N_GROUPS):
        lo = SUBLANES + EXPERTS_PER_GROUP * g
        el = jnp.where(g_idx == g, lt[lo:lo + EXPERTS_PER_GROUP, :], el)
    emax = jnp.max(el, axis=0, keepdims=True)
    esum = jnp.sum(jnp.exp(el - emax), axis=0, keepdims=True)
    iota_e = lax.broadcasted_iota(jnp.int32, el.shape, 0)
    i1 = jnp.min(jnp.where(el == emax, iota_e, EXPERTS_PER_GROUP), axis=0, keepdims=True)
    el2 = jnp.where(iota_e == i1, -jnp.inf, el)
    emax2 = jnp.max(el2, axis=0, keepdims=True)
    i2 = jnp.min(jnp.where(el2 == emax2, iota_e, EXPERTS_PER_GROUP), axis=0, keepdims=True)
    p1 = 1.0 / esum
    p2 = jnp.exp(emax2 - emax) / esum
    psum = p1 + p2
    g1 = p_g * (p1 / psum)
    g2 = p_g * (p2 / psum)
    e1 = g_idx * EXPERTS_PER_GROUP + i1
    e2 = g_idx * EXPERTS_PER_GROUP + i2
    gate_ref[...] = jnp.concatenate([g1, g2, jnp.zeros((SUBLANES - 2, tm), F32)], axis=0)

    dn = (((1,), (1,)), ((), ()))
    iota = lax.broadcasted_iota(jnp.int32, (N_EXPERTS, tile), 0)
    iota_l = lax.broadcasted_iota(jnp.int32, (LANES, tile), 0)
    ones = jnp.ones((SUBLANES, tile), BF16)
    for s in range(tm // tile):
        cs = slice(s * tile, (s + 1) * tile)
        a1, a2 = e1[:, cs], e2[:, cs]
        oh0 = iota == a1
        oh1 = iota == a2
        c = jnp.where(oh0 | oh1, 1.0, 0.0)
        ci = jnp.sum(c, axis=1, keepdims=True).astype(jnp.int32)
        hi = jnp.broadcast_to((ci >> 5).astype(F32), (N_EXPERTS, LANES)).astype(BF16)
        lo = jnp.broadcast_to((ci & 31).astype(F32), (N_EXPERTS, LANES)).astype(BF16)
        first = (32.0 * jnp.dot(ltri_ref[...], hi, preferred_element_type=F32)
                 + jnp.dot(ltri_ref[...], lo, preferred_element_type=F32))
        before = jnp.dot(c.astype(BF16), tri_ref[...], preferred_element_type=F32) + first[:, 0:1]
        p0 = jnp.sum(jnp.where(oh0, before, 0.0), axis=0, keepdims=True)
        p1 = jnp.sum(jnp.where(oh1, before, 0.0), axis=0, keepdims=True)
        pos_ref[:, cs] = jnp.concatenate([p0, p1, jnp.zeros((SUBLANES - 2, tile), F32)], axis=0)
        hits = jnp.where((iota_l == a1) | (iota_l == a2), 1.0, 0.0).astype(BF16)
        cnt_ref[SUBLANES * s:SUBLANES * (s + 1), :] = lax.dot_general(ones, hits, dn, preferred_element_type=F32)


def _route(lt, tri, ltri):
    t = lt.shape[1]
    tile = tri.shape[0]
    tm = min(t, ROUTE_WIDTH)
    col = pl.BlockSpec((SUBLANES, tm), lambda i: (0, i))
    return pl.pallas_call(
        functools.partial(_route_kernel, tile=tile),
        grid=(t // tm,),
        in_specs=[pl.BlockSpec((ROUTER_ROWS, tm), lambda i: (0, i)), _full(tri), _full(ltri)],
        out_specs=[col, col, pl.BlockSpec((tm // tile * SUBLANES, LANES), lambda i: (i, 0))],
        out_shape=[jax.ShapeDtypeStruct((SUBLANES, t), F32), jax.ShapeDtypeStruct((SUBLANES, t), F32),
                   jax.ShapeDtypeStruct((t // tile * SUBLANES, LANES), F32)],
        compiler_params=_params(),
        name="route",
    )(lt, tri, ltri)


def _tokens(ref, tok, n):
    return ref.at[pl.ds(pl.multiple_of(tok * TOKEN_ROWS, TOKEN_ROWS), n * TOKEN_ROWS)]


def _pack_tokens(ref, v):
    n, d = v.shape
    half = d // 2
    bits = lax.bitcast_convert_type(v, U32)
    w = (bits[:, :half] >> 16) | (bits[:, half:] & jnp.uint32(0xFFFF0000))
    for j in range(TOKEN_ROWS):
        ref[pl.ds(j, n, stride=TOKEN_ROWS), :] = w[:, LANES * j:LANES * (j + 1)]


def _unpack_tokens(ref, n):
    ws = [ref[pl.ds(j, n, stride=TOKEN_ROWS), :] for j in range(TOKEN_ROWS)]
    lo = [lax.bitcast_convert_type(w << 16, F32) for w in ws]
    hi = [lax.bitcast_convert_type(w & jnp.uint32(0xFFFF0000), F32) for w in ws]
    return jnp.concatenate(lo + hi, axis=1)


def _strip_copies(n, src, src_tok, dst, dst_tok, sem, max_tokens, wait=False):
    del max_tokens

    @pl.when(n > 0)
    def _():
        cp = pltpu.make_async_copy(_tokens(src, src_tok, n), _tokens(dst, dst_tok, n), sem)
        if wait:
            cp.wait()
        else:
            cp.start()


def _dispatch_kernel(cnt_ref, off_ref, gst_ref, tot_ref, pst_ref, pen_ref,
                     h2_ref, pos_ref, xs_ref,
                     stage0_ref, stage1_ref, zero_ref, sem, zsem):
    i = pl.program_id(0)
    last = pl.num_programs(0) - 1
    tm = pos_ref.shape[1]
    na = 2 * tm
    d = h2_ref.shape[1]

    p0i = pos_ref[0:1, :].astype(jnp.int32)
    p1i = pos_ref[1:2, :].astype(jnp.int32)
    rid = lax.broadcasted_iota(jnp.int32, (na, tm), 0)
    onehot = jnp.where((rid == p0i) | (rid == p1i), 1.0, 0.0).astype(BF16)
    srt = jnp.dot(onehot, h2_ref[...], preferred_element_type=F32)

    def step(stage, s):
        @pl.when(i >= 2)
        def _():
            pltpu.make_async_copy(stage, _tokens(xs_ref, 0, na), sem.at[s]).wait()

        _pack_tokens(stage, srt)

        def per_expert(e, c):
            k = i * N_EXPERTS + e
            _strip_copies(cnt_ref[k], stage, off_ref[k], xs_ref, gst_ref[k], sem.at[s], na)
            return c

        lax.fori_loop(0, N_EXPERTS, per_expert, 0)

    @pl.when(i % 2 == 0)
    def _():
        step(stage0_ref, 0)

    @pl.when(i % 2 == 1)
    def _():
        step(stage1_ref, 1)

    @pl.when(i == last)
    def _():
        zero_ref[...] = jnp.zeros_like(zero_ref)
        for wait in (False, True):
            for e in range(N_EXPERTS):
                _strip_copies(pen_ref[e] - pst_ref[e] - tot_ref[e], zero_ref, 0, xs_ref,
                              pst_ref[e] + tot_ref[e], zsem, SLOT_BLOCK - 1, wait=wait)

        nblk = xs_ref.shape[0] // zero_ref.shape[0]

        def bcopy(b):
            return pltpu.make_async_copy(zero_ref, _tokens(xs_ref, b * SLOT_BLOCK, SLOT_BLOCK), zsem)

        first_unused = pen_ref[N_EXPERTS - 1] // SLOT_BLOCK
        lax.fori_loop(first_unused, nblk, lambda b, c: (bcopy(b).start(), c)[1], 0)
        lax.fori_loop(first_unused, nblk, lambda b, c: (bcopy(b).wait(), c)[1], 0)

        @pl.when(i >= 1)
        def _():
            @pl.when(i % 2 == 0)
            def _():
                pltpu.make_async_copy(stage1_ref, _tokens(xs_ref, 0, na), sem.at[1]).wait()

            @pl.when(i % 2 == 1)
            def _():
                pltpu.make_async_copy(stage0_ref, _tokens(xs_ref, 0, na), sem.at[0]).wait()

        @pl.when(i % 2 == 0)
        def _():
            pltpu.make_async_copy(stage0_ref, _tokens(xs_ref, 0, na), sem.at[0]).wait()

        @pl.when(i % 2 == 1)
        def _():
            pltpu.make_async_copy(stage1_ref, _tokens(xs_ref, 0, na), sem.at[1]).wait()


def _dispatch(cnt_t, off_t, gst_t, tot, pst, pen, h2, pos, tm, n_slots):
    sub = SUBLANES
    tr = TOKEN_ROWS
    t, d = h2.shape
    assert d == 2 * tr * LANES
    grid_spec = pltpu.PrefetchScalarGridSpec(
        num_scalar_prefetch=6,
        grid=(t // tm,),
        in_specs=[pl.BlockSpec((tm, d), lambda i, *_: (i, 0)),
                  pl.BlockSpec((sub, tm), lambda i, *_: (0, i))],
        out_specs=pl.BlockSpec(memory_space=pl.ANY),
        scratch_shapes=[pltpu.VMEM((2 * tm * tr, LANES), U32), pltpu.VMEM((2 * tm * tr, LANES), U32),
                        pltpu.VMEM((SLOT_BLOCK * tr, LANES), U32), pltpu.SemaphoreType.DMA((2,)),
                        pltpu.SemaphoreType.DMA(())],
    )
    return pl.pallas_call(
        _dispatch_kernel,
        grid_spec=grid_spec,
        out_shape=jax.ShapeDtypeStruct((n_slots * tr, LANES), U32),
        compiler_params=_params(),
        name="dispatch",
    )(cnt_t, off_t, gst_t, tot, pst, pen, h2, pos)


def _ffn_kernel(be_ref, nu_ref, xs_ref, wg_ref, wu_ref, wd_ref, ys_ref, wgb_ref, wub_ref, wdb_ref, *, blk):
    i = pl.program_id(0)

    @pl.when(i < nu_ref[0])
    def _():
        prev = be_ref[jnp.maximum(i - 1, 0)]

        @pl.when((i == 0) | (be_ref[i] != prev))
        def _():
            wgb_ref[...] = wg_ref[...].astype(BF16)
            wub_ref[...] = wu_ref[...].astype(BF16)
            wdb_ref[...] = wd_ref[...].astype(BF16)

        x = _unpack_tokens(xs_ref, blk).astype(BF16)
        g = jnp.dot(x, wgb_ref[...], preferred_element_type=F32)
        u = jnp.dot(x, wub_ref[...], preferred_element_type=F32)
        hg = 0.5 * g
        a = (hg * jnp.tanh(hg) + hg) * u
        y = jnp.dot(a.astype(BF16), wdb_ref[...], preferred_element_type=F32)
        _pack_tokens(ys_ref, y.astype(BF16).astype(F32))

    @pl.when(i >= nu_ref[0])
    def _():
        ys_ref[...] = jnp.zeros_like(ys_ref)


def _ffn(block_e, n_used, xs, wg, wu, wd):
    sub = TOKEN_ROWS
    blk = SLOT_BLOCK
    n_blocks = xs.shape[0] // (sub * blk)
    ne, d, de = wg.shape

    def slot_map(i, be, nu):
        return (i, 0)

    def w_map(i, be, nu):
        return (be[jnp.minimum(i, nu[0] - 1)], 0, 0)

    grid_spec = pltpu.PrefetchScalarGridSpec(
        num_scalar_prefetch=2,
        grid=(n_blocks,),
        in_specs=[pl.BlockSpec((blk * sub, LANES), slot_map),
                  pl.BlockSpec((None, d, de), w_map), pl.BlockSpec((None, d, de), w_map),
                  pl.BlockSpec((None, de, d), w_map)],
        out_specs=pl.BlockSpec((blk * sub, LANES), slot_map),
        scratch_shapes=[pltpu.VMEM((d, de), BF16), pltpu.VMEM((d, de), BF16), pltpu.VMEM((de, d), BF16)],
    )
    return pl.pallas_call(
        functools.partial(_ffn_kernel, blk=blk),
        grid_spec=grid_spec,
        out_shape=jax.ShapeDtypeStruct(xs.shape, U32),
        compiler_params=_params(),
        name="ffn",
    )(block_e, n_used, xs, wg, wu, wd)


def _combine_kernel(cnt_ref, off_ref, gst_ref, x1_ref, pos_ref, gate_ref, ys_ref, g_ref, o_ref,
                    buf0_ref, buf1_ref, sem):
    i = pl.program_id(0)
    n = pl.num_programs(0)
    tm, d = x1_ref.shape
    na = 2 * tm
    sub = SUBLANES

    def fetch(tile, buf, s):
        def per_expert(e, c):
            k = tile * N_EXPERTS + e
            _strip_copies(cnt_ref[k], ys_ref, gst_ref[k], buf, off_ref[k], sem.at[s], na)
            return c

        lax.fori_loop(0, N_EXPERTS, per_expert, 0)

    @pl.when(i == 0)
    def _():
        fetch(0, buf0_ref, 0)

    @pl.when((i + 1 < n) & (i % 2 == 0))
    def _():
        fetch(i + 1, buf1_ref, 1)

    @pl.when((i + 1 < n) & (i % 2 == 1))
    def _():
        fetch(i + 1, buf0_ref, 0)

    prow = jnp.concatenate([pos_ref[0:2, :], jnp.zeros((sub - 2, tm), F32)], axis=0)
    pcol = jnp.concatenate([prow] * (LANES // sub), axis=0).T.astype(jnp.int32)
    lane = lax.broadcasted_iota(jnp.int32, (tm, na), 1)
    pick = jnp.where((lane == pcol[:, 0:1]) | (lane == pcol[:, 1:2]), 1.0, 0.0).astype(BF16)
    rid = lax.broadcasted_iota(jnp.int32, (na, tm), 0)
    p0 = pos_ref[0:1, :].astype(jnp.int32)
    p1 = pos_ref[1:2, :].astype(jnp.int32)
    grow = jnp.sum(jnp.where(rid == p0, gate_ref[0:1, :], 0.0) + jnp.where(rid == p1, gate_ref[1:2, :], 0.0),
                   axis=1, keepdims=True)

    def finish(buf, s):
        pltpu.make_async_copy(_tokens(ys_ref, 0, na), buf, sem.at[s]).wait()
        y = _unpack_tokens(buf, na)
        moe = jnp.dot(pick, (y * grow).astype(BF16), preferred_element_type=F32)
        x = x1_ref[...] + moe
        o_ref[...] = _rms(x, g_ref[...]).reshape(o_ref.shape)

    @pl.when(i % 2 == 0)
    def _():
        finish(buf0_ref, 0)

    @pl.when(i % 2 == 1)
    def _():
        finish(buf1_ref, 1)


def _combine(cnt_t, off_t, gst_t, x1, pos, gate, ys, g, nb, seq):
    t, d = x1.shape
    tq = TIME_CHUNK
    tm = nb * tq
    sub = SUBLANES
    grid_spec = pltpu.PrefetchScalarGridSpec(
        num_scalar_prefetch=3,
        grid=(t // tm,),
        in_specs=[pl.BlockSpec((tm, d), lambda i, *_: (i, 0)),
                  pl.BlockSpec((sub, tm), lambda i, *_: (0, i)),
                  pl.BlockSpec((sub, tm), lambda i, *_: (0, i)),
                  pl.BlockSpec(memory_space=pl.ANY),
                  _full(g)],
        out_specs=pl.BlockSpec((nb, tq, d), lambda i, *_: (0, i, 0)),
        scratch_shapes=[pltpu.VMEM((2 * tm * TOKEN_ROWS, LANES), U32),
                        pltpu.VMEM((2 * tm * TOKEN_ROWS, LANES), U32), pltpu.SemaphoreType.DMA((2,))],
    )
    return pl.pallas_call(
        _combine_kernel,
        grid_spec=grid_spec,
        out_shape=jax.ShapeDtypeStruct((nb, seq, d), F32),
        compiler_params=_params(),
        name="combine",
    )(cnt_t, off_t, gst_t, x1, pos, gate, ys, g)


def _s5_tables(lam_re, lam_im, log_dt, b_re, b_im, c_re, c_im):
    ng, p = lam_re.shape
    npair = ng // 2
    kper = LANES // (2 * S5_GROUP)
    lam = lax.complex(lam_re, lam_im)
    dt = jnp.exp(log_dt)[:, None]
    lam_bar = jnp.exp(lam * dt)
    b_bar = ((lam_bar - 1.0) / lam)[..., None] * lax.complex(b_re, b_im)
    cmat = lax.complex(c_re, c_im)
    lam2 = lam_bar * lam_bar
    lr = jnp.real(lam2).reshape(npair, 2 * p)
    li = jnp.imag(lam2).reshape(npair, 2 * p)
    eye2 = jnp.eye(2, dtype=F32)
    sel = jax.nn.one_hot(jnp.arange(npair) % kper, kper, dtype=F32)

    def in_tiles(bc):
        bv = jnp.stack([jnp.real(bc), jnp.imag(bc)])
        bv = bv.reshape(2, npair, 2, p, S5_GROUP).transpose(1, 2, 4, 0, 3)
        bblk = bv[:, :, :, :, None, :] * eye2[None, :, None, None, :, None]
        bblk = bblk.reshape(npair, 2 * S5_GROUP, 4 * p)
        return (sel[:, :, None, None] * bblk[:, None]).reshape(npair, LANES, 4 * p)

    def out_tiles(cc):
        cv = jnp.stack([jnp.real(cc), -jnp.imag(cc)])
        cv = cv.reshape(2, npair, 2, S5_GROUP, p).transpose(1, 0, 2, 4, 3)
        cblk = cv[:, :, :, :, None, :] * eye2[None, None, :, None, :, None]
        cblk = cblk.reshape(npair, 4 * p, 2 * S5_GROUP)
        return (cblk[:, :, None, :] * sel[:, None, :, None]).reshape(npair, 4 * p, LANES)

    bm2 = jnp.concatenate([in_tiles(b_bar), in_tiles(lam_bar[..., None] * b_bar)], axis=1)
    cm = jnp.concatenate([out_tiles(cmat), out_tiles(cmat * lam_bar[:, None, :])], axis=2)
    direct = jnp.real(jnp.einsum('gop,gpi->gio', cmat, b_bar))
    gmat = (direct[:, :, None, :] * jnp.eye(ng, dtype=F32)[:, None, :, None]).reshape(
        ng * S5_GROUP, ng * S5_GROUP)
    return bm2.astype(BF16), lr, li, cm.astype(BF16), gmat.astype(BF16)


def _blockdiag_tiles(w):
    nh, hi, ho = w.shape
    per = MXU_DIM // hi
    eye = jnp.eye(per, dtype=w.dtype)
    t = w.reshape(nh // per, per, hi, ho)[:, :, :, None, :] * eye[None, :, None, :, None]
    return t.reshape(nh // per, per * hi, per * ho)


def kernel(x, norm_mix, w_in, s5_lam_re, s5_lam_im, s5_log_dt, s5_b_re, s5_b_im, s5_c_re, s5_c_im, s5_d,
           w_s5_out, conv_w, conv_b, lru_w_r, lru_b_r, lru_w_i, lru_b_i, lru_lambda, w_lru_out, w_o, norm_ffn,
           router_group_w, router_group_b, router_expert_w, router_expert_b, expert_w_gate, expert_w_up,
           expert_w_down, norm_final):
    nb, seq, d = x.shape
    t = nb * seq
    assert w_in.shape[0] == 1, "one layer: the final RMSNorm is fused into the layer's combine kernel"
    l = 0
    s5w = s5_d.shape[-1]
    lruw = conv_b.shape[-1]
    rows = nb * TIME_CHUNK

    r = jnp.arange(rows)
    perm = jax.nn.one_hot((r % nb) * TIME_CHUNK + r // nb, rows, dtype=BF16)

    sp = ((-0.5 * LRU_C) * jax.nn.softplus(-lru_lambda[l]))[None, :]
    lrup = (conv_w[l], conv_b[l][None, :], _blockdiag_tiles(0.5 * lru_w_r[l]).astype(BF16),
            _blockdiag_tiles(0.5 * lru_w_i[l]).astype(BF16), 0.5 * lru_b_r[l][None, :],
            0.5 * lru_b_i[l][None, :], sp)
    gate_cols = jnp.arange(w_in.shape[-1]) >= s5w + 2 * lruw
    w_in_b = jnp.where(gate_cols[None, :], 0.5 * w_in[l], w_in[l]).astype(BF16)
    u, a, b, gg, ga, gb = _inproj(x, norm_mix[l][None, :], perm, w_in_b, lrup, s5w, lruw)

    bm, lr, li, cm, gmat = _s5_tables(s5_lam_re[l], s5_lam_im[l], s5_log_dt[l], s5_b_re[l], s5_b_im[l],
                                           s5_c_re[l], s5_c_im[l])
    glu_cols = jnp.arange(w_s5_out.shape[-1]) >= d
    w_s5_out_b = jnp.where(glu_cols[None, :], 0.5 * w_s5_out[l], w_s5_out[l]).astype(BF16)
    s5p = (bm, lr, li, cm, gmat, s5_d[l][None, :], w_s5_out_b)
    mixed = _mixers(u, a, b, gg, ga, gb, s5p, w_lru_out[l].astype(BF16), nb)

    gap = jnp.zeros((SUBLANES - N_GROUPS, d), F32)
    wrt = jnp.concatenate([router_group_w[l].T, gap, router_expert_w[l].T], axis=0)
    wr_hi = wrt.astype(BF16)
    wr_lo = (wrt - wr_hi.astype(F32)).astype(BF16)
    rb = jnp.concatenate([router_group_b[l], jnp.zeros((SUBLANES - N_GROUPS,), F32), router_expert_b[l]])
    rbias = jnp.broadcast_to(rb[:, None], (ROUTER_ROWS, LANES))

    x1, h2, logits = _mixroute(x, mixed, perm.T, w_o[l].astype(BF16), norm_ffn[l][None, :], wr_hi, wr_lo, rbias)
    tri = jnp.triu(jnp.ones((rows, rows), BF16), k=1)
    ltri = jnp.tril(jnp.ones((N_EXPERTS, N_EXPERTS), BF16), k=-1)
    gate, pos, tcnt = _route(logits, tri, ltri)

    n_tiles = seq // TIME_CHUNK
    cnt_t = tcnt.reshape(n_tiles, SUBLANES, LANES)[:, 0, :N_EXPERTS].astype(jnp.int32)
    off_t = jnp.cumsum(cnt_t, axis=1) - cnt_t
    counts = jnp.sum(cnt_t, axis=0)
    padded = ((counts + SLOT_BLOCK - 1) // SLOT_BLOCK) * SLOT_BLOCK
    pad_ends = jnp.cumsum(padded)
    pad_starts = pad_ends - padded
    gst_t = pad_starts[None, :] + jnp.cumsum(cnt_t, axis=0) - cnt_t
    n_blocks = -(-(2 * t) // SLOT_BLOCK) + N_EXPERTS
    n_slots = n_blocks * SLOT_BLOCK
    block_start = jnp.arange(n_blocks, dtype=jnp.int32) * SLOT_BLOCK
    block_e = jnp.minimum(jnp.sum((pad_ends[None, :] <= block_start[:, None]).astype(jnp.int32), axis=1),
                          N_EXPERTS - 1)
    n_used = pad_ends[-1:] // SLOT_BLOCK
    cnt_f, off_f, gst_f = cnt_t.reshape(-1), off_t.reshape(-1), gst_t.reshape(-1)

    xs = _dispatch(cnt_f, off_f, gst_f, counts, pad_starts, pad_ends, h2, pos, rows, n_slots)
    ys = _ffn(block_e, n_used, xs, expert_w_gate[l], expert_w_up[l], expert_w_down[l])
    return _combine(cnt_f, off_f, gst_f, x1, pos, gate, ys, norm_final[None, :], nb, seq)
```

```python
import functools
import math

import jax
import jax.numpy as jnp
from jax import lax
from jax.experimental import pallas as pl
from jax.experimental.pallas import tpu as pltpu

F32 = jnp.float32
BF16 = jnp.bfloat16
U32 = jnp.uint32

RMS_EPS = 1e-6
S5_GROUP = 16
CONV_WIDTH = 4
LRU_C = 8.0
N_GROUPS = 4
EXPERTS_PER_GROUP = 8
N_EXPERTS = N_GROUPS * EXPERTS_PER_GROUP

LANES = 128
SUBLANES = 8
TOKEN_ROWS = 4
MXU_DIM = 256
TIME_CHUNK = 64
SCAN_UNROLL = 4
SLOT_BLOCK = 512
ROUTE_WIDTH = 4096
ROUTER_ROWS = 40
VMEM_LIMIT = 56 * 1024 * 1024


def _gelu(x):
    c = math.sqrt(2.0 / math.pi)
    hx = 0.5 * x
    return hx * jnp.tanh(x * (c + (c * 0.044715) * (x * x))) + hx


def _sigmoid_of_half(xh):
    return 0.5 * jnp.tanh(xh) + 0.5


def _rms(x, g):
    return x * lax.rsqrt(jnp.mean(x * x, axis=-1, keepdims=True) + RMS_EPS) * g


def _params():
    return pltpu.CompilerParams(dimension_semantics=("arbitrary",), vmem_limit_bytes=VMEM_LIMIT)


def _full(a):
    return pl.BlockSpec(a.shape, lambda i, *_: (0,) * a.ndim)


def _tile(t):
    return pl.ds(pl.multiple_of(t * SUBLANES, SUBLANES), SUBLANES)


def _inproj_kernel(x_ref, g_ref, p_ref, w_ref, cw_ref, cb_ref, wr_ref, wi_ref, br_ref, bi_ref, sp_ref,
                   u_ref, a_ref, b_ref, gg_ref, ga_ref, gb_ref, xprev_ref, *, s5w, lruw):
    nb, tq, d = x_ref.shape
    rows = nb * tq
    halo = (CONV_WIDTH - 1) * nb

    @pl.when(pl.program_id(0) == 0)
    def _():
        xprev_ref[...] = jnp.zeros_like(xprev_ref)

    h = _rms(x_ref[...].reshape(rows, d), g_ref[...])
    hb = jnp.dot(p_ref[...], h.astype(BF16), preferred_element_type=F32).astype(BF16)

    def proj(lo, hi):
        return jnp.dot(hb, w_ref[:, lo:hi], preferred_element_type=F32)

    o1 = s5w
    o2 = o1 + lruw
    o3 = o2 + lruw
    o4 = o3 + d
    xin = proj(o1, o2)
    xext = jnp.concatenate([xprev_ref[...], xin], axis=0)
    xprev_ref[...] = xin[rows - halo:, :]
    xc_all = cb_ref[...]
    for k in range(CONV_WIDTH):
        xc_all = xc_all + xext[k * nb:k * nb + rows, :] * cw_ref[k:k + 1, :]

    def lru_gates(j):
        cs = slice(MXU_DIM * j, MXU_DIM * (j + 1))
        xc = xc_all[:, cs]
        xcb = xc.astype(BF16)
        tr = jnp.tanh(jnp.dot(xcb, wr_ref[j], preferred_element_type=F32) + br_ref[:, cs])
        ti = jnp.tanh(jnp.dot(xcb, wi_ref[j], preferred_element_type=F32) + bi_ref[:, cs])
        log_a = sp_ref[:, cs] * tr + sp_ref[:, cs]
        th = jnp.tanh(log_a)
        half_mult = jnp.sqrt((-0.5 * th) / (1.0 - th))
        a_ref[:, cs] = jnp.exp(log_a)
        b_ref[:, cs] = (half_mult * xc) * (ti + 1.0)

    others = [lambda: u_ref.__setitem__(Ellipsis, proj(0, o1)),
              lambda: gg_ref.__setitem__(Ellipsis, _gelu(proj(o2, o3)).astype(BF16)),
              lambda: ga_ref.__setitem__(Ellipsis, _sigmoid_of_half(proj(o3, o4)).astype(BF16)),
              lambda: gb_ref.__setitem__(Ellipsis, _sigmoid_of_half(proj(o4, o4 + d)).astype(BF16))]
    ntile = wr_ref.shape[0]
    for j in range(max(ntile, len(others))):
        if j < ntile:
            lru_gates(j)
        if j < len(others):
            others[j]()


def _inproj(x, g, perm, w_in_b, lrup, s5w, lruw):
    nb, seq, d = x.shape
    tq = TIME_CHUNK
    rows = nb * tq
    t = nb * seq
    row = lambda c: pl.BlockSpec((rows, c), lambda i: (i, 0))
    return pl.pallas_call(
        functools.partial(_inproj_kernel, s5w=s5w, lruw=lruw),
        grid=(seq // tq,),
        in_specs=[pl.BlockSpec((nb, tq, d), lambda i: (0, i, 0)), _full(g), _full(perm), _full(w_in_b)]
        + [_full(p) for p in lrup],
        out_specs=[row(s5w), row(lruw), row(lruw), row(lruw), row(d), row(d)],
        out_shape=[jax.ShapeDtypeStruct((t, s5w), F32), jax.ShapeDtypeStruct((t, lruw), F32),
                   jax.ShapeDtypeStruct((t, lruw), F32), jax.ShapeDtypeStruct((t, lruw), BF16),
                   jax.ShapeDtypeStruct((t, d), BF16), jax.ShapeDtypeStruct((t, d), BF16)],
        scratch_shapes=[pltpu.VMEM(((CONV_WIDTH - 1) * nb, lruw), F32)],
        compiler_params=_params(),
        name="inproj",
    )(x, g, perm, w_in_b, *lrup)


def _mixers_kernel(u_ref, a_ref, b_ref, gg_ref, ga_ref, gb_ref,
                   bm_ref, lr_ref, li_ref, cm_ref, gm_ref, dk_ref, wo_ref, wlo_ref,
                   o_ref, state_ref, sbuf_ref, uprev_ref, h_ref, hbuf_ref, *, tq, nb):
    d = o_ref.shape[-1]
    s5w = u_ref.shape[-1]
    npair = bm_ref.shape[0]
    kper = LANES // (2 * S5_GROUP)
    nk = tq // 2
    half = nk * nb

    @pl.when(pl.program_id(0) == 0)
    def _():
        state_ref[...] = jnp.zeros_like(state_ref)
        uprev_ref[...] = jnp.zeros_like(uprev_ref)
        h_ref[...] = jnp.zeros_like(h_ref)

    u2 = u_ref[...]
    u4 = u2.reshape(nk, 2, nb, s5w)
    u_e = u4[:, 0].reshape(half, s5w)
    u_o = u4[:, 1].reshape(half, s5w)
    u_po = jnp.concatenate([uprev_ref[...], u_o[:half - nb]], axis=0)
    uprev_ref[...] = u_o[half - nb:]
    ue_b = u_e.astype(BF16)
    upo_b = u_po.astype(BF16)
    for k in range(npair):
        cs = slice(LANES * (k // kper), LANES * (k // kper + 1))
        bu = jnp.dot(jnp.concatenate([ue_b[:, cs], upo_b[:, cs]], axis=1), bm_ref[k],
                     preferred_element_type=F32)
        sbuf_ref[2 * k] = bu[:, :LANES]
        sbuf_ref[2 * k + 1] = bu[:, LANES:]

    pairs_per_loop = 4
    for kk in range(npair // pairs_per_loop):
        k0 = kk * pairs_per_loop
        lr = [jnp.broadcast_to(lr_ref[k0 + j:k0 + j + 1, :], (nb, LANES)) for j in range(pairs_per_loop)]
        li = [jnp.broadcast_to(li_ref[k0 + j:k0 + j + 1, :], (nb, LANES)) for j in range(pairs_per_loop)]
        init = tuple(state_ref[2 * k0 + j] for j in range(2 * pairs_per_loop))

        def body(it, carry, k0=k0, lr=lr, li=li):
            carry = list(carry)
            for s in range(SCAN_UNROLL):
                rows = _tile(it * SCAN_UNROLL + s)
                for j in range(pairs_per_loop):
                    sre, sim = carry[2 * j], carry[2 * j + 1]
                    nre = lr[j] * sre - li[j] * sim + sbuf_ref[2 * (k0 + j), rows, :]
                    nim = lr[j] * sim + li[j] * sre + sbuf_ref[2 * (k0 + j) + 1, rows, :]
                    sbuf_ref[2 * (k0 + j), rows, :] = nre
                    sbuf_ref[2 * (k0 + j) + 1, rows, :] = nim
                    carry[2 * j], carry[2 * j + 1] = nre, nim
            return tuple(carry)

        fin = lax.fori_loop(0, nk // SCAN_UNROLL, body, init)
        for j in range(2 * pairs_per_loop):
            state_ref[2 * k0 + j] = fin[j]

    def scan_body(it, h):
        for s in range(SCAN_UNROLL):
            tr = _tile(it * SCAN_UNROLL + s)
            h = a_ref[tr, :] * h + b_ref[tr, :]
            hbuf_ref[tr, :] = h
        return h

    h_ref[...] = lax.fori_loop(0, tq // SCAN_UNROLL, scan_body, h_ref[...])

    ys_e, ys_o = [], []
    for kb in range(npair // kper):
        acc_e = acc_o = None
        for j in range(kper):
            k = kb * kper + j
            s = jnp.concatenate([sbuf_ref[2 * k], sbuf_ref[2 * k + 1]], axis=1).astype(BF16)
            p = jnp.dot(s, cm_ref[k], preferred_element_type=F32)
            acc_e = p[:, :LANES] if acc_e is None else acc_e + p[:, :LANES]
            acc_o = p[:, LANES:] if acc_o is None else acc_o + p[:, LANES:]
        ys_e.append(acc_e)
        ys_o.append(acc_o)
    y_e = jnp.concatenate(ys_e, axis=1) + dk_ref[...] * u_e
    y_o = (jnp.concatenate(ys_o, axis=1) + jnp.dot(u_o.astype(BF16), gm_ref[...], preferred_element_type=F32)
           + dk_ref[...] * u_o)
    y = jnp.stack([y_e.reshape(nk, nb, s5w), y_o.reshape(nk, nb, s5w)], axis=1).reshape(2 * half, s5w)
    z = jnp.dot(_gelu(y).astype(BF16), wo_ref[...], preferred_element_type=F32)
    ya = z[:, :d] * _sigmoid_of_half(z[:, d:])

    yl = hbuf_ref[...] * gg_ref[...].astype(F32)
    yb = jnp.dot(yl.astype(BF16), wlo_ref[...], preferred_element_type=F32)
    o_ref[...] = (ya * ga_ref[...].astype(F32) + yb * gb_ref[...].astype(F32)).astype(BF16)


def _mixers(u, a, b, gg, ga, gb, s5p, w_lru_out_b, nb):
    t, s5w = u.shape
    w = a.shape[-1]
    d = ga.shape[-1]
    tq = TIME_CHUNK
    rows = nb * tq
    nstate = s5p[0].shape[0] * MXU_DIM
    row = lambda c: pl.BlockSpec((rows, c), lambda i: (i, 0))
    params = tuple(s5p) + (w_lru_out_b,)
    return pl.pallas_call(
        functools.partial(_mixers_kernel, tq=tq, nb=nb),
        grid=(t // rows,),
        in_specs=[row(s5w), row(w), row(w), row(w), row(d), row(d)] + [_full(p) for p in params],
        out_specs=row(d),
        out_shape=jax.ShapeDtypeStruct((t, d), BF16),
        scratch_shapes=[pltpu.VMEM((nstate // LANES, nb, LANES), F32),
                        pltpu.VMEM((nstate // LANES, rows // 2, LANES), F32),
                        pltpu.VMEM((nb, s5w), F32),
                        pltpu.VMEM((nb, w), F32),
                        pltpu.VMEM((rows, w), F32)],
        compiler_params=_params(),
        name="mixers",
    )(u, a, b, gg, ga, gb, *params)


def _mixroute_kernel(x_ref, m_ref, pt_ref, wo_ref, g_ref, wh_ref, wl_ref, rb_ref,
                     x1_ref, h2_ref, lt_ref):
    nb, tq, d = x_ref.shape
    tm = nb * tq
    m = jnp.dot(pt_ref[...], m_ref[...], preferred_element_type=F32).astype(BF16)
    x1 = x_ref[...].reshape(tm, d) + jnp.dot(m, wo_ref[...], preferred_element_type=F32)
    x1_ref[...] = x1
    h = _rms(x1, g_ref[...])

    hh = h.astype(BF16)
    h2_ref[...] = hh
    hl = (h - hh.astype(F32)).astype(BF16)
    dn = (((1,), (1,)), ((), ()))
    lt = (lax.dot_general(wh_ref[...], hh, dn, preferred_element_type=F32)
          + lax.dot_general(wh_ref[...], hl, dn, preferred_element_type=F32)
          + lax.dot_general(wl_ref[...], hh, dn, preferred_element_type=F32))
    lt_ref[...] = lt + rb_ref[:, 0:1]


def _mixroute(x, m, perm_t, w_o_b, g, wr_hi, wr_lo, rbias):
    nb, seq, d = x.shape
    tq = TIME_CHUNK
    tm = nb * tq
    t = nb * seq
    row = lambda c: pl.BlockSpec((tm, c), lambda i: (i, 0))
    return pl.pallas_call(
        _mixroute_kernel,
        grid=(seq // tq,),
        in_specs=[pl.BlockSpec((nb, tq, d), lambda i: (0, i, 0)), row(d), _full(perm_t), _full(w_o_b),
                  _full(g), _full(wr_hi), _full(wr_lo), _full(rbias)],
        out_specs=[row(d), row(d), pl.BlockSpec((ROUTER_ROWS, tm), lambda i: (0, i))],
        out_shape=[jax.ShapeDtypeStruct((t, d), F32), jax.ShapeDtypeStruct((t, d), BF16),
                   jax.ShapeDtypeStruct((ROUTER_ROWS, t), F32)],
        compiler_params=_params(),
        name="mixroute",
    )(x, m, perm_t, w_o_b, g, wr_hi, wr_lo, rbias)


def _route_kernel(lt_ref, tri_ref, ltri_ref, gate_ref, pos_ref, cnt_ref, *, tile):
    lt = lt_ref[...]
    tm = lt.shape[1]
    gl = lt[0:N_GROUPS, :]
    gmax = jnp.max(gl, axis=0, keepdims=True)
    gsum = jnp.sum(jnp.exp(gl - gmax), axis=0, keepdims=True)
    p_g = 1.0 / gsum
    iota_g = lax.broadcasted_iota(jnp.int32, gl.shape, 0)
    g_idx = jnp.min(jnp.where(gl == gmax, iota_g, N_GROUPS), axis=0, keepdims=True)

    el = lt[SUBLANES:SUBLANES + EXPERTS_PER_GROUP, :]
    for g in range(1, N_GROUPS):
        lo = SUBLANES + EXPERTS_PER_GROUP * g
        el = jnp.where(g_idx == g, lt[lo:lo + EXPERTS_PER_GROUP, :], el)
    emax = jnp.max(el, axis=0, keepdims=True)
    esum = jnp.sum(jnp.exp(el - emax), axis=0, keepdims=True)
    iota_e = lax.broadcasted_iota(jnp.int32, el.shape, 0)
    i1 = jnp.min(jnp.where(el == emax, iota_e, EXPERTS_PER_GROUP), axis=0, keepdims=True)
    el2 = jnp.where(iota_e == i1, -jnp.inf, el)
    emax2 = jnp.max(el2, axis=0, keepdims=True)
    i2 = jnp.min(jnp.where(el2 == emax2, iota_e, EXPERTS_PER_GROUP), axis=0, keepdims=True)
    p1 = 1.0 / esum
    p2 = jnp.exp(emax2 - emax) / esum
    psum = p1 + p2
    g1 = p_g * (p1 / psum)
    g2 = p_g * (p2 / psum)
    e1 = g_idx * EXPERTS_PER_GROUP + i1
    e2 = g_idx * EXPERTS_PER_GROUP + i2
    gate_ref[...] = jnp.concatenate([g1, g2, jnp.zeros((SUBLANES - 2, tm), F32)], axis=0)

    dn = (((1,), (1,)), ((), ()))
    iota = lax.broadcasted_iota(jnp.int32, (N_EXPERTS, tile), 0)
    iota_l = lax.broadcasted_iota(jnp.int32, (LANES, tile), 0)
    ones = jnp.ones((SUBLANES, tile), BF16)
    for s in range(tm // tile):
        cs = slice(s * tile, (s + 1) * tile)
        a1, a2 = e1[:, cs], e2[:, cs]
        oh0 = iota == a1
        oh1 = iota == a2
        c = jnp.where(oh0 | oh1, 1.0, 0.0)
        ci = jnp.sum(c, axis=1, keepdims=True).astype(jnp.int32)
        hi = jnp.broadcast_to((ci >> 5).astype(F32), (N_EXPERTS, LANES)).astype(BF16)
        lo = jnp.broadcast_to((ci & 31).astype(F32), (N_EXPERTS, LANES)).astype(BF16)
        first = (32.0 * jnp.dot(ltri_ref[...], hi, preferred_element_type=F32)
                 + jnp.dot(ltri_ref[...], lo, preferred_element_type=F32))
        before = jnp.dot(c.astype(BF16), tri_ref[...], preferred_element_type=F32) + first[:, 0:1]
        p0 = jnp.sum(jnp.where(oh0, before, 0.0), axis=0, keepdims=True)
        p1 = jnp.sum(jnp.where(oh1, before, 0.0), axis=0, keepdims=True)
        pos_ref[:, cs] = jnp.concatenate([p0, p1, jnp.zeros((SUBLANES - 2, tile), F32)], axis=0)
        hits = jnp.where((iota_l == a1) | (iota_l == a2), 1.0, 0.0).astype(BF16)
        cnt_ref[SUBLANES * s:SUBLANES * (s + 1), :] = lax.dot_general(ones, hits, dn, preferred_element_type=F32)


def _route(lt, tri, ltri):
    t = lt.shape[1]
    tile = tri.shape[0]
    tm = min(t, ROUTE_WIDTH)
    col = pl.BlockSpec((SUBLANES, tm), lambda i: (0, i))
    return pl.pallas_call(
        functools.partial(_route_kernel, tile=tile),
        grid=(t // tm,),
        in_specs=[pl.BlockSpec((ROUTER_ROWS, tm), lambda i: (0, i)), _full(tri), _full(ltri)],
        out_specs=[col, col, pl.BlockSpec((tm // tile * SUBLANES, LANES), lambda i: (i, 0))],
        out_shape=[jax.ShapeDtypeStruct((SUBLANES, t), F32), jax.ShapeDtypeStruct((SUBLANES, t), F32),
                   jax.ShapeDtypeStruct((t // tile * SUBLANES, LANES), F32)],
        compiler_params=_params(),
        name="route",
    )(lt, tri, ltri)


def _tokens(ref, tok, n):
    return ref.at[pl.ds(pl.multiple_of(tok * TOKEN_ROWS, TOKEN_ROWS), n * TOKEN_ROWS)]


def _pack_tokens(ref, v):
    n, d = v.shape
    half = d // 2
    bits = lax.bitcast_convert_type(v, U32)
    w = (bits[:, :half] >> 16) | (bits[:, half:] & jnp.uint32(0xFFFF0000))
    for j in range(TOKEN_ROWS):
        ref[pl.ds(j, n, stride=TOKEN_ROWS), :] = w[:, LANES * j:LANES * (j + 1)]


def _unpack_tokens(ref, n):
    ws = [ref[pl.ds(j, n, stride=TOKEN_ROWS), :] for j in range(TOKEN_ROWS)]
    lo = [lax.bitcast_convert_type(w << 16, F32) for w in ws]
    hi = [lax.bitcast_convert_type(w & jnp.uint32(0xFFFF0000), F32) for w in ws]
    return jnp.concatenate(lo + hi, axis=1)


def _strip_copies(n, src, src_tok, dst, dst_tok, sem, max_tokens, wait=False):
    del max_tokens

    @pl.when(n > 0)
    def _():
        cp = pltpu.make_async_copy(_tokens(src, src_tok, n), _tokens(dst, dst_tok, n), sem)
        if wait:
            cp.wait()
        else:
            cp.start()


def _dispatch_kernel(cnt_ref, off_ref, gst_ref, tot_ref, pst_ref, pen_ref,
                     h2_ref, pos_ref, xs_ref,
                     stage0_ref, stage1_ref, zero_ref, sem, zsem):
    i = pl.program_id(0)
    last = pl.num_programs(0) - 1
    tm = pos_ref.shape[1] // 2
    na = 2 * tm
    rid = lax.broadcasted_iota(jnp.int32, (na, tm), 0)

    for s, stage in enumerate((stage0_ref, stage1_ref)):
        tile = 2 * i + s
        cs = slice(s * tm, (s + 1) * tm)
        p0i = pos_ref[0:1, cs].astype(jnp.int32)
        p1i = pos_ref[1:2, cs].astype(jnp.int32)
        onehot = jnp.where((rid == p0i) | (rid == p1i), 1.0, 0.0).astype(BF16)
        srt = jnp.dot(onehot, h2_ref[cs, :], preferred_element_type=F32)

        @pl.when(i >= 1)
        def _(stage=stage, s=s):
            pltpu.make_async_copy(stage, _tokens(xs_ref, 0, na), sem.at[s]).wait()

        _pack_tokens(stage, srt)

        def per_expert(e, c, stage=stage, s=s, tile=tile):
            k = tile * N_EXPERTS + e
            _strip_copies(cnt_ref[k], stage, off_ref[k], xs_ref, gst_ref[k], sem.at[s], na)
            return c

        lax.fori_loop(0, N_EXPERTS, per_expert, 0)

    @pl.when(i == last)
    def _():
        zero_ref[...] = jnp.zeros_like(zero_ref)
        for wait in (False, True):
            for e in range(N_EXPERTS):
                _strip_copies(pen_ref[e] - pst_ref[e] - tot_ref[e], zero_ref, 0, xs_ref,
                              pst_ref[e] + tot_ref[e], zsem, SLOT_BLOCK - 1, wait=wait)

        nblk = xs_ref.shape[0] // zero_ref.shape[0]

        def bcopy(b):
            return pltpu.make_async_copy(zero_ref, _tokens(xs_ref, b * SLOT_BLOCK, SLOT_BLOCK), zsem)

        first_unused = pen_ref[N_EXPERTS - 1] // SLOT_BLOCK
        lax.fori_loop(first_unused, nblk, lambda b, c: (bcopy(b).start(), c)[1], 0)
        lax.fori_loop(first_unused, nblk, lambda b, c: (bcopy(b).wait(), c)[1], 0)

        pltpu.make_async_copy(stage0_ref, _tokens(xs_ref, 0, na), sem.at[0]).wait()
        pltpu.make_async_copy(stage1_ref, _tokens(xs_ref, 0, na), sem.at[1]).wait()


def _dispatch(cnt_t, off_t, gst_t, tot, pst, pen, h2, pos, tm, n_slots):
    sub = SUBLANES
    tr = TOKEN_ROWS
    t, d = h2.shape
    assert d == 2 * tr * LANES
    grid_spec = pltpu.PrefetchScalarGridSpec(
        num_scalar_prefetch=6,
        grid=(t // (2 * tm),),
        in_specs=[pl.BlockSpec((2 * tm, d), lambda i, *_: (i, 0)),
                  pl.BlockSpec((sub, 2 * tm), lambda i, *_: (0, i))],
        out_specs=pl.BlockSpec(memory_space=pl.ANY),
        scratch_shapes=[pltpu.VMEM((2 * tm * tr, LANES), U32), pltpu.VMEM((2 * tm * tr, LANES), U32),
                        pltpu.VMEM((SLOT_BLOCK * tr, LANES), U32), pltpu.SemaphoreType.DMA((2,)),
                        pltpu.SemaphoreType.DMA(())],
    )
    return pl.pallas_call(
        _dispatch_kernel,
        grid_spec=grid_spec,
        out_shape=jax.ShapeDtypeStruct((n_slots * tr, LANES), U32),
        compiler_params=_params(),
        name="dispatch",
    )(cnt_t, off_t, gst_t, tot, pst, pen, h2, pos)


def _ffn_kernel(be_ref, nu_ref, xs_ref, wg_ref, wu_ref, wd_ref, ys_ref, wgb_ref, wub_ref, wdb_ref, *, blk):
    i = pl.program_id(0)

    @pl.when(i < nu_ref[0])
    def _():
        prev = be_ref[jnp.maximum(i - 1, 0)]

        @pl.when((i == 0) | (be_ref[i] != prev))
        def _():
            wgb_ref[...] = wg_ref[...].astype(BF16)
            wub_ref[...] = wu_ref[...].astype(BF16)
            wdb_ref[...] = wd_ref[...].astype(BF16)

        x = _unpack_tokens(xs_ref, blk).astype(BF16)
        g = jnp.dot(x, wgb_ref[...], preferred_element_type=F32)
        u = jnp.dot(x, wub_ref[...], preferred_element_type=F32)
        hg = 0.5 * g
        a = (hg * jnp.tanh(hg) + hg) * u
        y = jnp.dot(a.astype(BF16), wdb_ref[...], preferred_element_type=F32)
        _pack_tokens(ys_ref, y.astype(BF16).astype(F32))

    @pl.when(i >= nu_ref[0])
    def _():
        ys_ref[...] = jnp.zeros_like(ys_ref)


def _ffn(block_e, n_used, xs, wg, wu, wd):
    sub = TOKEN_ROWS
    blk = SLOT_BLOCK
    n_blocks = xs.shape[0] // (sub * blk)
    ne, d, de = wg.shape

    def slot_map(i, be, nu):
        return (i, 0)

    def w_map(i, be, nu):
        return (be[jnp.minimum(i, nu[0] - 1)], 0, 0)

    grid_spec = pltpu.PrefetchScalarGridSpec(
        num_scalar_prefetch=2,
        grid=(n_blocks,),
        in_specs=[pl.BlockSpec((blk * sub, LANES), slot_map),
                  pl.BlockSpec((None, d, de), w_map), pl.BlockSpec((None, d, de), w_map),
                  pl.BlockSpec((None, de, d), w_map)],
        out_specs=pl.BlockSpec((blk * sub, LANES), slot_map),
        scratch_shapes=[pltpu.VMEM((d, de), BF16), pltpu.VMEM((d, de), BF16), pltpu.VMEM((de, d), BF16)],
    )
    return pl.pallas_call(
        functools.partial(_ffn_kernel, blk=blk),
        grid_spec=grid_spec,
        out_shape=jax.ShapeDtypeStruct(xs.shape, U32),
        compiler_params=_params(),
        name="ffn",
    )(block_e, n_used, xs, wg, wu, wd)


def _combine_kernel(cnt_ref, off_ref, gst_ref, x1_ref, pos_ref, gate_ref, ys_ref, g_ref, o_ref,
                    buf0_ref, buf1_ref, sem):
    i = pl.program_id(0)
    n = pl.num_programs(0)
    d = x1_ref.shape[1]
    tm = x1_ref.shape[0] // 2
    na = 2 * tm
    sub = SUBLANES
    nb, tq2, _ = o_ref.shape
    tq = tq2 // 2
    lane = lax.broadcasted_iota(jnp.int32, (tm, na), 1)
    rid = lax.broadcasted_iota(jnp.int32, (na, tm), 0)

    def fetch(tile, buf, s):
        def per_expert(e, c):
            k = tile * N_EXPERTS + e
            _strip_copies(cnt_ref[k], ys_ref, gst_ref[k], buf, off_ref[k], sem.at[s], na)
            return c

        lax.fori_loop(0, N_EXPERTS, per_expert, 0)

    for s, buf in enumerate((buf0_ref, buf1_ref)):
        tile = 2 * i + s
        cs = slice(s * tm, (s + 1) * tm)

        @pl.when(i == 0)
        def _(buf=buf, s=s, tile=tile):
            fetch(tile, buf, s)

        prow = jnp.concatenate([pos_ref[0:2, cs], jnp.zeros((sub - 2, tm), F32)], axis=0)
        pcol = jnp.concatenate([prow] * (LANES // sub), axis=0).T.astype(jnp.int32)
        pick = jnp.where((lane == pcol[:, 0:1]) | (lane == pcol[:, 1:2]), 1.0, 0.0).astype(BF16)
        p0 = pos_ref[0:1, cs].astype(jnp.int32)
        p1 = pos_ref[1:2, cs].astype(jnp.int32)
        grow = jnp.sum(jnp.where(rid == p0, gate_ref[0:1, cs], 0.0) + jnp.where(rid == p1, gate_ref[1:2, cs], 0.0),
                       axis=1, keepdims=True)

        pltpu.make_async_copy(_tokens(ys_ref, 0, na), buf, sem.at[s]).wait()
        y = _unpack_tokens(buf, na)

        @pl.when(i + 1 < n)
        def _(buf=buf, s=s, tile=tile):
            fetch(tile + 2, buf, s)

        moe = jnp.dot(pick, (y * grow).astype(BF16), preferred_element_type=F32)
        x = x1_ref[cs, :] + moe
        o_ref[:, s * tq:(s + 1) * tq, :] = _rms(x, g_ref[...]).reshape(nb, tq, d)


def _combine(cnt_t, off_t, gst_t, x1, pos, gate, ys, g, nb, seq):
    t, d = x1.shape
    tq = TIME_CHUNK
    tm = nb * tq
    sub = SUBLANES
    grid_spec = pltpu.PrefetchScalarGridSpec(
        num_scalar_prefetch=3,
        grid=(t // (2 * tm),),
        in_specs=[pl.BlockSpec((2 * tm, d), lambda i, *_: (i, 0)),
                  pl.BlockSpec((sub, 2 * tm), lambda i, *_: (0, i)),
                  pl.BlockSpec((sub, 2 * tm), lambda i, *_: (0, i)),
                  pl.BlockSpec(memory_space=pl.ANY),
                  _full(g)],
        out_specs=pl.BlockSpec((nb, 2 * tq, d), lambda i, *_: (0, i, 0)),
        scratch_shapes=[pltpu.VMEM((2 * tm * TOKEN_ROWS, LANES), U32),
                        pltpu.VMEM((2 * tm * TOKEN_ROWS, LANES), U32), pltpu.SemaphoreType.DMA((2,))],
    )
    return pl.pallas_call(
        _combine_kernel,
        grid_spec=grid_spec,
        out_shape=jax.ShapeDtypeStruct((nb, seq, d), F32),
        compiler_params=_params(),
        name="combine",
    )(cnt_t, off_t, gst_t, x1, pos, gate, ys, g)


def _s5_tables(lam_re, lam_im, log_dt, b_re, b_im, c_re, c_im):
    ng, p = lam_re.shape
    npair = ng // 2
    kper = LANES // (2 * S5_GROUP)
    lam = lax.complex(lam_re, lam_im)
    dt = jnp.exp(log_dt)[:, None]
    lam_bar = jnp.exp(lam * dt)
    b_bar = ((lam_bar - 1.0) / lam)[..., None] * lax.complex(b_re, b_im)
    cmat = lax.complex(c_re, c_im)
    lam2 = lam_bar * lam_bar
    lr = jnp.real(lam2).reshape(npair, 2 * p)
    li = jnp.imag(lam2).reshape(npair, 2 * p)
    eye2 = jnp.eye(2, dtype=F32)
    sel = jax.nn.one_hot(jnp.arange(npair) % kper, kper, dtype=F32)

    def in_tiles(bc):
        bv = jnp.stack([jnp.real(bc), jnp.imag(bc)])
        bv = bv.reshape(2, npair, 2, p, S5_GROUP).transpose(1, 2, 4, 0, 3)
        bblk = bv[:, :, :, :, None, :] * eye2[None, :, None, None, :, None]
        bblk = bblk.reshape(npair, 2 * S5_GROUP, 4 * p)
        return (sel[:, :, None, None] * bblk[:, None]).reshape(npair, LANES, 4 * p)

    def out_tiles(cc):
        cv = jnp.stack([jnp.real(cc), -jnp.imag(cc)])
        cv = cv.reshape(2, npair, 2, S5_GROUP, p).transpose(1, 0, 2, 4, 3)
        cblk = cv[:, :, :, :, None, :] * eye2[None, None, :, None, :, None]
        cblk = cblk.reshape(npair, 4 * p, 2 * S5_GROUP)
        return (cblk[:, :, None, :] * sel[:, None, :, None]).reshape(npair, 4 * p, LANES)

    bm2 = jnp.concatenate([in_tiles(b_bar), in_tiles(lam_bar[..., None] * b_bar)], axis=1)
    cm = jnp.concatenate([out_tiles(cmat), out_tiles(cmat * lam_bar[:, None, :])], axis=2)
    direct = jnp.real(jnp.einsum('gop,gpi->gio', cmat, b_bar))
    gmat = (direct[:, :, None, :] * jnp.eye(ng, dtype=F32)[:, None, :, None]).reshape(
        ng * S5_GROUP, ng * S5_GROUP)
    return bm2.astype(BF16), lr, li, cm.astype(BF16), gmat.astype(BF16)


def _blockdiag_tiles(w):
    nh, hi, ho = w.shape
    per = MXU_DIM // hi
    eye = jnp.eye(per, dtype=w.dtype)
    t = w.reshape(nh // per, per, hi, ho)[:, :, :, None, :] * eye[None, :, None, :, None]
    return t.reshape(nh // per, per * hi, per * ho)


def kernel(x, norm_mix, w_in, s5_lam_re, s5_lam_im, s5_log_dt, s5_b_re, s5_b_im, s5_c_re, s5_c_im, s5_d,
           w_s5_out, conv_w, conv_b, lru_w_r, lru_b_r, lru_w_i, lru_b_i, lru_lambda, w_lru_out, w_o, norm_ffn,
           router_group_w, router_group_b, router_expert_w, router_expert_b, expert_w_gate, expert_w_up,
           expert_w_down, norm_final):
    nb, seq, d = x.shape
    t = nb * seq
    assert w_in.shape[0] == 1, "one layer: the final RMSNorm is fused into the layer's combine kernel"
    l = 0
    s5w = s5_d.shape[-1]
    lruw = conv_b.shape[-1]
    rows = nb * TIME_CHUNK

    r = jnp.arange(rows)
    perm = jax.nn.one_hot((r % nb) * TIME_CHUNK + r // nb, rows, dtype=BF16)

    sp = ((-0.5 * LRU_C) * jax.nn.softplus(-lru_lambda[l]))[None, :]
    lrup = (conv_w[l], conv_b[l][None, :], _blockdiag_tiles(0.5 * lru_w_r[l]).astype(BF16),
            _blockdiag_tiles(0.5 * lru_w_i[l]).astype(BF16), 0.5 * lru_b_r[l][None, :],
            0.5 * lru_b_i[l][None, :], sp)
    gate_cols = jnp.arange(w_in.shape[-1]) >= s5w + 2 * lruw
    w_in_b = jnp.where(gate_cols[None, :], 0.5 * w_in[l], w_in[l]).astype(BF16)
    u, a, b, gg, ga, gb = _inproj(x, norm_mix[l][None, :], perm, w_in_b, lrup, s5w, lruw)

    bm, lr, li, cm, gmat = _s5_tables(s5_lam_re[l], s5_lam_im[l], s5_log_dt[l], s5_b_re[l], s5_b_im[l],
                                           s5_c_re[l], s5_c_im[l])
    glu_cols = jnp.arange(w_s5_out.shape[-1]) >= d
    w_s5_out_b = jnp.where(glu_cols[None, :], 0.5 * w_s5_out[l], w_s5_out[l]).astype(BF16)
    s5p = (bm, lr, li, cm, gmat, s5_d[l][None, :], w_s5_out_b)
    mixed = _mixers(u, a, b, gg, ga, gb, s5p, w_lru_out[l].astype(BF16), nb)

    gap = jnp.zeros((SUBLANES - N_GROUPS, d), F32)
    wrt = jnp.concatenate([router_group_w[l].T, gap, router_expert_w[l].T], axis=0)
    wr_hi = wrt.astype(BF16)
    wr_lo = (wrt - wr_hi.astype(F32)).astype(BF16)
    rb = jnp.concatenate([router_group_b[l], jnp.zeros((SUBLANES - N_GROUPS,), F32), router_expert_b[l]])
    rbias = jnp.broadcast_to(rb[:, None], (ROUTER_ROWS, LANES))

    x1, h2, logits = _mixroute(x, mixed, perm.T, w_o[l].astype(BF16), norm_ffn[l][None, :], wr_hi, wr_lo, rbias)
    tri = jnp.triu(jnp.ones((rows, rows), BF16), k=1)
    ltri = jnp.tril(jnp.ones((N_EXPERTS, N_EXPERTS), BF16), k=-1)
    gate, pos, tcnt = _route(logits, tri, ltri)

    n_tiles = seq // TIME_CHUNK
    cnt_t = tcnt.reshape(n_tiles, SUBLANES, LANES)[:, 0, :N_EXPERTS].astype(jnp.int32)
    off_t = jnp.cumsum(cnt_t, axis=1) - cnt_t
    counts = jnp.sum(cnt_t, axis=0)
    padded = ((counts + SLOT_BLOCK - 1) // SLOT_BLOCK) * SLOT_BLOCK
    pad_ends = jnp.cumsum(padded)
    pad_starts = pad_ends - padded
    gst_t = pad_starts[None, :] + jnp.cumsum(cnt_t, axis=0) - cnt_t
    n_blocks = -(-(2 * t) // SLOT_BLOCK) + N_EXPERTS
    n_slots = n_blocks * SLOT_BLOCK
    block_start = jnp.arange(n_blocks, dtype=jnp.int32) * SLOT_BLOCK
    block_e = jnp.minimum(jnp.sum((pad_ends[None, :] <= block_start[:, None]).astype(jnp.int32), axis=1),
                          N_EXPERTS - 1)
    n_used = pad_ends[-1:] // SLOT_BLOCK
    cnt_f, off_f, gst_f = cnt_t.reshape(-1), off_t.reshape(-1), gst_t.reshape(-1)

    xs = _dispatch(cnt_f, off_f, gst_f, counts, pad_starts, pad_ends, h2, pos, rows, n_slots)
    ys = _ffn(block_e, n_used, xs, expert_w_gate[l], expert_w_up[l], expert_w_down[l])
    return _combine(cnt_f, off_f, gst_f, x1, pos, gate, ys, norm_final[None, :], nb, seq)
```

```python
import functools
import math

import jax
import jax.numpy as jnp
from jax import lax
from jax.experimental import pallas as pl
from jax.experimental.pallas import tpu as pltpu

F32 = jnp.float32
BF16 = jnp.bfloat16
U32 = jnp.uint32

RMS_EPS = 1e-6
S5_GROUP = 16
CONV_WIDTH = 4
LRU_C = 8.0
N_GROUPS = 4
EXPERTS_PER_GROUP = 8
N_EXPERTS = N_GROUPS * EXPERTS_PER_GROUP

LANES = 128
SUBLANES = 8
TOKEN_ROWS = 4
MXU_DIM = 256
TIME_CHUNK = 64
SCAN_UNROLL = 4
SLOT_BLOCK = 512
ROUTE_WIDTH = 4096
ROUTER_ROWS = 40
VMEM_LIMIT = 56 * 1024 * 1024


def _gelu(x):
    c = math.sqrt(2.0 / math.pi)
    hx = 0.5 * x
    return hx * jnp.tanh(x * (c + (c * 0.044715) * (x * x))) + hx


def _sigmoid_of_half(xh):
    return 0.5 * jnp.tanh(xh) + 0.5


def _rms(x, g):
    return x * lax.rsqrt(jnp.mean(x * x, axis=-1, keepdims=True) + RMS_EPS) * g


def _params():
    return pltpu.CompilerParams(dimension_semantics=("arbitrary",), vmem_limit_bytes=VMEM_LIMIT)


def _full(a):
    return pl.BlockSpec(a.shape, lambda i, *_: (0,) * a.ndim)


def _tile(t):
    return pl.ds(pl.multiple_of(t * SUBLANES, SUBLANES), SUBLANES)


def _inproj_kernel(x_ref, g_ref, p_ref, w_ref, cw_ref, cb_ref, wr_ref, wi_ref, br_ref, bi_ref, sp_ref,
                   u_ref, a_ref, b_ref, gg_ref, ga_ref, gb_ref, xprev_ref, *, s5w, lruw):
    nb, tq, d = x_ref.shape
    rows = nb * tq
    halo = (CONV_WIDTH - 1) * nb

    @pl.when(pl.program_id(0) == 0)
    def _():
        xprev_ref[...] = jnp.zeros_like(xprev_ref)

    h = _rms(x_ref[...].reshape(rows, d), g_ref[...])
    hb = jnp.dot(p_ref[...], h.astype(BF16), preferred_element_type=F32).astype(BF16)

    def proj(lo, hi):
        return jnp.dot(hb, w_ref[:, lo:hi], preferred_element_type=F32)

    o1 = s5w
    o2 = o1 + lruw
    o3 = o2 + lruw
    o4 = o3 + d
    xin = proj(o1, o2)
    xext = jnp.concatenate([xprev_ref[...], xin], axis=0)
    xprev_ref[...] = xin[rows - halo:, :]
    xc_all = cb_ref[...]
    for k in range(CONV_WIDTH):
        xc_all = xc_all + xext[k * nb:k * nb + rows, :] * cw_ref[k:k + 1, :]

    def lru_gates(j):
        cs = slice(MXU_DIM * j, MXU_DIM * (j + 1))
        xc = xc_all[:, cs]
        xcb = xc.astype(BF16)
        tr = jnp.tanh(jnp.dot(xcb, wr_ref[j], preferred_element_type=F32) + br_ref[:, cs])
        ti = jnp.tanh(jnp.dot(xcb, wi_ref[j], preferred_element_type=F32) + bi_ref[:, cs])
        log_a = sp_ref[:, cs] * tr + sp_ref[:, cs]
        th = jnp.tanh(log_a)
        half_mult = jnp.sqrt((-0.5 * th) / (1.0 - th))
        a_ref[:, cs] = jnp.exp(log_a)
        b_ref[:, cs] = (half_mult * xc) * (ti + 1.0)

    others = [lambda: u_ref.__setitem__(Ellipsis, proj(0, o1)),
              lambda: gg_ref.__setitem__(Ellipsis, _gelu(proj(o2, o3)).astype(BF16)),
              lambda: ga_ref.__setitem__(Ellipsis, _sigmoid_of_half(proj(o3, o4)).astype(BF16)),
              lambda: gb_ref.__setitem__(Ellipsis, _sigmoid_of_half(proj(o4, o4 + d)).astype(BF16))]
    ntile = wr_ref.shape[0]
    for j in range(max(ntile, len(others))):
        if j < ntile:
            lru_gates(j)
        if j < len(others):
            others[j]()


def _inproj(x, g, perm, w_in_b, lrup, s5w, lruw):
    nb, seq, d = x.shape
    tq = TIME_CHUNK
    rows = nb * tq
    t = nb * seq
    row = lambda c: pl.BlockSpec((rows, c), lambda i: (i, 0))
    return pl.pallas_call(
        functools.partial(_inproj_kernel, s5w=s5w, lruw=lruw),
        grid=(seq // tq,),
        in_specs=[pl.BlockSpec((nb, tq, d), lambda i: (0, i, 0)), _full(g), _full(perm), _full(w_in_b)]
        + [_full(p) for p in lrup],
        out_specs=[row(s5w), row(lruw), row(lruw), row(lruw), row(d), row(d)],
        out_shape=[jax.ShapeDtypeStruct((t, s5w), F32), jax.ShapeDtypeStruct((t, lruw), F32),
                   jax.ShapeDtypeStruct((t, lruw), F32), jax.ShapeDtypeStruct((t, lruw), BF16),
                   jax.ShapeDtypeStruct((t, d), BF16), jax.ShapeDtypeStruct((t, d), BF16)],
        scratch_shapes=[pltpu.VMEM(((CONV_WIDTH - 1) * nb, lruw), F32)],
        compiler_params=_params(),
        name="inproj",
    )(x, g, perm, w_in_b, *lrup)


def _mixers_kernel(u_ref, a_ref, b_ref, gg_ref, ga_ref, gb_ref,
                   bm_ref, lr_ref, li_ref, cm_ref, gm_ref, dk_ref, wo_ref, wlo_ref,
                   o_ref, state_ref, sbuf_ref, uprev_ref, h_ref, hbuf_ref, *, tq, nb):
    d = o_ref.shape[-1]
    s5w = u_ref.shape[-1]
    npair = bm_ref.shape[0]
    kper = LANES // (2 * S5_GROUP)
    nk = tq // 2
    half = nk * nb

    @pl.when(pl.program_id(0) == 0)
    def _():
        state_ref[...] = jnp.zeros_like(state_ref)
        uprev_ref[...] = jnp.zeros_like(uprev_ref)
        h_ref[...] = jnp.zeros_like(h_ref)

    u2 = u_ref[...]
    u4 = u2.reshape(nk, 2, nb, s5w)
    u_e = u4[:, 0].reshape(half, s5w)
    u_o = u4[:, 1].reshape(half, s5w)
    u_po = jnp.concatenate([uprev_ref[...], u_o[:half - nb]], axis=0)
    uprev_ref[...] = u_o[half - nb:]
    ue_b = u_e.astype(BF16)
    upo_b = u_po.astype(BF16)
    for k in range(npair):
        cs = slice(LANES * (k // kper), LANES * (k // kper + 1))
        bu = jnp.dot(jnp.concatenate([ue_b[:, cs], upo_b[:, cs]], axis=1), bm_ref[k],
                     preferred_element_type=F32)
        sbuf_ref[2 * k] = bu[:, :LANES]
        sbuf_ref[2 * k + 1] = bu[:, LANES:]

    pairs_per_loop = 4
    for kk in range(npair // pairs_per_loop):
        k0 = kk * pairs_per_loop
        lr = [jnp.broadcast_to(lr_ref[k0 + j:k0 + j + 1, :], (nb, LANES)) for j in range(pairs_per_loop)]
        li = [jnp.broadcast_to(li_ref[k0 + j:k0 + j + 1, :], (nb, LANES)) for j in range(pairs_per_loop)]
        init = tuple(state_ref[2 * k0 + j] for j in range(2 * pairs_per_loop))

        def body(it, carry, k0=k0, lr=lr, li=li):
            carry = list(carry)
            for s in range(SCAN_UNROLL):
                rows = _tile(it * SCAN_UNROLL + s)
                for j in range(pairs_per_loop):
                    sre, sim = carry[2 * j], carry[2 * j + 1]
                    nre = lr[j] * sre - li[j] * sim + sbuf_ref[2 * (k0 + j), rows, :]
                    nim = lr[j] * sim + li[j] * sre + sbuf_ref[2 * (k0 + j) + 1, rows, :]
                    sbuf_ref[2 * (k0 + j), rows, :] = nre
                    sbuf_ref[2 * (k0 + j) + 1, rows, :] = nim
                    carry[2 * j], carry[2 * j + 1] = nre, nim
            return tuple(carry)

        fin = lax.fori_loop(0, nk // SCAN_UNROLL, body, init)
        for j in range(2 * pairs_per_loop):
            state_ref[2 * k0 + j] = fin[j]

    def scan_body(it, h):
        for s in range(SCAN_UNROLL):
            tr = _tile(it * SCAN_UNROLL + s)
            h = a_ref[tr, :] * h + b_ref[tr, :]
            hbuf_ref[tr, :] = h
        return h

    h_ref[...] = lax.fori_loop(0, tq // SCAN_UNROLL, scan_body, h_ref[...])

    ys_e, ys_o = [], []
    for kb in range(npair // kper):
        acc_e = acc_o = None
        for j in range(kper):
            k = kb * kper + j
            s = jnp.concatenate([sbuf_ref[2 * k], sbuf_ref[2 * k + 1]], axis=1).astype(BF16)
            p = jnp.dot(s, cm_ref[k], preferred_element_type=F32)
            acc_e = p[:, :LANES] if acc_e is None else acc_e + p[:, :LANES]
            acc_o = p[:, LANES:] if acc_o is None else acc_o + p[:, LANES:]
        ys_e.append(acc_e)
        ys_o.append(acc_o)
    y_e = jnp.concatenate(ys_e, axis=1) + dk_ref[...] * u_e
    y_o = (jnp.concatenate(ys_o, axis=1) + jnp.dot(u_o.astype(BF16), gm_ref[...], preferred_element_type=F32)
           + dk_ref[...] * u_o)
    y = jnp.stack([y_e.reshape(nk, nb, s5w), y_o.reshape(nk, nb, s5w)], axis=1).reshape(2 * half, s5w)
    z = jnp.dot(_gelu(y).astype(BF16), wo_ref[...], preferred_element_type=F32)
    ya = z[:, :d] * _sigmoid_of_half(z[:, d:])

    yl = hbuf_ref[...] * gg_ref[...].astype(F32)
    yb = jnp.dot(yl.astype(BF16), wlo_ref[...], preferred_element_type=F32)
    o_ref[...] = (ya * ga_ref[...].astype(F32) + yb * gb_ref[...].astype(F32)).astype(BF16)


def _mixers(u, a, b, gg, ga, gb, s5p, w_lru_out_b, nb):
    t, s5w = u.shape
    w = a.shape[-1]
    d = ga.shape[-1]
    tq = TIME_CHUNK
    rows = nb * tq
    nstate = s5p[0].shape[0] * MXU_DIM
    row = lambda c: pl.BlockSpec((rows, c), lambda i: (i, 0))
    params = tuple(s5p) + (w_lru_out_b,)
    return pl.pallas_call(
        functools.partial(_mixers_kernel, tq=tq, nb=nb),
        grid=(t // rows,),
        in_specs=[row(s5w), row(w), row(w), row(w), row(d), row(d)] + [_full(p) for p in params],
        out_specs=row(d),
        out_shape=jax.ShapeDtypeStruct((t, d), BF16),
        scratch_shapes=[pltpu.VMEM((nstate // LANES, nb, LANES), F32),
                        pltpu.VMEM((nstate // LANES, rows // 2, LANES), F32),
                        pltpu.VMEM((nb, s5w), F32),
                        pltpu.VMEM((nb, w), F32),
                        pltpu.VMEM((rows, w), F32)],
        compiler_params=_params(),
        name="mixers",
    )(u, a, b, gg, ga, gb, *params)


def _mixroute_kernel(x_ref, m_ref, pt_ref, wo_ref, g_ref, wh_ref, wl_ref, rb_ref,
                     x1_ref, h2_ref, lt_ref):
    nb, tq, d = x_ref.shape
    tm = nb * tq
    m = jnp.dot(pt_ref[...], m_ref[...], preferred_element_type=F32).astype(BF16)
    x1 = x_ref[...].reshape(tm, d) + jnp.dot(m, wo_ref[...], preferred_element_type=F32)
    x1_ref[...] = x1
    h = _rms(x1, g_ref[...])

    hh = h.astype(BF16)
    h2_ref[...] = hh
    hl = (h - hh.astype(F32)).astype(BF16)
    dn = (((1,), (1,)), ((), ()))
    lt = (lax.dot_general(wh_ref[...], hh, dn, preferred_element_type=F32)
          + lax.dot_general(wh_ref[...], hl, dn, preferred_element_type=F32)
          + lax.dot_general(wl_ref[...], hh, dn, preferred_element_type=F32))
    lt_ref[...] = lt + rb_ref[:, 0:1]


def _mixroute(x, m, perm_t, w_o_b, g, wr_hi, wr_lo, rbias):
    nb, seq, d = x.shape
    tq = TIME_CHUNK
    tm = nb * tq
    t = nb * seq
    row = lambda c: pl.BlockSpec((tm, c), lambda i: (i, 0))
    return pl.pallas_call(
        _mixroute_kernel,
        grid=(seq // tq,),
        in_specs=[pl.BlockSpec((nb, tq, d), lambda i: (0, i, 0)), row(d), _full(perm_t), _full(w_o_b),
                  _full(g), _full(wr_hi), _full(wr_lo), _full(rbias)],
        out_specs=[row(d), row(d), pl.BlockSpec((ROUTER_ROWS, tm), lambda i: (0, i))],
        out_shape=[jax.ShapeDtypeStruct((t, d), F32), jax.ShapeDtypeStruct((t, d), BF16),
                   jax.ShapeDtypeStruct((ROUTER_ROWS, t), F32)],
        compiler_params=_params(),
        name="mixroute",
    )(x, m, perm_t, w_o_b, g, wr_hi, wr_lo, rbias)


def _route_kernel(lt_ref, tri_ref, ltri_ref, gate_ref, pos_ref, cnt_ref, *, tile):
    lt = lt_ref[...]
    tm = lt.shape[1]
    gl = lt[0:N_GROUPS, :]
    gmax = jnp.max(gl, axis=0, keepdims=True)
    gsum = jnp.sum(jnp.exp(gl - gmax), axis=0, keepdims=True)
    p_g = 1.0 / gsum
    iota_g = lax.broadcasted_iota(jnp.int32, gl.shape, 0)
    g_idx = jnp.min(jnp.where(gl == gmax, iota_g, N_GROUPS), axis=0, keepdims=True)

    el = lt[SUBLANES:SUBLANES + EXPERTS_PER_GROUP, :]
    for g in range(1, N_GROUPS):
        lo = SUBLANES + EXPERTS_PER_GROUP * g
        el = jnp.where(g_idx == g, lt[lo:lo + EXPERTS_PER_GROUP, :], el)
    emax = jnp.max(el, axis=0, keepdims=True)
    esum = jnp.sum(jnp.exp(el - emax), axis=0, keepdims=True)
    iota_e = lax.broadcasted_iota(jnp.int32, el.shape, 0)
    i1 = jnp.min(jnp.where(el == emax, iota_e, EXPERTS_PER_GROUP), axis=0, keepdims=True)
    el2 = jnp.where(iota_e == i1, -jnp.inf, el)
    emax2 = jnp.max(el2, axis=0, keepdims=True)
    i2 = jnp.min(jnp.where(el2 == emax2, iota_e, EXPERTS_PER_GROUP), axis=0, keepdims=True)
    p1 = 1.0 / esum
    p2 = jnp.exp(emax2 - emax) / esum
    psum = p1 + p2
    g1 = p_g * (p1 / psum)
    g2 = p_g * (p2 / psum)
    e1 = g_idx * EXPERTS_PER_GROUP + i1
    e2 = g_idx * EXPERTS_PER_GROUP + i2
    gate_ref[...] = jnp.concatenate([g1, g2, jnp.zeros((SUBLANES - 2, tm), F32)], axis=0)

    dn = (((1,), (1,)), ((), ()))
    iota = lax.broadcasted_iota(jnp.int32, (N_EXPERTS, tile), 0)
    iota_l = lax.broadcasted_iota(jnp.int32, (LANES, tile), 0)
    ones = jnp.ones((SUBLANES, tile), BF16)
    for s in range(tm // tile):
        cs = slice(s * tile, (s + 1) * tile)
        a1, a2 = e1[:, cs], e2[:, cs]
        oh0 = iota == a1
        oh1 = iota == a2
        c = jnp.where(oh0 | oh1, 1.0, 0.0)
        ci = jnp.sum(c, axis=1, keepdims=True).astype(jnp.int32)
        hi = jnp.broadcast_to((ci >> 5).astype(F32), (N_EXPERTS, LANES)).astype(BF16)
        lo = jnp.broadcast_to((ci & 31).astype(F32), (N_EXPERTS, LANES)).astype(BF16)
        first = (32.0 * jnp.dot(ltri_ref[...], hi, preferred_element_type=F32)
                 + jnp.dot(ltri_ref[...], lo, preferred_element_type=F32))
        before = jnp.dot(c.astype(BF16), tri_ref[...], preferred_element_type=F32) + first[:, 0:1]
        p0 = jnp.sum(jnp.where(oh0, before, 0.0), axis=0, keepdims=True)
        p1 = jnp.sum(jnp.where(oh1, before, 0.0), axis=0, keepdims=True)
        pos_ref[:, cs] = jnp.concatenate([p0, p1, jnp.zeros((SUBLANES - 2, tile), F32)], axis=0)
        hits = jnp.where((iota_l == a1) | (iota_l == a2), 1.0, 0.0).astype(BF16)
        cnt_ref[SUBLANES * s:SUBLANES * (s + 1), :] = lax.dot_general(ones, hits, dn, preferred_element_type=F32)


def _route(lt, tri, ltri):
    t = lt.shape[1]
    tile = tri.shape[0]
    tm = min(t, ROUTE_WIDTH)
    col = pl.BlockSpec((SUBLANES, tm), lambda i: (0, i))
    return pl.pallas_call(
        functools.partial(_route_kernel, tile=tile),
        grid=(t // tm,),
        in_specs=[pl.BlockSpec((ROUTER_ROWS, tm), lambda i: (0, i)), _full(tri), _full(ltri)],
        out_specs=[col, col, pl.BlockSpec((tm // tile * SUBLANES, LANES), lambda i: (i, 0))],
        out_shape=[jax.ShapeDtypeStruct((SUBLANES, t), F32), jax.ShapeDtypeStruct((SUBLANES, t), F32),
                   jax.ShapeDtypeStruct((t // tile * SUBLANES, LANES), F32)],
        compiler_params=_params(),
        name="route",
    )(lt, tri, ltri)


def _tokens(ref, tok, n):
    return ref.at[pl.ds(pl.multiple_of(tok * TOKEN_ROWS, TOKEN_ROWS), n * TOKEN_ROWS)]


def _pack_tokens(ref, v):
    n, d = v.shape
    half = d // 2
    bits = lax.bitcast_convert_type(v, U32)
    w = (bits[:, :half] >> 16) | (bits[:, half:] & jnp.uint32(0xFFFF0000))
    for j in range(TOKEN_ROWS):
        ref[pl.ds(j, n, stride=TOKEN_ROWS), :] = w[:, LANES * j:LANES * (j + 1)]


def _unpack_tokens(ref, n):
    ws = [ref[pl.ds(j, n, stride=TOKEN_ROWS), :] for j in range(TOKEN_ROWS)]
    lo = [lax.bitcast_convert_type(w << 16, F32) for w in ws]
    hi = [lax.bitcast_convert_type(w & jnp.uint32(0xFFFF0000), F32) for w in ws]
    return jnp.concatenate(lo + hi, axis=1)


def _strip_copies(n, src, src_tok, dst, dst_tok, sem, max_tokens, wait=False):
    del max_tokens

    @pl.when(n > 0)
    def _():
        cp = pltpu.make_async_copy(_tokens(src, src_tok, n), _tokens(dst, dst_tok, n), sem)
        if wait:
            cp.wait()
        else:
            cp.start()


def _dispatch_kernel(cnt_ref, off_ref, gst_ref, tot_ref, pst_ref, pen_ref,
                     h2_ref, pos_ref, xs_ref,
                     stage0_ref, stage1_ref, zero_ref, sem, zsem):
    i = pl.program_id(0)
    last = pl.num_programs(0) - 1
    tm = pos_ref.shape[1] // 2
    na = 2 * tm
    rid = lax.broadcasted_iota(jnp.int32, (na, tm), 0)

    for s, stage in enumerate((stage0_ref, stage1_ref)):
        tile = 2 * i + s
        cs = slice(s * tm, (s + 1) * tm)
        p0i = pos_ref[0:1, cs].astype(jnp.int32)
        p1i = pos_ref[1:2, cs].astype(jnp.int32)
        onehot = jnp.where((rid == p0i) | (rid == p1i), 1.0, 0.0).astype(BF16)
        srt = jnp.dot(onehot, h2_ref[cs, :], preferred_element_type=F32)

        @pl.when(i >= 1)
        def _(stage=stage, s=s):
            pltpu.make_async_copy(stage, _tokens(xs_ref, 0, na), sem.at[s]).wait()

        _pack_tokens(stage, srt)

        def per_expert(e, c, stage=stage, s=s, tile=tile):
            k = tile * N_EXPERTS + e
            _strip_copies(cnt_ref[k], stage, off_ref[k], xs_ref, gst_ref[k], sem.at[s], na)
            return c

        lax.fori_loop(0, N_EXPERTS, per_expert, 0)

    @pl.when(i == last)
    def _():
        zero_ref[...] = jnp.zeros_like(zero_ref)
        for wait in (False, True):
            for e in range(N_EXPERTS):
                _strip_copies(pen_ref[e] - pst_ref[e] - tot_ref[e], zero_ref, 0, xs_ref,
                              pst_ref[e] + tot_ref[e], zsem, SLOT_BLOCK - 1, wait=wait)

        nblk = xs_ref.shape[0] // zero_ref.shape[0]

        def bcopy(b):
            return pltpu.make_async_copy(zero_ref, _tokens(xs_ref, b * SLOT_BLOCK, SLOT_BLOCK), zsem)

        first_unused = pen_ref[N_EXPERTS - 1] // SLOT_BLOCK
        lax.fori_loop(first_unused, nblk, lambda b, c: (bcopy(b).start(), c)[1], 0)
        lax.fori_loop(first_unused, nblk, lambda b, c: (bcopy(b).wait(), c)[1], 0)

        pltpu.make_async_copy(stage0_ref, _tokens(xs_ref, 0, na), sem.at[0]).wait()
        pltpu.make_async_copy(stage1_ref, _tokens(xs_ref, 0, na), sem.at[1]).wait()


def _dispatch(cnt_t, off_t, gst_t, tot, pst, pen, h2, pos, tm, n_slots):
    sub = SUBLANES
    tr = TOKEN_ROWS
    t, d = h2.shape
    assert d == 2 * tr * LANES
    grid_spec = pltpu.PrefetchScalarGridSpec(
        num_scalar_prefetch=6,
        grid=(t // (2 * tm),),
        in_specs=[pl.BlockSpec((2 * tm, d), lambda i, *_: (i, 0)),
                  pl.BlockSpec((sub, 2 * tm), lambda i, *_: (0, i))],
        out_specs=pl.BlockSpec(memory_space=pl.ANY),
        scratch_shapes=[pltpu.VMEM((2 * tm * tr, LANES), U32), pltpu.VMEM((2 * tm * tr, LANES), U32),
                        pltpu.VMEM((SLOT_BLOCK * tr, LANES), U32), pltpu.SemaphoreType.DMA((2,)),
                        pltpu.SemaphoreType.DMA(())],
    )
    return pl.pallas_call(
        _dispatch_kernel,
        grid_spec=grid_spec,
        out_shape=jax.ShapeDtypeStruct((n_slots * tr, LANES), U32),
        compiler_params=_params(),
        name="dispatch",
    )(cnt_t, off_t, gst_t, tot, pst, pen, h2, pos)


def _ffn_kernel(be_ref, nu_ref, xs_ref, wg_ref, wu_ref, wd_ref, ys_ref, wgb_ref, wub_ref, wdb_ref, *, blk):
    i = pl.program_id(0)

    @pl.when(i < nu_ref[0])
    def _():
        prev = be_ref[jnp.maximum(i - 1, 0)]

        @pl.when((i == 0) | (be_ref[i] != prev))
        def _():
            wgb_ref[...] = wg_ref[...].astype(BF16)
            wub_ref[...] = wu_ref[...].astype(BF16)
            wdb_ref[...] = wd_ref[...].astype(BF16)

        x = _unpack_tokens(xs_ref, blk).astype(BF16)
        g = jnp.dot(x, wgb_ref[...], preferred_element_type=F32)
        u = jnp.dot(x, wub_ref[...], preferred_element_type=F32)
        hg = 0.5 * g
        a = (hg * jnp.tanh(hg) + hg) * u
        y = jnp.dot(a.astype(BF16), wdb_ref[...], preferred_element_type=F32)
        _pack_tokens(ys_ref, y.astype(BF16).astype(F32))

    @pl.when(i >= nu_ref[0])
    def _():
        ys_ref[...] = jnp.zeros_like(ys_ref)


def _ffn(block_e, n_used, xs, wg, wu, wd):
    sub = TOKEN_ROWS
    blk = SLOT_BLOCK
    n_blocks = xs.shape[0] // (sub * blk)
    ne, d, de = wg.shape

    def slot_map(i, be, nu):
        return (i, 0)

    def w_map(i, be, nu):
        return (be[jnp.minimum(i, nu[0] - 1)], 0, 0)

    grid_spec = pltpu.PrefetchScalarGridSpec(
        num_scalar_prefetch=2,
        grid=(n_blocks,),
        in_specs=[pl.BlockSpec((blk * sub, LANES), slot_map),
                  pl.BlockSpec((None, d, de), w_map), pl.BlockSpec((None, d, de), w_map),
                  pl.BlockSpec((None, de, d), w_map)],
        out_specs=pl.BlockSpec((blk * sub, LANES), slot_map),
        scratch_shapes=[pltpu.VMEM((d, de), BF16), pltpu.VMEM((d, de), BF16), pltpu.VMEM((de, d), BF16)],
    )
    return pl.pallas_call(
        functools.partial(_ffn_kernel, blk=blk),
        grid_spec=grid_spec,
        out_shape=jax.ShapeDtypeStruct(xs.shape, U32),
        compiler_params=_params(),
        name="ffn",
    )(block_e, n_used, xs, wg, wu, wd)


def _combine_kernel(cnt_ref, off_ref, gst_ref, x1_ref, pos_ref, gate_ref, ys_ref, g_ref, o_ref,
                    buf0_ref, buf1_ref, sem):
    i = pl.program_id(0)
    n = pl.num_programs(0)
    tm, d = x1_ref.shape
    na = 2 * tm
    sub = SUBLANES

    def fetch(tile, buf, s):
        def per_expert(e, c):
            k = tile * N_EXPERTS + e
            _strip_copies(cnt_ref[k], ys_ref, gst_ref[k], buf, off_ref[k], sem.at[s], na)
            return c

        lax.fori_loop(0, N_EXPERTS, per_expert, 0)

    @pl.when(i == 0)
    def _():
        fetch(0, buf0_ref, 0)

    @pl.when((i + 1 < n) & (i % 2 == 0))
    def _():
        fetch(i + 1, buf1_ref, 1)

    @pl.when((i + 1 < n) & (i % 2 == 1))
    def _():
        fetch(i + 1, buf0_ref, 0)

    prow = jnp.concatenate([pos_ref[0:2, :], jnp.zeros((sub - 2, tm), F32)], axis=0)
    pcol = jnp.concatenate([prow] * (LANES // sub), axis=0).T.astype(jnp.int32)
    lane = lax.broadcasted_iota(jnp.int32, (tm, na), 1)
    pick = jnp.where((lane == pcol[:, 0:1]) | (lane == pcol[:, 1:2]), 1.0, 0.0).astype(BF16)
    rid = lax.broadcasted_iota(jnp.int32, (na, tm), 0)
    p0 = pos_ref[0:1, :].astype(jnp.int32)
    p1 = pos_ref[1:2, :].astype(jnp.int32)
    grow = jnp.sum(jnp.where(rid == p0, gate_ref[0:1, :], 0.0) + jnp.where(rid == p1, gate_ref[1:2, :], 0.0),
                   axis=1, keepdims=True)

    def finish(buf, s):
        pltpu.make_async_copy(_tokens(ys_ref, 0, na), buf, sem.at[s]).wait()
        y = _unpack_tokens(buf, na)
        moe = jnp.dot(pick, (y * grow).astype(BF16), preferred_element_type=F32)
        x = x1_ref[...] + moe
        o_ref[...] = _rms(x, g_ref[...]).reshape(o_ref.shape)

    @pl.when(i % 2 == 0)
    def _():
        finish(buf0_ref, 0)

    @pl.when(i % 2 == 1)
    def _():
        finish(buf1_ref, 1)


def _combine(cnt_t, off_t, gst_t, x1, pos, gate, ys, g, nb, seq):
    t, d = x1.shape
    tq = TIME_CHUNK
    tm = nb * tq
    sub = SUBLANES
    grid_spec = pltpu.PrefetchScalarGridSpec(
        num_scalar_prefetch=3,
        grid=(t // tm,),
        in_specs=[pl.BlockSpec((tm, d), lambda i, *_: (i, 0)),
                  pl.BlockSpec((sub, tm), lambda i, *_: (0, i)),
                  pl.BlockSpec((sub, tm), lambda i, *_: (0, i)),
                  pl.BlockSpec(memory_space=pl.ANY),
                  _full(g)],
        out_specs=pl.BlockSpec((nb, tq, d), lambda i, *_: (0, i, 0)),
        scratch_shapes=[pltpu.VMEM((2 * tm * TOKEN_ROWS, LANES), U32),
                        pltpu.VMEM((2 * tm * TOKEN_ROWS, LANES), U32), pltpu.SemaphoreType.DMA((2,))],
    )
    return pl.pallas_call(
        _combine_kernel,
        grid_spec=grid_spec,
        out_shape=jax.ShapeDtypeStruct((nb, seq, d), F32),
        compiler_params=_params(),
        name="combine",
    )(cnt_t, off_t, gst_t, x1, pos, gate, ys, g)


def _s5_tables(lam_re, lam_im, log_dt, b_re, b_im, c_re, c_im):
    ng, p = lam_re.shape
    npair = ng // 2
    kper = LANES // (2 * S5_GROUP)
    lam = lax.complex(lam_re, lam_im)
    dt = jnp.exp(log_dt)[:, None]
    lam_bar = jnp.exp(lam * dt)
    b_bar = ((lam_bar - 1.0) / lam)[..., None] * lax.complex(b_re, b_im)
    cmat = lax.complex(c_re, c_im)
    lam2 = lam_bar * lam_bar
    lr = jnp.real(lam2).reshape(npair, 2 * p)
    li = jnp.imag(lam2).reshape(npair, 2 * p)
    eye2 = jnp.eye(2, dtype=F32)
    sel = jax.nn.one_hot(jnp.arange(npair) % kper, kper, dtype=F32)

    def in_tiles(bc):
        bv = jnp.stack([jnp.real(bc), jnp.imag(bc)])
        bv = bv.reshape(2, npair, 2, p, S5_GROUP).transpose(1, 2, 4, 0, 3)
        bblk = bv[:, :, :, :, None, :] * eye2[None, :, None, None, :, None]
        bblk = bblk.reshape(npair, 2 * S5_GROUP, 4 * p)
        return (sel[:, :, None, None] * bblk[:, None]).reshape(npair, LANES, 4 * p)

    def out_tiles(cc):
        cv = jnp.stack([jnp.real(cc), -jnp.imag(cc)])
        cv = cv.reshape(2, npair, 2, S5_GROUP, p).transpose(1, 0, 2, 4, 3)
        cblk = cv[:, :, :, :, None, :] * eye2[None, None, :, None, :, None]
        cblk = cblk.reshape(npair, 4 * p, 2 * S5_GROUP)
        return (cblk[:, :, None, :] * sel[:, None, :, None]).reshape(npair, 4 * p, LANES)

    bm2 = jnp.concatenate([in_tiles(b_bar), in_tiles(lam_bar[..., None] * b_bar)], axis=1)
    cm = jnp.concatenate([out_tiles(cmat), out_tiles(cmat * lam_bar[:, None, :])], axis=2)
    direct = jnp.real(jnp.einsum('gop,gpi->gio', cmat, b_bar))
    width = ng * S5_GROUP
    col = jnp.arange(width)
    spread = (col[None, :] % S5_GROUP == jnp.arange(S5_GROUP)[:, None]).astype(F32)
    tiled = jnp.dot(direct.reshape(width, S5_GROUP), spread, precision=lax.Precision.HIGHEST)
    gmat = jnp.where(col[:, None] // S5_GROUP == col[None, :] // S5_GROUP, tiled, 0.0)
    return bm2.astype(BF16), lr, li, cm.astype(BF16), gmat.astype(BF16)


def _blockdiag_tiles(w):
    nh, hi, ho = w.shape
    per = MXU_DIM // hi
    eye = jnp.eye(per, dtype=w.dtype)
    t = w.reshape(nh // per, per, hi, ho)[:, :, :, None, :] * eye[None, :, None, :, None]
    return t.reshape(nh // per, per * hi, per * ho)


def kernel(x, norm_mix, w_in, s5_lam_re, s5_lam_im, s5_log_dt, s5_b_re, s5_b_im, s5_c_re, s5_c_im, s5_d,
           w_s5_out, conv_w, conv_b, lru_w_r, lru_b_r, lru_w_i, lru_b_i, lru_lambda, w_lru_out, w_o, norm_ffn,
           router_group_w, router_group_b, router_expert_w, router_expert_b, expert_w_gate, expert_w_up,
           expert_w_down, norm_final):
    nb, seq, d = x.shape
    t = nb * seq
    assert w_in.shape[0] == 1, "one layer: the final RMSNorm is fused into the layer's combine kernel"
    l = 0
    s5w = s5_d.shape[-1]
    lruw = conv_b.shape[-1]
    rows = nb * TIME_CHUNK

    r = jnp.arange(rows)
    perm = jax.nn.one_hot((r % nb) * TIME_CHUNK + r // nb, rows, dtype=BF16)

    sp = ((-0.5 * LRU_C) * jax.nn.softplus(-lru_lambda[l]))[None, :]
    lrup = (conv_w[l], conv_b[l][None, :], _blockdiag_tiles(0.5 * lru_w_r[l]).astype(BF16),
            _blockdiag_tiles(0.5 * lru_w_i[l]).astype(BF16), 0.5 * lru_b_r[l][None, :],
            0.5 * lru_b_i[l][None, :], sp)
    gate_cols = jnp.arange(w_in.shape[-1]) >= s5w + 2 * lruw
    w_in_b = jnp.where(gate_cols[None, :], 0.5 * w_in[l], w_in[l]).astype(BF16)
    u, a, b, gg, ga, gb = _inproj(x, norm_mix[l][None, :], perm, w_in_b, lrup, s5w, lruw)

    bm, lr, li, cm, gmat = _s5_tables(s5_lam_re[l], s5_lam_im[l], s5_log_dt[l], s5_b_re[l], s5_b_im[l],
                                           s5_c_re[l], s5_c_im[l])
    glu_cols = jnp.arange(w_s5_out.shape[-1]) >= d
    w_s5_out_b = jnp.where(glu_cols[None, :], 0.5 * w_s5_out[l], w_s5_out[l]).astype(BF16)
    s5p = (bm, lr, li, cm, gmat, s5_d[l][None, :], w_s5_out_b)
    mixed = _mixers(u, a, b, gg, ga, gb, s5p, w_lru_out[l].astype(BF16), nb)

    gap = jnp.zeros((SUBLANES - N_GROUPS, d), F32)
    wrt = jnp.concatenate([router_group_w[l].T, gap, router_expert_w[l].T], axis=0)
    wr_hi = wrt.astype(BF16)
    wr_lo = (wrt - wr_hi.astype(F32)).astype(BF16)
    rb = jnp.concatenate([router_group_b[l], jnp.zeros((SUBLANES - N_GROUPS,), F32), router_expert_b[l]])
    rbias = jnp.broadcast_to(rb[:, None], (ROUTER_ROWS, LANES))

    x1, h2, logits = _mixroute(x, mixed, perm.T, w_o[l].astype(BF16), norm_ffn[l][None, :], wr_hi, wr_lo, rbias)
    tri = jnp.triu(jnp.ones((rows, rows), BF16), k=1)
    ltri = jnp.tril(jnp.ones((N_EXPERTS, N_EXPERTS), BF16), k=-1)
    gate, pos, tcnt = _route(logits, tri, ltri)

    n_tiles = seq // TIME_CHUNK
    cnt_t = tcnt.reshape(n_tiles, SUBLANES, LANES)[:, 0, :N_EXPERTS].astype(jnp.int32)
    off_t = jnp.cumsum(cnt_t, axis=1) - cnt_t
    counts = jnp.sum(cnt_t, axis=0)
    padded = ((counts + SLOT_BLOCK - 1) // SLOT_BLOCK) * SLOT_BLOCK
    pad_ends = jnp.cumsum(padded)
    pad_starts = pad_ends - padded
    gst_t = pad_starts[None, :] + jnp.cumsum(cnt_t, axis=0) - cnt_t
    n_blocks = -(-(2 * t) // SLOT_BLOCK) + N_EXPERTS
    n_slots = n_blocks * SLOT_BLOCK
    block_start = jnp.arange(n_blocks, dtype=jnp.int32) * SLOT_BLOCK
    block_e = jnp.minimum(jnp.sum((pad_ends[None, :] <= block_start[:, None]).astype(jnp.int32), axis=1),
                          N_EXPERTS - 1)
    n_used = pad_ends[-1:] // SLOT_BLOCK
    cnt_f, off_f, gst_f = cnt_t.reshape(-1), off_t.reshape(-1), gst_t.reshape(-1)

    xs = _dispatch(cnt_f, off_f, gst_f, counts, pad_starts, pad_ends, h2, pos, rows, n_slots)
    ys = _ffn(block_e, n_used, xs, expert_w_gate[l], expert_w_up[l], expert_w_down[l])
    return _combine(cnt_f, off_f, gst_f, x1, pos, gate, ys, norm_final[None, :], nb, seq)
```
